```python
import jax, jax.numpy as jnp
from jax import lax
import numpy as np

D_MODEL = 1024
BATCH = 2
SEQ = 8192
DEPTH = 4
DEC_BATCH = 16
DEC_SEQ = 64
PAST_LEN = 4096

CHUNK = 64
N_EVEN = (DEPTH + 1) // 2
N_ODD = DEPTH // 2
A_HEADS = 4
A_HEAD_DIM = 128
A_WIDTH = A_HEADS * A_HEAD_DIM
B_HEADS = 8
B_NOPE = 64
B_ROPE = 32
B_VDIM = 64
B_QK = B_NOPE + B_ROPE
B_WIDTH = B_HEADS * B_VDIM
Q_RANK = 384
KV_RANK = 256
ROPE_THETA = 10000.0
Q_BLOCK = 128
MLA_SCALE = B_QK ** -0.5
C_HEADS = 16
C_HEAD_DIM = 64
C_WIDTH = C_HEADS * C_HEAD_DIM
LEFT_CHUNKS = 8
MAX_REL = 128
N_REL = 2 * MAX_REL + 1
C_SCALE = C_HEAD_DIM ** -0.5
BAND_BUF = min(LEFT_CHUNKS * CHUNK, PAST_LEN)
D_FF = ((8 * D_MODEL + 3 * 256 - 1) // (3 * 256)) * 256
IN_AB = 4 * A_WIDTH + 2 * A_HEADS + Q_RANK + KV_RANK + B_ROPE
AB_SPLITS = (A_WIDTH, 2 * A_WIDTH, 3 * A_WIDTH, 4 * A_WIDTH, 4 * A_WIDTH + A_HEADS,
             4 * A_WIDTH + 2 * A_HEADS, 4 * A_WIDTH + 2 * A_HEADS + Q_RANK,
             4 * A_WIDTH + 2 * A_HEADS + Q_RANK + KV_RANK)

kernel_name = 'mlstm_mla_band_streaming_step'


def rmsnorm(x, g, eps=1e-6):
    xf = x.astype(jnp.float32)
    y = xf * lax.rsqrt(jnp.mean(xf * xf, axis=-1, keepdims=True) + eps)
    return (y * g.astype(jnp.float32)).astype(x.dtype)


def rope(x, pos):
    half = x.shape[-1] // 2
    inv = ROPE_THETA ** (-jnp.arange(half, dtype=jnp.float32) / half)
    ang = pos.astype(jnp.float32)[:, None] * inv[None, :]
    bshape = (1, pos.shape[0]) + (1,) * (x.ndim - 3) + (half,)
    cos = jnp.cos(ang).reshape(bshape)
    sin = jnp.sin(ang).reshape(bshape)
    xf = x.astype(jnp.float32)
    x1, x2 = xf[..., :half], xf[..., half:]
    return jnp.concatenate([x1 * cos - x2 * sin, x1 * sin + x2 * cos], axis=-1).astype(x.dtype)


def swiglu(h, wg, wu, wd):
    return (jax.nn.silu(h @ wg) * (h @ wu)) @ wd


def tail_rows(t, n):
    L = t.shape[1]
    if L >= n:
        return t[:, L - n:]
    return jnp.pad(t, ((0, 0), (n - L, 0)) + ((0, 0),) * (t.ndim - 2))


def mlstm_chunk(carry, inp):
    c_mat, n_vec, m_st = carry
    q, k, v, ig, lf = inp
    L = q.shape[2]
    b = jnp.cumsum(lf, axis=-1)
    causal = jnp.tril(jnp.ones((L, L), dtype=bool))
    log_d = jnp.where(causal, b[..., :, None] - b[..., None, :] + ig[..., None, :], -jnp.inf)
    inter = b + m_st[..., None]
    m_t = jnp.maximum(inter, jnp.max(log_d, axis=-1))
    w_inter = jnp.exp(inter - m_t)
    s = jnp.einsum('bhtd,bhsd->bhts', q, k) * jnp.exp(log_d - m_t[..., None])
    num = w_inter[..., None] * jnp.einsum('bhtd,bhde->bhte', q, c_mat) + jnp.einsum('bhts,bhse->bhte', s, v)
    den = w_inter * jnp.einsum('bhtd,bhd->bht', q, n_vec) + jnp.sum(s, axis=-1)
    h = num / jnp.maximum(jnp.abs(den), jnp.exp(-m_t))[..., None]
    b_last = b[..., -1]
    g = b_last[..., None] - b + ig
    m_new = jnp.maximum(b_last + m_st, jnp.max(g, axis=-1))
    decay = jnp.exp(b_last + m_st - m_new)
    wk = jnp.exp(g - m_new[..., None])
    c_new = decay[..., None, None] * c_mat + jnp.einsum('bhs,bhsd,bhse->bhde', wk, k, v)
    n_new = decay[..., None] * n_vec + jnp.einsum('bhs,bhsd->bhd', wk, k)
    return (c_new, n_new, m_new), h


def mlstm_prompt(q, k, v, ig, lf):
    bsz, nh, s_len, dh = q.shape
    nc = s_len // CHUNK

    def to_chunks(t):
        t = t.reshape(t.shape[:2] + (nc, CHUNK) + t.shape[3:])
        return jnp.moveaxis(t, 2, 0)

    init = (jnp.zeros((bsz, nh, dh, dh), jnp.float32),
            jnp.zeros((bsz, nh, dh), jnp.float32),
            jnp.zeros((bsz, nh), jnp.float32))
    final, hs = lax.scan(mlstm_chunk, init,
                         (to_chunks(q), to_chunks(k), to_chunks(v), to_chunks(ig), to_chunks(lf)))
    h = jnp.moveaxis(hs, 0, 2).reshape(bsz, nh, s_len, dh)
    return h, final


def ab_in(h, pos, w_in, b_g, q_norm, kv_norm, w_uq, w_uk):
    bsz, L, _ = h.shape
    z = h @ w_in
    q_a, k_a, v_a, o_a, i_a, f_a, c_q, c_kv, k_r = jnp.split(z, AB_SPLITS, axis=-1)

    def heads(t):
        return t.reshape(bsz, L, A_HEADS, A_HEAD_DIM).transpose(0, 2, 1, 3).astype(jnp.float32)

    qa = heads(q_a)
    ka = heads(k_a) * (A_HEAD_DIM ** -0.5)
    va = heads(v_a)
    ig = (i_a + b_g[:A_HEADS]).astype(jnp.float32).transpose(0, 2, 1)
    lf = jax.nn.log_sigmoid((f_a + b_g[A_HEADS:]).astype(jnp.float32)).transpose(0, 2, 1)
    cq = rmsnorm(c_q, q_norm)
    qb = (cq @ w_uq).reshape(bsz, L, B_HEADS, B_QK)
    q_lat = jnp.einsum('blhn,rhn->blhr', qb[..., :B_NOPE], w_uk)
    q_rope = rope(qb[..., B_NOPE:], pos)
    ckv = rmsnorm(c_kv, kv_norm)
    krope = rope(k_r, pos)
    return (qa, ka, va, ig, lf, o_a), (q_lat, q_rope, ckv, krope)


def ab_out(h_a, o_a, hnorm, o_lat, w_uv, w_out):
    bsz, L, _ = o_a.shape
    ha = rmsnorm(jnp.swapaxes(h_a, 1, 2), hnorm.reshape(A_HEADS, A_HEAD_DIM)).reshape(bsz, L, A_WIDTH)
    ha = (ha * jax.nn.sigmoid(o_a.astype(jnp.float32))).astype(o_a.dtype)
    hb = jnp.einsum('blhr,rhv->blhv', o_lat, w_uv).reshape(bsz, L, B_WIDTH)
    return jnp.concatenate([ha, hb], axis=-1) @ w_out


def mla_attend(q_lat, q_rope, q_pos, ckv, krope, k_pos):
    s = (jnp.einsum('bqhr,bkr->bhqk', q_lat, ckv)
         + jnp.einsum('bqhe,bke->bhqk', q_rope, krope)).astype(jnp.float32) * MLA_SCALE
    visible = (k_pos[None, :] // CHUNK) <= (q_pos[:, None] // CHUNK)
    p = jax.nn.softmax(jnp.where(visible, s, -jnp.inf), axis=-1).astype(ckv.dtype)
    return jnp.einsum('bhqk,bkr->bqhr', p, ckv)


def mla_prompt_attend(q_lat, q_rope, ckv, krope, pos):
    bsz, s_len = q_lat.shape[:2]
    nb = s_len // Q_BLOCK

    def blk(t):
        return jnp.moveaxis(t.reshape((bsz, nb, Q_BLOCK) + t.shape[2:]), 1, 0)

    def one(args):
        ql, qr, qp = args
        return mla_attend(ql, qr, qp, ckv, krope, pos)

    o = lax.map(one, (blk(q_lat), blk(q_rope), pos.reshape(nb, Q_BLOCK)))
    return jnp.moveaxis(o, 0, 1).reshape((bsz, s_len) + o.shape[3:])


def c_in(h, w_qkv):
    bsz, L, _ = h.shape
    z = (h @ w_qkv).reshape(bsz, L, 3, C_HEADS, C_HEAD_DIM)
    return z[:, :, 0], z[:, :, 1], z[:, :, 2]


def band_attend(q, k, v, q_pos, k_pos, rel_bias):
    s = jnp.einsum('bqhd,bkhd->bhqk', q, k).astype(jnp.float32) * C_SCALE
    idx = jnp.clip(q_pos[:, None] - k_pos[None, :], -MAX_REL, MAX_REL) + MAX_REL
    s = s + rel_bias[:, idx].astype(jnp.float32)
    qc = (q_pos // CHUNK)[:, None]
    kc = (k_pos // CHUNK)[None, :]
    valid = (k_pos[None, :] >= 0) & (kc <= qc) & (kc >= qc - LEFT_CHUNKS)
    p = jax.nn.softmax(jnp.where(valid, s, -jnp.inf), axis=-1).astype(v.dtype)
    return jnp.einsum('bhqk,bkhd->bqhd', p, v)


def band_prompt(q, k, v, rel_bias):
    bsz, s_len, nh, dh = q.shape
    nc = s_len // CHUNK
    left = LEFT_CHUNKS * CHUNK
    band = left + CHUNK
    pad = ((0, 0), (left, 0), (0, 0), (0, 0))
    kp = jnp.pad(k, pad)
    vp = jnp.pad(v, pad)

    def one(c):
        start = c * CHUNK
        qc = lax.dynamic_slice_in_dim(q, start, CHUNK, axis=1)
        kc = lax.dynamic_slice_in_dim(kp, start, band, axis=1)
        vc = lax.dynamic_slice_in_dim(vp, start, band, axis=1)
        q_pos = start + jnp.arange(CHUNK, dtype=jnp.int32)
        k_pos = start - left + jnp.arange(band, dtype=jnp.int32)
        return band_attend(qc, kc, vc, q_pos, k_pos, rel_bias)

    o = lax.map(one, jnp.arange(nc, dtype=jnp.int32))
    return jnp.moveaxis(o, 0, 1).reshape(bsz, s_len, nh, dh)


def setup_inputs(seed: int = 0) -> dict:
    key = jax.random.key(seed)
    ks = jax.random.split(key, 40)
    cnt = [0]

    def nrm(shape, scale=1.0):
        k = ks[cnt[0]]
        cnt[0] += 1
        return scale * jax.random.normal(k, shape, jnp.float32)

    x_prompt = nrm((BATCH, SEQ, D_MODEL))
    x_sample = nrm((DEC_BATCH, DEC_SEQ, D_MODEL))
    cache_mla_ckv = nrm((N_EVEN, DEC_BATCH, PAST_LEN, KV_RANK))
    cache_mla_krope = nrm((N_EVEN, DEC_BATCH, PAST_LEN, B_ROPE))
    state_mlstm_C = nrm((N_EVEN, DEC_BATCH, A_HEADS, A_HEAD_DIM, A_HEAD_DIM), 0.3)
    state_mlstm_n = nrm((N_EVEN, DEC_BATCH, A_HEADS, A_HEAD_DIM), 0.3)
    state_mlstm_m = nrm((N_EVEN, DEC_BATCH, A_HEADS), 0.5)
    cache_band_k = nrm((N_ODD, DEC_BATCH, BAND_BUF, C_HEADS, C_HEAD_DIM))
    cache_band_v = nrm((N_ODD, DEC_BATCH, BAND_BUF, C_HEADS, C_HEAD_DIM))
    norm_mix = 1.0 + nrm((DEPTH, D_MODEL), 0.05)
    norm_ffn = 1.0 + nrm((DEPTH, D_MODEL), 0.05)
    norm_final = 1.0 + nrm((D_MODEL,), 0.05)
    w_in_ab = nrm((N_EVEN, D_MODEL, IN_AB), D_MODEL ** -0.5)
    b_i = nrm((N_EVEN, A_HEADS), 0.1)
    b_f = jnp.linspace(3.0, 6.0, A_HEADS, dtype=jnp.float32)[None, :] + nrm((N_EVEN, A_HEADS), 0.1)
    b_gates = jnp.concatenate([b_i, b_f], axis=-1)
    mlstm_hnorm = 1.0 + nrm((N_EVEN, A_WIDTH), 0.05)
    mla_q_norm = 1.0 + nrm((N_EVEN, Q_RANK), 0.05)
    mla_kv_norm = 1.0 + nrm((N_EVEN, KV_RANK), 0.05)
    mla_w_uq = nrm((N_EVEN, Q_RANK, B_HEADS * B_QK), Q_RANK ** -0.5)
    mla_w_uk = nrm((N_EVEN, KV_RANK, B_HEADS, B_NOPE), KV_RANK ** -0.5)
    mla_w_uv = nrm((N_EVEN, KV_RANK, B_HEADS, B_VDIM), KV_RANK ** -0.5)
    w_out_ab = nrm((N_EVEN, A_WIDTH + B_WIDTH, D_MODEL), (A_WIDTH + B_WIDTH) ** -0.5)
    w_qkv_c = nrm((N_ODD, D_MODEL, 3 * C_WIDTH), D_MODEL ** -0.5)
    w_out_c = nrm((N_ODD, C_WIDTH, D_MODEL), C_WIDTH ** -0.5)
    rel_bias_c = nrm((N_ODD, C_HEADS, N_REL), 0.5)
    w_gate = nrm((DEPTH, D_MODEL, D_FF), D_MODEL ** -0.5)
    w_up = nrm((DEPTH, D_MODEL, D_FF), D_MODEL ** -0.5)
    w_down = nrm((DEPTH, D_FF, D_MODEL), D_FF ** -0.5)
    return {'x_prompt': x_prompt, 'x_sample': x_sample,
            'cache_mla_ckv': cache_mla_ckv, 'cache_mla_krope': cache_mla_krope,
            'state_mlstm_C': state_mlstm_C, 'state_mlstm_n': state_mlstm_n, 'state_mlstm_m': state_mlstm_m,
            'cache_band_k': cache_band_k, 'cache_band_v': cache_band_v,
            'norm_mix': norm_mix, 'norm_ffn': norm_ffn, 'norm_final': norm_final,
            'w_in_ab': w_in_ab, 'b_gates': b_gates, 'mlstm_hnorm': mlstm_hnorm,
            'mla_q_norm': mla_q_norm, 'mla_kv_norm': mla_kv_norm,
            'mla_w_uq': mla_w_uq, 'mla_w_uk': mla_w_uk, 'mla_w_uv': mla_w_uv, 'w_out_ab': w_out_ab,
            'w_qkv_c': w_qkv_c, 'w_out_c': w_out_c, 'rel_bias_c': rel_bias_c,
            'w_gate': w_gate, 'w_up': w_up, 'w_down': w_down}


def reference(x_prompt, x_sample, cache_mla_ckv, cache_mla_krope, state_mlstm_C, state_mlstm_n,
              state_mlstm_m, cache_band_k, cache_band_v, norm_mix, norm_ffn, norm_final,
              w_in_ab, b_gates, mlstm_hnorm, mla_q_norm, mla_kv_norm, mla_w_uq, mla_w_uk, mla_w_uv,
              w_out_ab, w_qkv_c, w_out_c, rel_bias_c, w_gate, w_up, w_down):
    pos_p = jnp.arange(SEQ, dtype=jnp.int32)
    pos_s = PAST_LEN + jnp.arange(DEC_SEQ, dtype=jnp.int32)
    pos_sk = jnp.arange(PAST_LEN + DEC_SEQ, dtype=jnp.int32)
    band_pos_s = (PAST_LEN - BAND_BUF) + jnp.arange(BAND_BUF + DEC_SEQ, dtype=jnp.int32)
    xp, xs = x_prompt, x_sample
    p_ckv, p_kr, p_C, p_n, p_m, p_bk, p_bv = [], [], [], [], [], [], []
    s_ckv, s_kr, s_C, s_n, s_m, s_bk, s_bv = [], [], [], [], [], [], []
    for layer in range(DEPTH):
        j = layer // 2
        hp = rmsnorm(xp, norm_mix[layer])
        hs = rmsnorm(xs, norm_mix[layer])
        if layer % 2 == 0:
            w = (w_in_ab[j], b_gates[j], mla_q_norm[j], mla_kv_norm[j], mla_w_uq[j], mla_w_uk[j])
            (qa, ka, va, ig, lf, oa), (ql, qr, ckv, kr) = ab_in(hp, pos_p, *w)
            ha, (c_f, n_f, m_f) = mlstm_prompt(qa, ka, va, ig, lf)
            olat = mla_prompt_attend(ql, qr, ckv, kr, pos_p)
            xp = xp + ab_out(ha, oa, mlstm_hnorm[j], olat, mla_w_uv[j], w_out_ab[j])
            p_ckv.append(ckv)
            p_kr.append(kr)
            p_C.append(c_f.astype(xp.dtype))
            p_n.append(n_f.astype(xp.dtype))
            p_m.append(m_f.astype(xp.dtype))
            (qa, ka, va, ig, lf, oa), (ql, qr, ckv, kr) = ab_in(hs, pos_s, *w)
            init = (state_mlstm_C[j].astype(jnp.float32), state_mlstm_n[j].astype(jnp.float32),
                    state_mlstm_m[j].astype(jnp.float32))
            (c_s, n_s, m_s), ha = mlstm_chunk(init, (qa, ka, va, ig, lf))
            keys_ckv = jnp.concatenate([cache_mla_ckv[j], ckv], axis=1)
            keys_kr = jnp.concatenate([cache_mla_krope[j], kr], axis=1)
            olat = mla_attend(ql, qr, pos_s, keys_ckv, keys_kr, pos_sk)
            xs = xs + ab_out(ha, oa, mlstm_hnorm[j], olat, mla_w_uv[j], w_out_ab[j])
            s_ckv.append(ckv)
            s_kr.append(kr)
            s_C.append(c_s.astype(state_mlstm_C.dtype))
            s_n.append(n_s.astype(state_mlstm_n.dtype))
            s_m.append(m_s.astype(state_mlstm_m.dtype))
        else:
            q, k, v = c_in(hp, w_qkv_c[j])
            o = band_prompt(q, k, v, rel_bias_c[j])
            xp = xp + o.reshape(o.shape[0], o.shape[1], C_WIDTH) @ w_out_c[j]
            p_bk.append(tail_rows(k, BAND_BUF))
            p_bv.append(tail_rows(v, BAND_BUF))
            q, k, v = c_in(hs, w_qkv_c[j])
            kb = jnp.concatenate([cache_band_k[j], k], axis=1)
            vb = jnp.concatenate([cache_band_v[j], v], axis=1)
            o = band_attend(q, kb, vb, pos_s, band_pos_s, rel_bias_c[j])
            xs = xs + o.reshape(o.shape[0], o.shape[1], C_WIDTH) @ w_out_c[j]
            s_bk.append(kb[:, -BAND_BUF:])
            s_bv.append(vb[:, -BAND_BUF:])
        xp = xp + swiglu(rmsnorm(xp, norm_ffn[layer]), w_gate[layer], w_up[layer], w_down[layer])
        xs = xs + swiglu(rmsnorm(xs, norm_ffn[layer]), w_gate[layer], w_up[layer], w_down[layer])
    y_prompt = rmsnorm(xp, norm_final)
    y_sample = rmsnorm(xs, norm_final)
    return (y_prompt, y_sample,
            jnp.stack(p_ckv), jnp.stack(p_kr), jnp.stack(p_C), jnp.stack(p_n), jnp.stack(p_m),
            jnp.stack(p_bk), jnp.stack(p_bv),
            jnp.stack(s_ckv), jnp.stack(s_kr), jnp.stack(s_C), jnp.stack(s_n), jnp.stack(s_m),
            jnp.stack(s_bk), jnp.stack(s_bv))
```

```python
import functools
import math

import jax
import jax.numpy as jnp
from jax import lax
from jax.experimental import pallas as pl
from jax.experimental.pallas import tpu as pltpu

F32 = jnp.float32
BF16 = jnp.bfloat16

D_MODEL = 1024
CHUNK = 64
A_HEADS = 4
A_HEAD_DIM = 128
A_WIDTH = A_HEADS * A_HEAD_DIM
B_HEADS = 8
B_NOPE = 64
B_ROPE = 32
B_VDIM = 64
B_QK = B_NOPE + B_ROPE
B_WIDTH = B_HEADS * B_VDIM
Q_RANK = 384
KV_RANK = 256
ROPE_THETA = 10000.0
MLA_SCALE = B_QK ** -0.5
C_HEADS = 16
C_HEAD_DIM = 64
C_WIDTH = C_HEADS * C_HEAD_DIM
LEFT_CHUNKS = 8
MAX_REL = 128
C_SCALE = C_HEAD_DIM ** -0.5
EPS = 1e-6
NEG = -1e30

LANES = 128
MLA_HEAD_PAD = 128
MLA_KV_BLOCK = 512
BAND_TILE = 128
BAND_LEFT = LEFT_CHUNKS * CHUNK
BAND_WINDOW = BAND_LEFT + BAND_TILE
BAND_ROLL_WIDTH = BAND_WINDOW + BAND_TILE
VMEM_LIMIT = 56 * 1024 * 1024

SM_CQ = 0
SM_CKV = Q_RANK
SM_KR = Q_RANK + KV_RANK
SM_WIDTH = SM_KR + LANES
GATE_I_LANE = B_ROPE
GATE_F_LANE = B_ROPE + A_HEADS


def _const_spec(shape):
    zeros = (0,) * len(shape)
    return pl.BlockSpec(shape, lambda *_: zeros, pipeline_mode=pl.Buffered(1))


def _params(semantics):
    return pltpu.CompilerParams(dimension_semantics=semantics, vmem_limit_bytes=VMEM_LIMIT)


def _rms(x, g):
    return x * lax.rsqrt(jnp.mean(x * x, axis=-1, keepdims=True) + EPS) * g


def _dot(a, b):
    return jnp.dot(a, b, preferred_element_type=F32)


def _dot_nt(a, b):
    return lax.dot_general(a, b, (((1,), (1,)), ((), ())), preferred_element_type=F32)


def _dot_tn(a, b):
    return lax.dot_general(a, b, (((0,), (0,)), ((), ())), preferred_element_type=F32)


def _norm_proj_body(x_ref, g_ref, w_ref, *out_refs, plan):
    h = _rms(x_ref[...], g_ref[...]).astype(BF16)
    for w0, width, dests in plan:
        z = _dot(h, w_ref[:, w0:w0 + width])
        for out_idx, o0, scale in dests:
            o_ref = out_refs[out_idx]
            zz = z if scale == 1.0 else z * scale
            o_ref[:, o0:o0 + width] = zz.astype(o_ref.dtype)


def norm_proj(x, g, w, plan, out_widths, out_dtypes, tm):
    t, d = x.shape
    n = w.shape[1]
    return pl.pallas_call(
        functools.partial(_norm_proj_body, plan=plan),
        grid=(t // tm,),
        in_specs=[pl.BlockSpec((tm, d), lambda i: (i, 0)), _const_spec((1, d)), _const_spec((d, n))],
        out_specs=[pl.BlockSpec((tm, ow), lambda i: (i, 0)) for ow in out_widths],
        out_shape=[jax.ShapeDtypeStruct((t, ow), dt) for ow, dt in zip(out_widths, out_dtypes)],
        compiler_params=_params(("parallel",)),
        name="norm_proj",
    )(x, g, w)


def _mix_ffn_body(*refs, n_mix, ff_chunks, final):
    x_ref = refs[0]
    a_refs = refs[1:1 + n_mix]
    wo_refs = refs[1 + n_mix:1 + 2 * n_mix]
    g_ref, wg_ref, wu_ref, wd_ref = refs[1 + 2 * n_mix:5 + 2 * n_mix]
    pos = 5 + 2 * n_mix
    gf_ref = refs[pos] if final else None
    o_ref = refs[pos + (1 if final else 0)]
    act_ref = refs[-1]
    x = x_ref[...]
    for a_ref, wo_ref in zip(a_refs, wo_refs):
        x = x + _dot(a_ref[...], wo_ref[...])
    h = _rms(x, g_ref[...]).astype(BF16)
    for c0, cw in ff_chunks:
        gate = _dot(h, wg_ref[:, c0:c0 + cw])
        up = _dot(h, wu_ref[:, c0:c0 + cw])
        act_ref[:, c0:c0 + cw] = (gate * jax.nn.sigmoid(gate) * up).astype(BF16)
    y = x + _dot(act_ref[...], wd_ref[...])
    if final:
        y = _rms(y, gf_ref[...])
    o_ref[...] = y


def mix_ffn(x, mix_in, mix_w, g, wg, wu, wd, g_final, tm):
    t, d = x.shape
    f = wg.shape[1]
    chunk = 512
    ff_chunks = [(c0, min(chunk, f - c0)) for c0 in range(0, f, chunk)]
    final = g_final is not None
    in_specs = [pl.BlockSpec((tm, d), lambda i: (i, 0))]
    in_specs += [pl.BlockSpec((tm, a.shape[1]), lambda i: (i, 0)) for a in mix_in]
    in_specs += [_const_spec(w.shape) for w in mix_w]
    in_specs += [_const_spec((1, d)), _const_spec((d, f)), _const_spec((d, f)), _const_spec((f, d))]
    args = [x, *mix_in, *mix_w, g, wg, wu, wd]
    if final:
        in_specs.append(_const_spec((1, d)))
        args.append(g_final)
    return pl.pallas_call(
        functools.partial(_mix_ffn_body, n_mix=len(mix_in), ff_chunks=ff_chunks, final=final),
        grid=(t // tm,),
        in_specs=in_specs,
        out_specs=pl.BlockSpec((tm, d), lambda i: (i, 0)),
        out_shape=jax.ShapeDtypeStruct((t, d), F32),
        scratch_shapes=[pltpu.VMEM((tm, f), BF16)],
        compiler_params=_params(("parallel",)),
        name="mix_ffn",
    )(*args)


def _mla_prep_q_body(sm_ref, qn_ref, kvn_ref, wq_ref, rc_ref, rs1_ref, rs2_ref, cos_t_ref, sin_t_ref,
                     ckv_ref, kr_ref, qt_ref):
    sm = sm_ref[...]
    cq = _rms(sm[:, SM_CQ:SM_CQ + Q_RANK], qn_ref[...]).astype(BF16)
    ckv_ref[...] = _rms(sm[:, SM_CKV:SM_CKV + KV_RANK], kvn_ref[...])
    grp = sm[:, SM_KR:SM_KR + LANES]
    half = B_ROPE // 2
    rot = (grp * rc_ref[...] + pltpu.roll(grp, half, 1) * rs1_ref[...]
           + pltpu.roll(grp, LANES - half, 1) * rs2_ref[...])
    kr_ref[...] = rot[:, :B_ROPE]
    qt = _dot_nt(wq_ref[...], cq) * MLA_SCALE
    cos_t = cos_t_ref[...]
    sin_t = sin_t_ref[...]
    for h in range(B_HEADS):
        r0 = h * MLA_HEAD_PAD
        x1 = qt[r0 + B_NOPE:r0 + B_NOPE + half]
        x2 = qt[r0 + B_NOPE + half:r0 + B_QK]
        qt_ref[0, r0:r0 + B_NOPE, :] = qt[r0:r0 + B_NOPE].astype(BF16)
        rot_q = jnp.concatenate([x1 * cos_t - x2 * sin_t, x1 * sin_t + x2 * cos_t], axis=0)
        qt_ref[0, r0 + B_NOPE:r0 + B_QK, :] = rot_q.astype(BF16)
        qt_ref[0, r0 + B_QK:r0 + MLA_HEAD_PAD, :] = qt[r0 + B_QK:r0 + MLA_HEAD_PAD].astype(BF16)


def mla_prep_q(small, qn, kvn, wq_t, rope_tabs, tab_index, tm):
    t = small.shape[0]
    rc, rs1, rs2, cos_t, sin_t = rope_tabs
    half = B_ROPE // 2
    row_tab = pl.BlockSpec((tm, LANES), lambda i: (tab_index(i), 0))
    col_tab = pl.BlockSpec((half, tm), lambda i: (0, tab_index(i)))
    return pl.pallas_call(
        _mla_prep_q_body,
        grid=(t // tm,),
        in_specs=[pl.BlockSpec((tm, SM_WIDTH), lambda i: (i, 0)), _const_spec(qn.shape), _const_spec(kvn.shape),
                  _const_spec(wq_t.shape), row_tab, row_tab, row_tab, col_tab, col_tab],
        out_specs=[pl.BlockSpec((tm, KV_RANK), lambda i: (i, 0)), pl.BlockSpec((tm, B_ROPE), lambda i: (i, 0)),
                   pl.BlockSpec((1, B_HEADS * MLA_HEAD_PAD, tm), lambda i: (i, 0, 0))],
        out_shape=[jax.ShapeDtypeStruct((t, KV_RANK), F32), jax.ShapeDtypeStruct((t, B_ROPE), F32),
                   jax.ShapeDtypeStruct((t // tm, B_HEADS * MLA_HEAD_PAD, tm), BF16)],
        compiler_params=_params(("parallel",)),
        name="mla_prep_q",
    )(small, qn, kvn, wq_t, rc, rs1, rs2, cos_t, sin_t)


def _mla_prep_kv_body(ckv_ref, kr_ref, wk_ref, place_ref, wv_ref, kp_ref, vt_ref):
    c = ckv_ref[...].astype(BF16)
    kr = kr_ref[...].astype(BF16)
    kp_ref[...] = (_dot(c, wk_ref[...]) + _dot(kr, place_ref[...])).astype(BF16)
    vt_ref[0] = _dot_nt(wv_ref[...], c).astype(BF16)


def mla_prep_kv(ckv, krope, wk_pad, place, wv_t, n_tiles, tm):
    kw = B_HEADS * MLA_HEAD_PAD
    return pl.pallas_call(
        _mla_prep_kv_body,
        grid=(n_tiles,),
        in_specs=[pl.BlockSpec((tm, KV_RANK), lambda i: (i, 0)), pl.BlockSpec((tm, B_ROPE), lambda i: (i, 0)),
                  _const_spec(wk_pad.shape), _const_spec(place.shape), _const_spec(wv_t.shape)],
        out_specs=[pl.BlockSpec((tm, kw), lambda i: (i, 0)), pl.BlockSpec((1, B_WIDTH, tm), lambda i: (i, 0, 0))],
        out_shape=[jax.ShapeDtypeStruct((n_tiles * tm, kw), BF16),
                   jax.ShapeDtypeStruct((n_tiles, B_WIDTH, tm), BF16)],
        compiler_params=_params(("parallel",)),
        name="mla_prep_kv",
    )(ckv, krope, wk_pad, place, wv_t)


def _mla_attn_body(qt_ref, kp_ref, vt_ref, o_ref, m_ref, l_ref, acc_ref, *, tq, q0):
    tk = MLA_KV_BLOCK
    i = pl.program_id(1)
    n_full = (q0 + i * tq) // tk
    m_ref[...] = jnp.full(m_ref.shape, NEG, F32)
    l_ref[...] = jnp.zeros(l_ref.shape, F32)
    acc_ref[...] = jnp.zeros(acc_ref.shape, F32)
    key_chunk = lax.broadcasted_iota(jnp.int32, (tk, tq), 0) // CHUNK
    qry_chunk = lax.broadcasted_iota(jnp.int32, (tk, tq), 1) // CHUNK
    visible = key_chunk <= qry_chunk

    def step(j, masked):
        row0 = pl.multiple_of(j * tk, tk)
        for h in range(B_HEADS):
            k_h = kp_ref[pl.ds(row0, tk), h * MLA_HEAD_PAD:(h + 1) * MLA_HEAD_PAD]
            q_h = qt_ref[0, h * MLA_HEAD_PAD:(h + 1) * MLA_HEAD_PAD, :]
            s = _dot(k_h, q_h)
            if masked:
                s = jnp.where(visible, s, NEG)
            m_prev = m_ref[h:h + 1, :]
            m_new = jnp.maximum(m_prev, jnp.max(s, axis=0, keepdims=True))
            alpha = jnp.exp(m_prev - m_new)
            p = jnp.exp(s - m_new)
            l_ref[h:h + 1, :] = alpha * l_ref[h:h + 1, :] + jnp.sum(p, axis=0, keepdims=True)
            m_ref[h:h + 1, :] = m_new
            v_h = vt_ref[j, h * B_VDIM:(h + 1) * B_VDIM, :]
            rows = slice(h * B_VDIM, (h + 1) * B_VDIM)
            acc_ref[rows, :] = alpha * acc_ref[rows, :] + _dot(v_h, p.astype(BF16))

    def full_step(j, carry):
        step(j, False)
        return carry

    lax.fori_loop(0, n_full, full_step, 0)
    step(n_full, True)
    for h in range(B_HEADS):
        rows = slice(h * B_VDIM, (h + 1) * B_VDIM)
        acc_ref[rows, :] = acc_ref[rows, :] / l_ref[h:h + 1, :]
    o_ref[...] = jnp.transpose(acc_ref[...]).astype(o_ref.dtype)


def mla_attn(qt, kp, vt, groups, nq, tq, n_blocks, q0):
    tk = MLA_KV_BLOCK
    kw = B_HEADS * MLA_HEAD_PAD
    return pl.pallas_call(
        functools.partial(_mla_attn_body, tq=tq, q0=q0),
        grid=(groups, nq),
        in_specs=[pl.BlockSpec((1, kw, tq), lambda g, i: (g * nq + i, 0, 0)),
                  pl.BlockSpec((n_blocks * tk, kw), lambda g, i: (g, 0), pipeline_mode=pl.Buffered(1)),
                  pl.BlockSpec((n_blocks, B_WIDTH, tk), lambda g, i: (g, 0, 0), pipeline_mode=pl.Buffered(1))],
        out_specs=pl.BlockSpec((tq, B_WIDTH), lambda g, i: (g * nq + i, 0)),
        out_shape=jax.ShapeDtypeStruct((groups * nq * tq, B_WIDTH), BF16),
        scratch_shapes=[pltpu.VMEM((B_HEADS, tq), F32), pltpu.VMEM((B_HEADS, tq), F32),
                        pltpu.VMEM((B_WIDTH, tq), F32)],
        compiler_params=_params(("parallel", "arbitrary")),
        name="mla_attn",
    )(qt, kp, vt)


def _scan_rows(x, op, fill, length):
    row = lax.broadcasted_iota(jnp.int32, x.shape, 0)
    shift = 1
    while shift < length:
        moved = pltpu.roll(x, shift, 0)
        x = op(x, jnp.where(row >= shift, moved, fill))
        shift *= 2
    return x


def _mlstm_body(q_ref, k_ref, v_ref, o_ref, gt_ref, bias_ref, hn_ref, c0_ref, m0_ref,
                ha_ref, c_out_ref, m_out_ref, c_scr, m_scr, *, blk):
    dh = A_HEAD_DIM
    step_idx = pl.program_id(1)

    @pl.when(step_idx == 0)
    def _():
        c_scr[...] = c0_ref[0]
        m_scr[...] = m0_ref[0]

    gates = gt_ref[...] + bias_ref[...]
    log_f = jax.nn.log_sigmoid(gates)
    b_all = pltpu.roll(_scan_rows(log_f, jnp.add, 0.0, blk), LANES - A_HEADS, 1)
    a_all = gates - b_all
    amax_all = _scan_rows(a_all, jnp.maximum, NEG, blk)
    pad = max(blk, LANES) - blk
    a_sq = a_all if pad == 0 else jnp.concatenate([a_all, jnp.zeros((pad, LANES), F32)], axis=0)
    a_rows = jnp.transpose(a_sq)
    causal = (lax.broadcasted_iota(jnp.int32, (blk, blk), 0) >= lax.broadcasted_iota(jnp.int32, (blk, blk), 1))
    ones_col = (lax.broadcasted_iota(jnp.int32, (blk, dh), 1) == 0).astype(BF16)
    for h in range(A_HEADS):
        lane = GATE_I_LANE + h
        cols = slice(h * dh, (h + 1) * dh)
        q = q_ref[:, cols]
        k = k_ref[:, cols]
        v_ext = jnp.concatenate([v_ref[:, cols], ones_col], axis=1)
        a_col = a_all[:, lane:lane + 1]
        b_col = b_all[:, lane:lane + 1]
        a_row = a_rows[lane:lane + 1, :blk]
        m_prev = m_scr[h:h + 1, 0:1]
        run_max = jnp.maximum(amax_all[:, lane:lane + 1], m_prev)
        decay_mat = jnp.exp(jnp.where(causal, a_row - run_max, NEG))
        p = (_dot_nt(q, k) * decay_mat).astype(BF16)
        state = c_scr[h]
        w_inter = jnp.exp(m_prev - run_max)
        numden = w_inter * _dot(q, state.astype(BF16)) + _dot(p, v_ext)
        den = numden[:, dh:dh + 1]
        hh = numden[:, :dh] / jnp.maximum(jnp.abs(den), jnp.exp(-(b_col + run_max)))
        hh = _rms(hh, hn_ref[:, cols])
        ha_ref[:, cols] = (hh * jax.nn.sigmoid(o_ref[:, cols])).astype(ha_ref.dtype)
        max_last = run_max[blk - 1:blk, :]
        w_k = jnp.exp(a_col - max_last)
        k_w = (k.astype(F32) * w_k).astype(BF16)
        c_scr[h] = jnp.exp(m_prev - max_last) * state + _dot_tn(k_w, v_ext)
        m_scr[h:h + 1, :] = jnp.broadcast_to(b_col[blk - 1:blk, :] + max_last, (1, LANES))

    @pl.when(step_idx == pl.num_programs(1) - 1)
    def _():
        c_out_ref[0] = c_scr[...]
        m_out_ref[0] = m_scr[...]


def mlstm(qkv, o32, small, bias, hnorm, c0, m0, groups, steps, blk, row0):
    base = row0 // blk
    gate_block = SM_KR // LANES

    def rows(col):
        return lambda g, s: (base + g * steps + s, col)

    return pl.pallas_call(
        functools.partial(_mlstm_body, blk=blk),
        grid=(groups, steps),
        in_specs=[pl.BlockSpec((blk, A_WIDTH), rows(0)), pl.BlockSpec((blk, A_WIDTH), rows(1)),
                  pl.BlockSpec((blk, A_WIDTH), rows(2)), pl.BlockSpec((blk, A_WIDTH), rows(0)),
                  pl.BlockSpec((blk, LANES), rows(gate_block)), _const_spec((1, LANES)), _const_spec((1, A_WIDTH)),
                  pl.BlockSpec((1, A_HEADS, A_HEAD_DIM, 2 * A_HEAD_DIM), lambda g, s: (g, 0, 0, 0)),
                  pl.BlockSpec((1, 8, LANES), lambda g, s: (g, 0, 0))],
        out_specs=[pl.BlockSpec((blk, A_WIDTH), lambda g, s: (g * steps + s, 0)),
                   pl.BlockSpec((1, A_HEADS, A_HEAD_DIM, 2 * A_HEAD_DIM), lambda g, s: (g, 0, 0, 0)),
                   pl.BlockSpec((1, 8, LANES), lambda g, s: (g, 0, 0))],
        out_shape=[jax.ShapeDtypeStruct((groups * steps * blk, A_WIDTH), BF16),
                   jax.ShapeDtypeStruct((groups, A_HEADS, A_HEAD_DIM, 2 * A_HEAD_DIM), F32),
                   jax.ShapeDtypeStruct((groups, 8, LANES), F32)],
        scratch_shapes=[pltpu.VMEM((A_HEADS, A_HEAD_DIM, 2 * A_HEAD_DIM), F32), pltpu.VMEM((8, LANES), F32)],
        compiler_params=_params(("parallel", "arbitrary")),
        name="mlstm",
    )(qkv, qkv, qkv, o32, small, bias, hnorm, c0, m0)


def _band_body(q_ref, *refs, clamp_start):
    nb = BAND_WINDOW // BAND_TILE
    k_refs = refs[:nb]
    v_refs = refs[nb:2 * nb]
    base_ref = refs[2 * nb]
    o_ref = refs[2 * nb + 1]
    tab_ref = refs[2 * nb + 2]
    tq = BAND_TILE
    i = pl.program_id(1)

    @pl.when(i == 0)
    def _():
        shape = (tq, BAND_ROLL_WIDTH)
        row = lax.broadcasted_iota(jnp.int32, shape, 0)
        win = lax.broadcasted_iota(jnp.int32, (tq, BAND_WINDOW), 1) // CHUNK
        qch = lax.broadcasted_iota(jnp.int32, (tq, BAND_WINDOW), 0) // CHUNK
        valid = (win >= qch) & (win <= qch + LEFT_CHUNKS)
        for h in range(C_HEADS):
            tab = jnp.broadcast_to(base_ref[h:h + 1, :], shape)
            shift = 1
            while shift < tq:
                tab = jnp.where((row & shift) != 0, pltpu.roll(tab, shift, 1), tab)
                shift *= 2
            tab_ref[h] = jnp.where(valid, tab[:, :BAND_WINDOW], NEG)

    lane_half = lax.broadcasted_iota(jnp.int32, (tq, LANES), 1) // C_HEAD_DIM
    for pair in range(C_HEADS // 2):
        cols = slice(pair * LANES, (pair + 1) * LANES)
        q_pair = q_ref[:, cols]
        outs = []
        for half in range(2):
            h = 2 * pair + half
            q_h = jnp.where(lane_half == half, q_pair, jnp.zeros_like(q_pair))
            parts = []
            for j in range(nb):
                s_j = _dot_nt(q_h, k_refs[j][:, cols])
                if clamp_start and j < nb - 1:
                    s_j = s_j + jnp.where(i + j < nb - 1, NEG, 0.0)
                parts.append(s_j)
            s = jnp.concatenate(parts, axis=1) + tab_ref[h]
            m = jnp.max(s, axis=1, keepdims=True)
            p = jnp.exp(s - m)
            l = jnp.sum(p, axis=1, keepdims=True)
            pb = p.astype(BF16)
            acc = _dot(pb[:, :BAND_TILE], v_refs[0][:, cols])
            for j in range(1, nb):
                acc = acc + _dot(pb[:, j * BAND_TILE:(j + 1) * BAND_TILE], v_refs[j][:, cols])
            outs.append(acc / l)
        o_ref[:, cols] = jnp.where(lane_half == 0, outs[0], outs[1]).astype(o_ref.dtype)


def band_attn(q_arr, k_arr, v_arr, cols, base_rows, groups, tiles, clamp_start, q_row0, kv_rows_per_group):
    tq = BAND_TILE
    nb = BAND_WINDOW // tq
    qb = q_row0 // tq
    kvb = kv_rows_per_group // tq

    def kv_spec(col, j):
        if clamp_start:
            return pl.BlockSpec((tq, C_WIDTH), lambda g, i: (qb + g * kvb + jnp.maximum(i + j - (nb - 1), 0), col))
        return pl.BlockSpec((tq, C_WIDTH), lambda g, i: (g * kvb + j, col))

    in_specs = [pl.BlockSpec((tq, C_WIDTH), lambda g, i: (qb + g * tiles + i, cols[0]))]
    in_specs += [kv_spec(cols[1], j) for j in range(nb)]
    in_specs += [kv_spec(cols[2], j) for j in range(nb)]
    in_specs.append(_const_spec(base_rows.shape))
    return pl.pallas_call(
        functools.partial(_band_body, clamp_start=clamp_start),
        grid=(groups, tiles),
        in_specs=in_specs,
        out_specs=pl.BlockSpec((tq, C_WIDTH), lambda g, i: (g * tiles + i, 0)),
        out_shape=jax.ShapeDtypeStruct((groups * tiles * tq, C_WIDTH), BF16),
        scratch_shapes=[pltpu.VMEM((C_HEADS, tq, BAND_WINDOW), F32)],
        compiler_params=_params(("parallel", "arbitrary")),
        name="band_attn",
    )(q_arr, *([k_arr] * nb), *([v_arr] * nb), base_rows)


def _rope_tables(positions):
    half = B_ROPE // 2
    inv = ROPE_THETA ** (-jnp.arange(half, dtype=F32) / half)
    ang = positions.astype(F32)[:, None] * inv[None, :]
    cos, sin = jnp.cos(ang), jnp.sin(ang)
    n = positions.shape[0]
    zeros = jnp.zeros((n, LANES - B_ROPE), F32)
    zh = jnp.zeros((n, half), F32)
    rc = jnp.concatenate([cos, cos, zeros], axis=1)
    rs1 = jnp.concatenate([zh, sin, zeros], axis=1)
    rs2 = jnp.concatenate([-sin, zh, zeros], axis=1)
    return rc, rs1, rs2, cos.T, sin.T


def _band_base_rows(rel_bias):
    x = jnp.arange(BAND_ROLL_WIDTH)
    rel = jnp.where(x < BAND_WINDOW, BAND_LEFT - x, BAND_LEFT + 1)
    idx = jnp.clip(rel, -MAX_REL, MAX_REL) + MAX_REL
    return rel_bias[:, idx]


def kernel(x_prompt, x_sample, cache_mla_ckv, cache_mla_krope, state_mlstm_C, state_mlstm_n, state_mlstm_m,
           cache_band_k, cache_band_v, norm_mix, norm_ffn, norm_final, w_in_ab, b_gates, mlstm_hnorm,
           mla_q_norm, mla_kv_norm, mla_w_uq, mla_w_uk, mla_w_uv, w_out_ab, w_qkv_c, w_out_c, rel_bias_c,
           w_gate, w_up, w_down):
    batch, seq, d = x_prompt.shape
    dec_batch, dec_seq, _ = x_sample.shape
    past = cache_mla_ckv.shape[2]
    band_buf = cache_band_k.shape[2]
    depth = norm_mix.shape[0]
    tp = batch * seq
    ts = dec_batch * dec_seq
    t = tp + ts
    tm_big = 1024
    tm = MLA_KV_BLOCK
    assert d == D_MODEL and dec_seq == CHUNK and band_buf == BAND_LEFT and past % MLA_KV_BLOCK == 0
    assert tp % tm_big == 0 and ts % tm_big == 0 and seq % tm == 0 and tm % dec_seq == 0

    x = jnp.concatenate([x_prompt.reshape(tp, d), x_sample.reshape(ts, d)], axis=0)

    pos_tab = jnp.concatenate([jnp.arange(seq, dtype=jnp.int32),
                               past + (jnp.arange(tm, dtype=jnp.int32) % dec_seq)])
    rope_tabs = _rope_tables(pos_tab)
    p_tiles = tp // tm
    seq_tiles = seq // tm

    def tab_index(i):
        return jnp.where(i < p_tiles, i % seq_tiles, seq_tiles)

    outs = {k: [] for k in ("p_ckv", "p_kr", "p_C", "p_n", "p_m", "p_bk", "p_bv",
                            "s_ckv", "s_kr", "s_C", "s_n", "s_m", "s_bk", "s_bv")}
    a4 = 4 * A_WIDTH
    for layer in range(depth):
        j = layer // 2
        g_mix = norm_mix[layer][None, :]
        g_ffn = norm_ffn[layer][None, :]
        g_fin = norm_final[None, :] if layer == depth - 1 else None
        ffn_w = (w_gate[layer].astype(BF16), w_up[layer].astype(BF16), w_down[layer].astype(BF16))
        if layer % 2 == 0:
            w = w_in_ab[j]
            gate_cols = jnp.concatenate([w[:, a4 + 2 * A_HEADS + Q_RANK + KV_RANK:], w[:, a4:a4 + 2 * A_HEADS],
                                         jnp.zeros((d, LANES - B_ROPE - 2 * A_HEADS), F32)], axis=1)
            w_all = jnp.concatenate([w[:, :a4], w[:, a4 + 2 * A_HEADS:a4 + 2 * A_HEADS + Q_RANK + KV_RANK],
                                     gate_cols], axis=1).astype(BF16)
            aw = A_WIDTH
            plan = [(0, aw, [(0, 0, 1.0)]), (aw, aw, [(0, aw, A_HEAD_DIM ** -0.5)]), (2 * aw, aw, [(0, 2 * aw, 1.0)]),
                    (3 * aw, aw, [(1, 0, 1.0)]), (4 * aw, Q_RANK, [(2, SM_CQ, 1.0)]),
                    (4 * aw + Q_RANK, KV_RANK + LANES, [(2, SM_CKV, 1.0)])]
            qkv, o32, small = norm_proj(x, g_mix, w_all, plan, (3 * aw, aw, SM_WIDTH), (BF16, F32, F32), tm_big)

            wq = mla_w_uq[j].reshape(Q_RANK, B_HEADS, B_QK)
            wq_t = jnp.pad(wq, ((0, 0), (0, 0), (0, MLA_HEAD_PAD - B_QK))).reshape(Q_RANK, -1).T.astype(BF16)
            ckv, krope, qt = mla_prep_q(small, mla_q_norm[j][None, :], mla_kv_norm[j][None, :], wq_t,
                                        rope_tabs, tab_index, tm)
            wk_pad = jnp.pad(mla_w_uk[j], ((0, 0), (0, 0), (0, MLA_HEAD_PAD - B_NOPE))).reshape(KV_RANK, -1).astype(BF16)
            place = jnp.pad(jnp.eye(B_ROPE, dtype=F32), ((0, 0), (B_NOPE, MLA_HEAD_PAD - B_QK)))
            place = jnp.tile(place, (1, B_HEADS)).astype(BF16)
            wv_t = mla_w_uv[j].reshape(KV_RANK, B_WIDTH).T.astype(BF16)
            kp_p, vt_p = mla_prep_kv(ckv, krope, wk_pad, place, wv_t, p_tiles, tm)
            ckv_s = ckv[tp:].reshape(dec_batch, dec_seq, KV_RANK)
            kr_s = krope[tp:].reshape(dec_batch, dec_seq, B_ROPE)
            fill = tm - dec_seq
            lat_s = jnp.concatenate([cache_mla_ckv[j], ckv_s, jnp.zeros((dec_batch, fill, KV_RANK), F32)], axis=1)
            lkr_s = jnp.concatenate([cache_mla_krope[j], kr_s, jnp.zeros((dec_batch, fill, B_ROPE), F32)], axis=1)
            s_blocks = (past + tm) // tm
            kp_s, vt_s = mla_prep_kv(lat_s.reshape(-1, KV_RANK), lkr_s.reshape(-1, B_ROPE), wk_pad, place, wv_t,
                                     dec_batch * s_blocks, tm)

            hb_p = mla_attn(qt, kp_p, vt_p, batch, seq_tiles, tm, seq_tiles, 0)
            qt_s = qt[p_tiles:].reshape(ts // tm, -1, tm // dec_seq, dec_seq)
            qt_s = jnp.moveaxis(qt_s, 2, 1).reshape(dec_batch, -1, dec_seq)
            qt_s = jnp.pad(qt_s, ((0, 0), (0, 0), (0, LANES - dec_seq)))
            hb_s = mla_attn(qt_s, kp_s, vt_s, dec_batch, 1, LANES, s_blocks, past)
            hb_s = hb_s.reshape(dec_batch, LANES, B_WIDTH)[:, :dec_seq].reshape(ts, B_WIDTH)

            bias = jnp.zeros((1, LANES), F32).at[0, GATE_I_LANE:GATE_I_LANE + 2 * A_HEADS].set(b_gates[j])
            hn = mlstm_hnorm[j][None, :]
            c0_p = jnp.zeros((batch, A_HEADS, A_HEAD_DIM, 2 * A_HEAD_DIM), F32)
            m0_p = jnp.zeros((batch, 8, LANES), F32)
            blk_p = 256
            ha_p, c_p, m_p = mlstm(qkv, o32, small, bias, hn, c0_p, m0_p, batch, seq // blk_p, blk_p, 0)
            c0_s = jnp.concatenate([state_mlstm_C[j], state_mlstm_n[j][..., None],
                                    jnp.zeros((dec_batch, A_HEADS, A_HEAD_DIM, A_HEAD_DIM - 1), F32)], axis=-1)
            m0_s = jnp.broadcast_to(jnp.pad(state_mlstm_m[j], ((0, 0), (0, 8 - A_HEADS)))[..., None],
                                    (dec_batch, 8, LANES))
            ha_s, c_s, m_s = mlstm(qkv, o32, small, bias, hn, c0_s, m0_s, dec_batch, 1, dec_seq, tp)

            ha = jnp.concatenate([ha_p, ha_s], axis=0)
            hb = jnp.concatenate([hb_p, hb_s], axis=0)
            wo = w_out_ab[j].astype(BF16)
            x = mix_ffn(x, (ha, hb), (wo[:A_WIDTH], wo[A_WIDTH:]), g_ffn, *ffn_w, g_fin, tm_big)

            outs["p_ckv"].append(ckv[:tp].reshape(batch, seq, KV_RANK))
            outs["p_kr"].append(krope[:tp].reshape(batch, seq, B_ROPE))
            outs["p_C"].append(c_p[..., :A_HEAD_DIM])
            outs["p_n"].append(c_p[..., A_HEAD_DIM])
            outs["p_m"].append(m_p[:, :A_HEADS, 0])
            outs["s_ckv"].append(ckv_s)
            outs["s_kr"].append(kr_s)
            outs["s_C"].append(c_s[..., :A_HEAD_DIM])
            outs["s_n"].append(c_s[..., A_HEAD_DIM])
            outs["s_m"].append(m_s[:, :A_HEADS, 0])
        else:
            cw = C_WIDTH
            w_all = w_qkv_c[j].astype(BF16)
            plan = [(0, 512, [(0, 0, C_SCALE)]), (512, 512, [(0, 512, C_SCALE)])]
            plan += [(c0, 512, [(0, c0, 1.0), (1, c0 - cw, 1.0)]) for c0 in range(cw, 3 * cw, 512)]
            qkv, kv32 = norm_proj(x, g_mix, w_all, plan, (3 * cw, 2 * cw), (BF16, F32), tm_big)
            base_rows = _band_base_rows(rel_bias_c[j])
            o_p = band_attn(qkv, qkv, qkv, (0, 1, 2), base_rows, batch, seq // BAND_TILE, True, 0, seq)

            k_new = kv32[tp:, :cw].reshape(dec_batch, dec_seq, cw)
            v_new = kv32[tp:, cw:].reshape(dec_batch, dec_seq, cw)
            zpad = jnp.zeros((dec_batch, BAND_TILE - dec_seq, cw), BF16)
            q_s = jnp.concatenate([qkv[tp:, :cw].reshape(dec_batch, dec_seq, cw), zpad], axis=1).reshape(-1, cw)
            kw_s = jnp.concatenate([cache_band_k[j].reshape(dec_batch, band_buf, cw).astype(BF16),
                                    k_new.astype(BF16), zpad], axis=1).reshape(-1, cw)
            vw_s = jnp.concatenate([cache_band_v[j].reshape(dec_batch, band_buf, cw).astype(BF16),
                                    v_new.astype(BF16), zpad], axis=1).reshape(-1, cw)
            o_s = band_attn(q_s, kw_s, vw_s, (0, 0, 0), base_rows, dec_batch, 1, False, 0, BAND_WINDOW)
            o_s = o_s.reshape(dec_batch, BAND_TILE, cw)[:, :dec_seq].reshape(ts, cw)
            o_all = jnp.concatenate([o_p, o_s], axis=0)
            x = mix_ffn(x, (o_all,), (w_out_c[j].astype(BF16),), g_ffn, *ffn_w, g_fin, tm_big)

            kv_p = kv32[:tp].reshape(batch, seq, 2, C_HEADS, C_HEAD_DIM)[:, seq - band_buf:]
            outs["p_bk"].append(kv_p[:, :, 0])
            outs["p_bv"].append(kv_p[:, :, 1])
            outs["s_bk"].append(jnp.concatenate(
                [cache_band_k[j][:, dec_seq:], k_new.reshape(dec_batch, dec_seq, C_HEADS, C_HEAD_DIM)], axis=1))
            outs["s_bv"].append(jnp.concatenate(
                [cache_band_v[j][:, dec_seq:], v_new.reshape(dec_batch, dec_seq, C_HEADS, C_HEAD_DIM)], axis=1))

    y_prompt = x[:tp].reshape(batch, seq, d)
    y_sample = x[tp:].reshape(dec_batch, dec_seq, d)
    st = {k: jnp.stack(v) for k, v in outs.items()}
    return (y_prompt, y_sample, st["p_ckv"], st["p_kr"], st["p_C"], st["p_n"], st["p_m"], st["p_bk"], st["p_bv"],
            st["s_ckv"], st["s_kr"], st["s_C"], st["s_n"], st["s_m"], st["s_bk"], st["s_bv"])
```

```python
import functools
import math

import jax
import jax.numpy as jnp
from jax import lax
from jax.experimental import pallas as pl
from jax.experimental.pallas import tpu as pltpu

F32 = jnp.float32
BF16 = jnp.bfloat16

D_MODEL = 1024
CHUNK = 64
A_HEADS = 4
A_HEAD_DIM = 128
A_WIDTH = A_HEADS * A_HEAD_DIM
B_HEADS = 8
B_NOPE = 64
B_ROPE = 32
B_VDIM = 64
B_QK = B_NOPE + B_ROPE
B_WIDTH = B_HEADS * B_VDIM
Q_RANK = 384
KV_RANK = 256
ROPE_THETA = 10000.0
MLA_SCALE = B_QK ** -0.5
C_HEADS = 16
C_HEAD_DIM = 64
C_WIDTH = C_HEADS * C_HEAD_DIM
LEFT_CHUNKS = 8
MAX_REL = 128
C_SCALE = C_HEAD_DIM ** -0.5
EPS = 1e-6
NEG = -1e30
LOG2E = math.log2(math.e)

LANES = 128
MLA_HEAD_PAD = 128
MLA_KV_BLOCK = 512
MLA_Q_TILE = 512
MLA_V_ROWS = B_VDIM + 16
MLA_VT_ROWS = B_HEADS * MLA_V_ROWS
BAND_TILE = 256
BAND_LEFT = LEFT_CHUNKS * CHUNK
BAND_WINDOW = BAND_LEFT + BAND_TILE
BAND_ROLL_WIDTH = BAND_WINDOW + BAND_TILE
VMEM_LIMIT = 56 * 1024 * 1024

SM_CQ = 0
SM_CKV = Q_RANK
SM_KR = Q_RANK + KV_RANK
SM_WIDTH = SM_KR + LANES
GATE_I_LANE = B_ROPE
GATE_F_LANE = B_ROPE + A_HEADS


def _const_spec(shape):
    zeros = (0,) * len(shape)
    return pl.BlockSpec(shape, lambda *_: zeros, pipeline_mode=pl.Buffered(1))


def _params(semantics):
    return pltpu.CompilerParams(dimension_semantics=semantics, vmem_limit_bytes=VMEM_LIMIT)


def _rms(x, g):
    return x * lax.rsqrt(jnp.mean(x * x, axis=-1, keepdims=True) + EPS) * g


def _dot(a, b):
    return jnp.dot(a, b, preferred_element_type=F32)


def _dot_nt(a, b):
    return lax.dot_general(a, b, (((1,), (1,)), ((), ())), preferred_element_type=F32)


def _dot_tn(a, b):
    return lax.dot_general(a, b, (((0,), (0,)), ((), ())), preferred_element_type=F32)


def _norm_proj_body(x_ref, g_ref, w_ref, *out_refs, plan):
    h = _rms(x_ref[...], g_ref[...]).astype(BF16)
    for w0, width, dests in plan:
        z = _dot(h, w_ref[:, w0:w0 + width])
        for out_idx, o0, scale in dests:
            o_ref = out_refs[out_idx]
            zz = z if scale == 1.0 else z * scale
            o_ref[:, o0:o0 + width] = zz.astype(o_ref.dtype)


def norm_proj(x, g, w, plan, out_widths, out_dtypes, tm):
    t, d = x.shape
    n = w.shape[1]
    return pl.pallas_call(
        functools.partial(_norm_proj_body, plan=plan),
        grid=(t // tm,),
        in_specs=[pl.BlockSpec((tm, d), lambda i: (i, 0)), _const_spec((1, d)), _const_spec((d, n))],
        out_specs=[pl.BlockSpec((tm, ow), lambda i: (i, 0)) for ow in out_widths],
        out_shape=[jax.ShapeDtypeStruct((t, ow), dt) for ow, dt in zip(out_widths, out_dtypes)],
        compiler_params=_params(("parallel",)),
        name="norm_proj",
    )(x, g, w)


def _mix_ffn_body(*refs, n_mix, ff_chunks, final):
    x_ref = refs[0]
    a_refs = refs[1:1 + n_mix]
    wo_refs = refs[1 + n_mix:1 + 2 * n_mix]
    g_ref, wg_ref, wu_ref, wd_ref = refs[1 + 2 * n_mix:5 + 2 * n_mix]
    pos = 5 + 2 * n_mix
    gf_ref = refs[pos] if final else None
    o_ref = refs[pos + (1 if final else 0)]
    act_ref = refs[-1]
    x = x_ref[...]
    for a_ref, wo_ref in zip(a_refs, wo_refs):
        x = x + _dot(a_ref[...], wo_ref[...])
    h = _rms(x, g_ref[...]).astype(BF16)
    for c0, cw in ff_chunks:
        gate = _dot(h, wg_ref[:, c0:c0 + cw])
        up = _dot(h, wu_ref[:, c0:c0 + cw])
        act_ref[:, c0:c0 + cw] = (gate * jax.nn.sigmoid(gate) * up).astype(BF16)
    y = x + _dot(act_ref[...], wd_ref[...])
    if final:
        y = _rms(y, gf_ref[...])
    o_ref[...] = y


def mix_ffn(x, mix_in, mix_w, g, wg, wu, wd, g_final, tm):
    t, d = x.shape
    f = wg.shape[1]
    chunk = 512
    ff_chunks = [(c0, min(chunk, f - c0)) for c0 in range(0, f, chunk)]
    final = g_final is not None
    in_specs = [pl.BlockSpec((tm, d), lambda i: (i, 0))]
    in_specs += [pl.BlockSpec((tm, a.shape[1]), lambda i: (i, 0)) for a in mix_in]
    in_specs += [_const_spec(w.shape) for w in mix_w]
    in_specs += [_const_spec((1, d)), _const_spec((d, f)), _const_spec((d, f)), _const_spec((f, d))]
    args = [x, *mix_in, *mix_w, g, wg, wu, wd]
    if final:
        in_specs.append(_const_spec((1, d)))
        args.append(g_final)
    return pl.pallas_call(
        functools.partial(_mix_ffn_body, n_mix=len(mix_in), ff_chunks=ff_chunks, final=final),
        grid=(t // tm,),
        in_specs=in_specs,
        out_specs=pl.BlockSpec((tm, d), lambda i: (i, 0)),
        out_shape=jax.ShapeDtypeStruct((t, d), F32),
        scratch_shapes=[pltpu.VMEM((tm, f), BF16)],
        compiler_params=_params(("parallel",)),
        name="mix_ffn",
    )(*args)


def _mla_prep_q_body(sm_ref, qn_ref, kvn_ref, wq_ref, rc_ref, rs1_ref, rs2_ref, cos_t_ref, sin_t_ref,
                     ckv_ref, kr_ref, qt_ref):
    sm = sm_ref[...]
    cq = _rms(sm[:, SM_CQ:SM_CQ + Q_RANK], qn_ref[...]).astype(BF16)
    ckv_ref[...] = _rms(sm[:, SM_CKV:SM_CKV + KV_RANK], kvn_ref[...])
    grp = sm[:, SM_KR:SM_KR + LANES]
    half = B_ROPE // 2
    rot = (grp * rc_ref[...] + pltpu.roll(grp, half, 1) * rs1_ref[...]
           + pltpu.roll(grp, LANES - half, 1) * rs2_ref[...])
    kr_ref[...] = rot[:, :B_ROPE]
    qt = _dot_nt(wq_ref[...], cq) * (MLA_SCALE * LOG2E)
    cos_t = cos_t_ref[...]
    sin_t = sin_t_ref[...]
    for h in range(B_HEADS):
        r0 = h * MLA_HEAD_PAD
        x1 = qt[r0 + B_NOPE:r0 + B_NOPE + half]
        x2 = qt[r0 + B_NOPE + half:r0 + B_QK]
        qt_ref[0, r0:r0 + B_NOPE, :] = qt[r0:r0 + B_NOPE].astype(BF16)
        rot_q = jnp.concatenate([x1 * cos_t - x2 * sin_t, x1 * sin_t + x2 * cos_t], axis=0)
        qt_ref[0, r0 + B_NOPE:r0 + B_QK, :] = rot_q.astype(BF16)
        qt_ref[0, r0 + B_QK:r0 + MLA_HEAD_PAD, :] = qt[r0 + B_QK:r0 + MLA_HEAD_PAD].astype(BF16)


def mla_prep_q(small, qn, kvn, wq_t, rope_tabs, tab_index, tm):
    t = small.shape[0]
    rc, rs1, rs2, cos_t, sin_t = rope_tabs
    half = B_ROPE // 2
    row_tab = pl.BlockSpec((tm, LANES), lambda i: (tab_index(i), 0))
    col_tab = pl.BlockSpec((half, tm), lambda i: (0, tab_index(i)))
    return pl.pallas_call(
        _mla_prep_q_body,
        grid=(t // tm,),
        in_specs=[pl.BlockSpec((tm, SM_WIDTH), lambda i: (i, 0)), _const_spec(qn.shape), _const_spec(kvn.shape),
                  _const_spec(wq_t.shape), row_tab, row_tab, row_tab, col_tab, col_tab],
        out_specs=[pl.BlockSpec((tm, KV_RANK), lambda i: (i, 0)), pl.BlockSpec((tm, B_ROPE), lambda i: (i, 0)),
                   pl.BlockSpec((1, B_HEADS * MLA_HEAD_PAD, tm), lambda i: (i, 0, 0))],
        out_shape=[jax.ShapeDtypeStruct((t, KV_RANK), F32), jax.ShapeDtypeStruct((t, B_ROPE), F32),
                   jax.ShapeDtypeStruct((t // tm, B_HEADS * MLA_HEAD_PAD, tm), BF16)],
        compiler_params=_params(("parallel",)),
        name="mla_prep_q",
    )(small, qn, kvn, wq_t, rc, rs1, rs2, cos_t, sin_t)


def _mla_prep_kv_body(ckv_ref, kr_ref, wk_ref, place_ref, wv_ref, kp_ref, vt_ref):
    c = ckv_ref[...].astype(BF16)
    kr = kr_ref[...].astype(BF16)
    kp_ref[...] = (_dot(c, wk_ref[...]) + _dot(kr, place_ref[...])).astype(BF16)
    vt = _dot_nt(wv_ref[...], c).astype(BF16)
    ones = jnp.ones((MLA_V_ROWS - B_VDIM, vt.shape[1]), BF16)
    for h in range(B_HEADS):
        vt_ref[0, h * MLA_V_ROWS:h * MLA_V_ROWS + B_VDIM, :] = vt[h * B_VDIM:(h + 1) * B_VDIM]
        vt_ref[0, h * MLA_V_ROWS + B_VDIM:(h + 1) * MLA_V_ROWS, :] = ones


def mla_prep_kv(ckv, krope, wk_pad, place, wv_t, n_tiles, tm):
    kw = B_HEADS * MLA_HEAD_PAD
    return pl.pallas_call(
        _mla_prep_kv_body,
        grid=(n_tiles,),
        in_specs=[pl.BlockSpec((tm, KV_RANK), lambda i: (i, 0)), pl.BlockSpec((tm, B_ROPE), lambda i: (i, 0)),
                  _const_spec(wk_pad.shape), _const_spec(place.shape), _const_spec(wv_t.shape)],
        out_specs=[pl.BlockSpec((tm, kw), lambda i: (i, 0)), pl.BlockSpec((1, MLA_VT_ROWS, tm), lambda i: (i, 0, 0))],
        out_shape=[jax.ShapeDtypeStruct((n_tiles * tm, kw), BF16),
                   jax.ShapeDtypeStruct((n_tiles, MLA_VT_ROWS, tm), BF16)],
        compiler_params=_params(("parallel",)),
        name="mla_prep_kv",
    )(ckv, krope, wk_pad, place, wv_t)


def _mla_attn_body(qt_ref, kp_ref, vt_ref, o_ref, m_ref, l_ref, acc_ref, *, tq, tk, q0, n_diag):
    i = pl.program_id(1)
    start = q0 + i * tq
    n_full = start // tk
    m_ref[...] = jnp.full(m_ref.shape, NEG, F32)
    l_ref[...] = jnp.zeros(l_ref.shape, F32)
    acc_ref[...] = jnp.zeros(acc_ref.shape, F32)
    key_chunk = lax.broadcasted_iota(jnp.int32, (tk, tq), 0) // CHUNK
    qry_chunk = lax.broadcasted_iota(jnp.int32, (tk, tq), 1) // CHUNK

    def step(j, masked):
        row0 = pl.multiple_of(j * tk, tk)
        if masked:
            visible = key_chunk + (j * tk - start) // CHUNK <= qry_chunk

        def scores(h):
            k_h = kp_ref[pl.ds(row0, tk), h * MLA_HEAD_PAD:(h + 1) * MLA_HEAD_PAD]
            q_h = qt_ref[0, h * MLA_HEAD_PAD:(h + 1) * MLA_HEAD_PAD, :]
            return _dot(k_h, q_h)

        s_next = scores(0)
        for h in range(B_HEADS):
            s = s_next
            if h + 1 < B_HEADS:
                s_next = scores(h + 1)
            if masked:
                s = jnp.where(visible, s, NEG)
            m_prev = m_ref[h:h + 1, :]
            m_new = jnp.maximum(m_prev, jnp.max(s, axis=0, keepdims=True))
            alpha = jnp.exp2(m_prev - m_new)
            p = jnp.exp2(s - m_new).astype(BF16)
            m_ref[h:h + 1, :] = m_new
            pv = _dot(vt_ref[j, h * MLA_V_ROWS:(h + 1) * MLA_V_ROWS, :], p)
            l_ref[h:h + 1, :] = alpha * l_ref[h:h + 1, :] + pv[B_VDIM:B_VDIM + 1]
            rows = slice(h * B_VDIM, (h + 1) * B_VDIM)
            acc_ref[rows, :] = alpha * acc_ref[rows, :] + pv[:B_VDIM]

    def full_step(j, carry):
        step(j, False)
        return carry

    lax.fori_loop(0, n_full, full_step, 0)
    for d in range(n_diag):
        step(n_full + d, True)
    for h in range(B_HEADS):
        rows = slice(h * B_VDIM, (h + 1) * B_VDIM)
        acc_ref[rows, :] = acc_ref[rows, :] / l_ref[h:h + 1, :]
    o_ref[...] = jnp.transpose(acc_ref[...]).astype(o_ref.dtype)


def mla_attn(qt, kp, vt, groups, nq, tq, tk, n_blocks, q0, q_frames):
    kw = B_HEADS * MLA_HEAD_PAD
    assert q0 % tk == 0 and (tq % tk == 0 or (nq == 1 and q_frames <= tk))
    n_diag = pl.cdiv(q_frames, tk)
    return pl.pallas_call(
        functools.partial(_mla_attn_body, tq=tq, tk=tk, q0=q0, n_diag=n_diag),
        grid=(groups, nq),
        in_specs=[pl.BlockSpec((1, kw, tq), lambda g, i: (g * nq + i, 0, 0)),
                  pl.BlockSpec((n_blocks * tk, kw), lambda g, i: (g, 0), pipeline_mode=pl.Buffered(1)),
                  pl.BlockSpec((n_blocks, MLA_VT_ROWS, tk), lambda g, i: (g, 0, 0), pipeline_mode=pl.Buffered(1))],
        out_specs=pl.BlockSpec((tq, B_WIDTH), lambda g, i: (g * nq + i, 0)),
        out_shape=jax.ShapeDtypeStruct((groups * nq * tq, B_WIDTH), BF16),
        scratch_shapes=[pltpu.VMEM((B_HEADS, tq), F32), pltpu.VMEM((B_HEADS, tq), F32),
                        pltpu.VMEM((B_WIDTH, tq), F32)],
        compiler_params=_params(("parallel", "arbitrary")),
        name="mla_attn",
    )(qt, kp, vt)


def _scan_rows(x, op, fill, length):
    row = lax.broadcasted_iota(jnp.int32, x.shape, 0)
    shift = 1
    while shift < length:
        moved = pltpu.roll(x, shift, 0)
        x = op(x, jnp.where(row >= shift, moved, fill))
        shift *= 2
    return x


def _mlstm_body(q_ref, k_ref, v_ref, o_ref, gt_ref, bias_ref, hn_ref, c0_ref, m0_ref,
                ha_ref, c_out_ref, m_out_ref, c_scr, m_scr, *, blk):
    dh = A_HEAD_DIM
    step_idx = pl.program_id(1)

    @pl.when(step_idx == 0)
    def _():
        c_scr[...] = c0_ref[0]
        m_scr[...] = m0_ref[0]

    gates = gt_ref[...] + bias_ref[...]
    log_f = jax.nn.log_sigmoid(gates)
    b_all = pltpu.roll(_scan_rows(log_f, jnp.add, 0.0, blk), LANES - A_HEADS, 1)
    a_all = gates - b_all
    amax_all = _scan_rows(a_all, jnp.maximum, NEG, blk)
    pad = max(blk, LANES) - blk
    a_sq = a_all if pad == 0 else jnp.concatenate([a_all, jnp.zeros((pad, LANES), F32)], axis=0)
    a_rows = jnp.transpose(a_sq)
    causal = (lax.broadcasted_iota(jnp.int32, (blk, blk), 0) >= lax.broadcasted_iota(jnp.int32, (blk, blk), 1))
    ones_col = (lax.broadcasted_iota(jnp.int32, (blk, dh), 1) == 0).astype(BF16)
    for h in range(A_HEADS):
        lane = GATE_I_LANE + h
        cols = slice(h * dh, (h + 1) * dh)
        q = q_ref[:, cols]
        k = k_ref[:, cols]
        v_ext = jnp.concatenate([v_ref[:, cols], ones_col], axis=1)
        a_col = a_all[:, lane:lane + 1]
        b_col = b_all[:, lane:lane + 1]
        a_row = a_rows[lane:lane + 1, :blk]
        m_prev = m_scr[h:h + 1, 0:1]
        run_max = jnp.maximum(amax_all[:, lane:lane + 1], m_prev)
        decay_mat = jnp.exp(jnp.where(causal, a_row - run_max, NEG))
        p = (_dot_nt(q, k) * decay_mat).astype(BF16)
        state = c_scr[h]
        w_inter = jnp.exp(m_prev - run_max)
        numden = w_inter * _dot(q, state.astype(BF16)) + _dot(p, v_ext)
        den = numden[:, dh:dh + 1]
        hh = numden[:, :dh] / jnp.maximum(jnp.abs(den), jnp.exp(-(b_col + run_max)))
        hh = _rms(hh, hn_ref[:, cols])
        ha_ref[:, cols] = (hh * jax.nn.sigmoid(o_ref[:, cols])).astype(ha_ref.dtype)
        max_last = run_max[blk - 1:blk, :]
        w_k = jnp.exp(a_col - max_last)
        k_w = (k.astype(F32) * w_k).astype(BF16)
        c_scr[h] = jnp.exp(m_prev - max_last) * state + _dot_tn(k_w, v_ext)
        m_scr[h:h + 1, :] = jnp.broadcast_to(b_col[blk - 1:blk, :] + max_last, (1, LANES))

    @pl.when(step_idx == pl.num_programs(1) - 1)
    def _():
        c_out_ref[0] = c_scr[...]
        m_out_ref[0] = m_scr[...]


def mlstm(qkv, o32, small, bias, hnorm, c0, m0, groups, steps, blk, row0):
    base = row0 // blk
    gate_block = SM_KR // LANES

    def rows(col):
        return lambda g, s: (base + g * steps + s, col)

    return pl.pallas_call(
        functools.partial(_mlstm_body, blk=blk),
        grid=(groups, steps),
        in_specs=[pl.BlockSpec((blk, A_WIDTH), rows(0)), pl.BlockSpec((blk, A_WIDTH), rows(1)),
                  pl.BlockSpec((blk, A_WIDTH), rows(2)), pl.BlockSpec((blk, A_WIDTH), rows(0)),
                  pl.BlockSpec((blk, LANES), rows(gate_block)), _const_spec((1, LANES)), _const_spec((1, A_WIDTH)),
                  pl.BlockSpec((1, A_HEADS, A_HEAD_DIM, 2 * A_HEAD_DIM), lambda g, s: (g, 0, 0, 0)),
                  pl.BlockSpec((1, 8, LANES), lambda g, s: (g, 0, 0))],
        out_specs=[pl.BlockSpec((blk, A_WIDTH), lambda g, s: (g * steps + s, 0)),
                   pl.BlockSpec((1, A_HEADS, A_HEAD_DIM, 2 * A_HEAD_DIM), lambda g, s: (g, 0, 0, 0)),
                   pl.BlockSpec((1, 8, LANES), lambda g, s: (g, 0, 0))],
        out_shape=[jax.ShapeDtypeStruct((groups * steps * blk, A_WIDTH), BF16),
                   jax.ShapeDtypeStruct((groups, A_HEADS, A_HEAD_DIM, 2 * A_HEAD_DIM), F32),
                   jax.ShapeDtypeStruct((groups, 8, LANES), F32)],
        scratch_shapes=[pltpu.VMEM((A_HEADS, A_HEAD_DIM, 2 * A_HEAD_DIM), F32), pltpu.VMEM((8, LANES), F32)],
        compiler_params=_params(("parallel", "arbitrary")),
        name="mlstm",
    )(qkv, qkv, qkv, o32, small, bias, hnorm, c0, m0)


def _band_proj_body(x_ref, g_ref, wk_ref, wq_t_ref, wv_t_ref, wkv_ref, k_ref, qt_ref, vt_ref, tail_ref,
                    *, tiles_per_seq, p_tiles):
    i = pl.program_id(0)
    h = _rms(x_ref[...], g_ref[...]).astype(BF16)
    step = 512
    for c0 in range(0, C_WIDTH, step):
        k_ref[:, c0:c0 + step] = _dot(h, wk_ref[:, c0:c0 + step]).astype(BF16)
        qt = _dot_nt(wq_t_ref[c0:c0 + step, :], h) * (C_SCALE * LOG2E)
        qt_ref[0, c0:c0 + step, :] = qt.astype(BF16)
        vt_ref[0, c0:c0 + step, :] = _dot_nt(wv_t_ref[c0:c0 + step, :], h).astype(BF16)

    is_tail = ((i + 1) % tiles_per_seq == 0) | (i >= p_tiles)

    @pl.when(is_tail)
    def _():
        for c0 in range(0, 2 * C_WIDTH, step):
            tail_ref[:, c0:c0 + step] = _dot(h, wkv_ref[:, c0:c0 + step])

    @pl.when(jnp.logical_not(is_tail) & (i % tiles_per_seq == 0))
    def _():
        tail_ref[...] = jnp.zeros(tail_ref.shape, F32)


def band_proj(x, g, wk, wq_t, wv_t, wkv, tm, tiles_per_seq, p_tiles, n_seq):
    t, d = x.shape
    n_tiles = t // tm
    n_tail = n_seq + n_tiles - p_tiles
    cw = C_WIDTH

    def tail_index(i):
        return (jnp.where(i < p_tiles, i // tiles_per_seq, n_seq + i - p_tiles), 0)

    return pl.pallas_call(
        functools.partial(_band_proj_body, tiles_per_seq=tiles_per_seq, p_tiles=p_tiles),
        grid=(n_tiles,),
        in_specs=[pl.BlockSpec((tm, d), lambda i: (i, 0)), _const_spec((1, d)), _const_spec((d, cw)),
                  _const_spec((cw, d)), _const_spec((cw, d)), _const_spec((d, 2 * cw))],
        out_specs=[pl.BlockSpec((tm, cw), lambda i: (i, 0)), pl.BlockSpec((1, cw, tm), lambda i: (i, 0, 0)),
                   pl.BlockSpec((1, cw, tm), lambda i: (i, 0, 0)), pl.BlockSpec((tm, 2 * cw), tail_index)],
        out_shape=[jax.ShapeDtypeStruct((t, cw), BF16), jax.ShapeDtypeStruct((n_tiles, cw, tm), BF16),
                   jax.ShapeDtypeStruct((n_tiles, cw, tm), BF16), jax.ShapeDtypeStruct((n_tail * tm, 2 * cw), F32)],
        compiler_params=_params(("arbitrary",)),
        name="band_proj",
    )(x, g, wk, wq_t, wv_t, wkv)


def _band_table_body(base_ref, tab_ref):
    tq = BAND_TILE
    shape = (tq, BAND_ROLL_WIDTH)
    row = lax.broadcasted_iota(jnp.int32, shape, 0)
    tab = jnp.broadcast_to(base_ref[0], shape)
    shift = 1
    while shift < tq:
        tab = jnp.where((row & shift) != 0, pltpu.roll(tab, shift, 1), tab)
        shift *= 2
    win = lax.broadcasted_iota(jnp.int32, (tq, BAND_WINDOW), 1) // CHUNK
    qch = lax.broadcasted_iota(jnp.int32, (tq, BAND_WINDOW), 0) // CHUNK
    valid = (win >= qch) & (win <= qch + LEFT_CHUNKS)
    tab_ref[0] = jnp.transpose(jnp.where(valid, tab[:, :BAND_WINDOW] * LOG2E, NEG))


def band_table(base_rows):
    return pl.pallas_call(
        _band_table_body,
        grid=(C_HEADS,),
        in_specs=[pl.BlockSpec((1, 1, BAND_ROLL_WIDTH), lambda h: (h, 0, 0))],
        out_specs=pl.BlockSpec((1, BAND_WINDOW, BAND_TILE), lambda h: (h, 0, 0)),
        out_shape=jax.ShapeDtypeStruct((C_HEADS, BAND_WINDOW, BAND_TILE), F32),
        compiler_params=_params(("parallel",)),
        name="band_table",
    )(base_rows)


def _band_body(qt_ref, *refs, clamp_start):
    nb = BAND_WINDOW // BAND_TILE
    k_refs = refs[:nb]
    v_refs = refs[nb:2 * nb]
    tab_ref, o_ref, acc_ref = refs[2 * nb:]
    tq = BAND_TILE
    i = pl.program_id(1)
    row_half = lax.broadcasted_iota(jnp.int32, (LANES, tq), 0) // C_HEAD_DIM
    ones = jnp.ones((16, tq), BF16)

    def run(first_tiles):
        def scores(h):
            pair, half = divmod(h, 2)
            q_pair = qt_ref[0, pair * LANES:(pair + 1) * LANES, :]
            q_h = jnp.where(row_half == half, q_pair, jnp.zeros_like(q_pair))
            parts = []
            for j in range(nb):
                s_j = _dot(k_refs[j][:, pair * LANES:(pair + 1) * LANES], q_h)
                if first_tiles and j < nb - 1:
                    s_j = jnp.where(i + j < nb - 1, NEG, s_j)
                parts.append(s_j)
            return jnp.concatenate(parts, axis=0)

        s_next = scores(0)
        for h in range(C_HEADS):
            s = s_next + tab_ref[h]
            if h + 1 < C_HEADS:
                s_next = scores(h + 1)
            m = jnp.max(s, axis=0, keepdims=True)
            p = jnp.exp2(s - m).astype(BF16)
            rows = slice(h * C_HEAD_DIM, (h + 1) * C_HEAD_DIM)
            pv = None
            for j in range(nb):
                v_ext = jnp.concatenate([v_refs[j][0, rows, :], ones], axis=0)
                part = _dot(v_ext, p[j * BAND_TILE:(j + 1) * BAND_TILE])
                pv = part if pv is None else pv + part
            acc_ref[rows, :] = pv[:C_HEAD_DIM] / pv[C_HEAD_DIM:C_HEAD_DIM + 1]

    if clamp_start:
        @pl.when(i < nb - 1)
        def _():
            run(True)

        @pl.when(i >= nb - 1)
        def _():
            run(False)
    else:
        run(False)
    o_ref[...] = jnp.transpose(acc_ref[...]).astype(o_ref.dtype)


def band_attn(qt_arr, qt_map, k_arr, k_maps, vt_arr, vt_maps, tab, groups, tiles, clamp_start):
    tq = BAND_TILE
    in_specs = [pl.BlockSpec((1, C_WIDTH, tq), qt_map)]
    in_specs += [pl.BlockSpec((tq, C_WIDTH), m) for m in k_maps]
    in_specs += [pl.BlockSpec((1, C_WIDTH, tq), m) for m in vt_maps]
    in_specs.append(_const_spec(tab.shape))
    return pl.pallas_call(
        functools.partial(_band_body, clamp_start=clamp_start),
        grid=(groups, tiles),
        in_specs=in_specs,
        out_specs=pl.BlockSpec((tq, C_WIDTH), lambda g, i: (g * tiles + i, 0)),
        out_shape=jax.ShapeDtypeStruct((groups * tiles * tq, C_WIDTH), BF16),
        scratch_shapes=[pltpu.VMEM((C_WIDTH, tq), F32)],
        compiler_params=_params(("parallel", "arbitrary")),
        name="band_attn",
    )(qt_arr, *([k_arr] * len(k_maps)), *([vt_arr] * len(vt_maps)), tab)


def _rope_tables(positions):
    half = B_ROPE // 2
    inv = ROPE_THETA ** (-jnp.arange(half, dtype=F32) / half)
    ang = positions.astype(F32)[:, None] * inv[None, :]
    cos, sin = jnp.cos(ang), jnp.sin(ang)
    n = positions.shape[0]
    zeros = jnp.zeros((n, LANES - B_ROPE), F32)
    zh = jnp.zeros((n, half), F32)
    rc = jnp.concatenate([cos, cos, zeros], axis=1)
    rs1 = jnp.concatenate([zh, sin, zeros], axis=1)
    rs2 = jnp.concatenate([-sin, zh, zeros], axis=1)
    return rc, rs1, rs2, cos.T, sin.T


def _band_base_rows(rel_bias):
    x = jnp.arange(BAND_ROLL_WIDTH)
    rel = jnp.where(x < BAND_WINDOW, BAND_LEFT - x, BAND_LEFT + 1)
    idx = jnp.clip(rel, -MAX_REL, MAX_REL) + MAX_REL
    return rel_bias[:, None, idx]


def kernel(x_prompt, x_sample, cache_mla_ckv, cache_mla_krope, state_mlstm_C, state_mlstm_n, state_mlstm_m,
           cache_band_k, cache_band_v, norm_mix, norm_ffn, norm_final, w_in_ab, b_gates, mlstm_hnorm,
           mla_q_norm, mla_kv_norm, mla_w_uq, mla_w_uk, mla_w_uv, w_out_ab, w_qkv_c, w_out_c, rel_bias_c,
           w_gate, w_up, w_down):
    batch, seq, d = x_prompt.shape
    dec_batch, dec_seq, _ = x_sample.shape
    past = cache_mla_ckv.shape[2]
    band_buf = cache_band_k.shape[2]
    depth = norm_mix.shape[0]
    tp = batch * seq
    ts = dec_batch * dec_seq
    t = tp + ts
    tm_big = 1024
    tm = MLA_Q_TILE
    tk = MLA_KV_BLOCK
    assert d == D_MODEL and dec_seq == CHUNK and band_buf == BAND_LEFT and past % tk == 0
    assert tp % tm_big == 0 and ts % tm_big == 0 and seq % tm == 0 and tm % dec_seq == 0 and tm % tk == 0

    x = jnp.concatenate([x_prompt.reshape(tp, d), x_sample.reshape(ts, d)], axis=0)

    pos_tab = jnp.concatenate([jnp.arange(seq, dtype=jnp.int32),
                               past + (jnp.arange(tm, dtype=jnp.int32) % dec_seq)])
    rope_tabs = _rope_tables(pos_tab)
    p_tiles = tp // tm
    seq_tiles = seq // tm

    def tab_index(i):
        return jnp.where(i < p_tiles, i % seq_tiles, seq_tiles)

    outs = {k: [] for k in ("p_ckv", "p_kr", "p_C", "p_n", "p_m", "p_bk", "p_bv",
                            "s_ckv", "s_kr", "s_C", "s_n", "s_m", "s_bk", "s_bv")}
    a4 = 4 * A_WIDTH
    for layer in range(depth):
        j = layer // 2
        g_mix = norm_mix[layer][None, :]
        g_ffn = norm_ffn[layer][None, :]
        g_fin = norm_final[None, :] if layer == depth - 1 else None
        ffn_w = (w_gate[layer].astype(BF16), w_up[layer].astype(BF16), w_down[layer].astype(BF16))
        if layer % 2 == 0:
            w = w_in_ab[j]
            gate_cols = jnp.concatenate([w[:, a4 + 2 * A_HEADS + Q_RANK + KV_RANK:], w[:, a4:a4 + 2 * A_HEADS],
                                         jnp.zeros((d, LANES - B_ROPE - 2 * A_HEADS), F32)], axis=1)
            w_all = jnp.concatenate([w[:, :a4], w[:, a4 + 2 * A_HEADS:a4 + 2 * A_HEADS + Q_RANK + KV_RANK],
                                     gate_cols], axis=1).astype(BF16)
            aw = A_WIDTH
            plan = [(0, aw, [(0, 0, 1.0)]), (aw, aw, [(0, aw, A_HEAD_DIM ** -0.5)]), (2 * aw, aw, [(0, 2 * aw, 1.0)]),
                    (3 * aw, aw, [(1, 0, 1.0)]), (4 * aw, Q_RANK, [(2, SM_CQ, 1.0)]),
                    (4 * aw + Q_RANK, KV_RANK + LANES, [(2, SM_CKV, 1.0)])]
            qkv, o32, small = norm_proj(x, g_mix, w_all, plan, (3 * aw, aw, SM_WIDTH), (BF16, F32, F32), tm_big)

            wq = mla_w_uq[j].reshape(Q_RANK, B_HEADS, B_QK)
            wq_t = jnp.pad(wq, ((0, 0), (0, 0), (0, MLA_HEAD_PAD - B_QK))).reshape(Q_RANK, -1).T.astype(BF16)
            ckv, krope, qt = mla_prep_q(small, mla_q_norm[j][None, :], mla_kv_norm[j][None, :], wq_t,
                                        rope_tabs, tab_index, tm)
            wk_pad = jnp.pad(mla_w_uk[j], ((0, 0), (0, 0), (0, MLA_HEAD_PAD - B_NOPE))).reshape(KV_RANK, -1).astype(BF16)
            place = jnp.pad(jnp.eye(B_ROPE, dtype=F32), ((0, 0), (B_NOPE, MLA_HEAD_PAD - B_QK)))
            place = jnp.tile(place, (1, B_HEADS)).astype(BF16)
            wv_t = mla_w_uv[j].reshape(KV_RANK, B_WIDTH).T.astype(BF16)
            kp_p, vt_p = mla_prep_kv(ckv, krope, wk_pad, place, wv_t, tp // tk, tk)
            ckv_s = ckv[tp:].reshape(dec_batch, dec_seq, KV_RANK)
            kr_s = krope[tp:].reshape(dec_batch, dec_seq, B_ROPE)
            fill = tk - dec_seq
            lat_s = jnp.concatenate([cache_mla_ckv[j], ckv_s, jnp.zeros((dec_batch, fill, KV_RANK), F32)], axis=1)
            lkr_s = jnp.concatenate([cache_mla_krope[j], kr_s, jnp.zeros((dec_batch, fill, B_ROPE), F32)], axis=1)
            s_blocks = (past + tk) // tk
            kp_s, vt_s = mla_prep_kv(lat_s.reshape(-1, KV_RANK), lkr_s.reshape(-1, B_ROPE), wk_pad, place, wv_t,
                                     dec_batch * s_blocks, tk)

            hb_p = mla_attn(qt, kp_p, vt_p, batch, seq_tiles, tm, tk, seq // tk, 0, tm)
            qt_s = qt[p_tiles:].reshape(ts // tm, -1, tm // dec_seq, dec_seq)
            qt_s = jnp.moveaxis(qt_s, 2, 1).reshape(dec_batch, -1, dec_seq)
            qt_s = jnp.pad(qt_s, ((0, 0), (0, 0), (0, LANES - dec_seq)))
            hb_s = mla_attn(qt_s, kp_s, vt_s, dec_batch, 1, LANES, tk, s_blocks, past, dec_seq)
            hb_s = hb_s.reshape(dec_batch, LANES, B_WIDTH)[:, :dec_seq].reshape(ts, B_WIDTH)

            bias = jnp.zeros((1, LANES), F32).at[0, GATE_I_LANE:GATE_I_LANE + 2 * A_HEADS].set(b_gates[j])
            hn = mlstm_hnorm[j][None, :]
            c0_p = jnp.zeros((batch, A_HEADS, A_HEAD_DIM, 2 * A_HEAD_DIM), F32)
            m0_p = jnp.zeros((batch, 8, LANES), F32)
            blk_p = 256
            ha_p, c_p, m_p = mlstm(qkv, o32, small, bias, hn, c0_p, m0_p, batch, seq // blk_p, blk_p, 0)
            c0_s = jnp.concatenate([state_mlstm_C[j], state_mlstm_n[j][..., None],
                                    jnp.zeros((dec_batch, A_HEADS, A_HEAD_DIM, A_HEAD_DIM - 1), F32)], axis=-1)
            m0_s = jnp.broadcast_to(jnp.pad(state_mlstm_m[j], ((0, 0), (0, 8 - A_HEADS)))[..., None],
                                    (dec_batch, 8, LANES))
            ha_s, c_s, m_s = mlstm(qkv, o32, small, bias, hn, c0_s, m0_s, dec_batch, 1, dec_seq, tp)

            ha = jnp.concatenate([ha_p, ha_s], axis=0)
            hb = jnp.concatenate([hb_p, hb_s], axis=0)
            wo = w_out_ab[j].astype(BF16)
            x = mix_ffn(x, (ha, hb), (wo[:A_WIDTH], wo[A_WIDTH:]), g_ffn, *ffn_w, g_fin, tm_big)

            outs["p_ckv"].append(ckv[:tp].reshape(batch, seq, KV_RANK))
            outs["p_kr"].append(krope[:tp].reshape(batch, seq, B_ROPE))
            outs["p_C"].append(c_p[..., :A_HEAD_DIM])
            outs["p_n"].append(c_p[..., A_HEAD_DIM])
            outs["p_m"].append(m_p[:, :A_HEADS, 0])
            outs["s_ckv"].append(ckv_s)
            outs["s_kr"].append(kr_s)
            outs["s_C"].append(c_s[..., :A_HEAD_DIM])
            outs["s_n"].append(c_s[..., A_HEAD_DIM])
            outs["s_m"].append(m_s[:, :A_HEADS, 0])
        else:
            cw = C_WIDTH
            wqkv = w_qkv_c[j].astype(BF16)
            tmb = 512
            tps = seq // tmb
            ptb = tp // tmb
            assert band_buf <= tmb and seq % tmb == 0 and ts % tmb == 0 and tmb % BAND_TILE == 0
            k_bf, qt_all, vt_all, kv_tail = band_proj(x, g_mix, wqkv[:, cw:2 * cw], wqkv[:, :cw].T,
                                                       wqkv[:, 2 * cw:].T, wqkv[:, cw:], tmb, tps, ptb, batch)
            tab = band_table(_band_base_rows(rel_bias_c[j]))
            tq = BAND_TILE
            nb = BAND_WINDOW // tq
            sub = tmb // tq
            tiles = seq // tq

            def block(g, i, jj):
                return g * tiles + jnp.maximum(i + jj - (nb - 1), 0)

            o_p = band_attn(
                qt_all, lambda g, i: ((g * tiles + i) // sub, 0, (g * tiles + i) % sub),
                k_bf, [functools.partial(lambda g, i, jj: (block(g, i, jj), 0), jj=jj) for jj in range(nb)],
                vt_all, [functools.partial(lambda g, i, jj: (block(g, i, jj) // sub, 0, block(g, i, jj) % sub), jj=jj)
                         for jj in range(nb)],
                tab, batch, tiles, True)

            tail_s = kv_tail[batch * tmb:]
            k_new = tail_s[:, :cw].reshape(dec_batch, dec_seq, cw)
            v_new = tail_s[:, cw:].reshape(dec_batch, dec_seq, cw)

            def per_stream_t(a):
                a = a.reshape(-1, cw, tmb // dec_seq, dec_seq)
                return jnp.moveaxis(a, 2, 1).reshape(dec_batch, cw, dec_seq)

            qt_s = jnp.pad(per_stream_t(qt_all[ptb:]), ((0, 0), (0, 0), (0, tq - dec_seq)))
            kw_s = jnp.concatenate([cache_band_k[j].reshape(dec_batch, band_buf, cw).astype(BF16),
                                    k_bf[tp:].reshape(dec_batch, dec_seq, cw),
                                    jnp.zeros((dec_batch, tq - dec_seq, cw), BF16)], axis=1).reshape(-1, cw)
            vt_s = jnp.concatenate([jnp.swapaxes(cache_band_v[j].reshape(dec_batch, band_buf, cw), 1, 2).astype(BF16),
                                    per_stream_t(vt_all[ptb:]),
                                    jnp.zeros((dec_batch, cw, tq - dec_seq), BF16)], axis=2)
            o_s = band_attn(
                qt_s, lambda g, i: (g, 0, 0),
                kw_s, [functools.partial(lambda g, i, jj: (g * nb + jj, 0), jj=jj) for jj in range(nb)],
                vt_s, [functools.partial(lambda g, i, jj: (g, 0, jj), jj=jj) for jj in range(nb)],
                tab, dec_batch, 1, False)
            o_s = o_s.reshape(dec_batch, tq, cw)[:, :dec_seq].reshape(ts, cw)
            o_all = jnp.concatenate([o_p, o_s], axis=0)
            x = mix_ffn(x, (o_all,), (w_out_c[j].astype(BF16),), g_ffn, *ffn_w, g_fin, tm_big)

            kv_p = kv_tail[:batch * tmb].reshape(batch, tmb, 2, C_HEADS, C_HEAD_DIM)[:, tmb - band_buf:]
            outs["p_bk"].append(kv_p[:, :, 0])
            outs["p_bv"].append(kv_p[:, :, 1])
            outs["s_bk"].append(jnp.concatenate(
                [cache_band_k[j][:, dec_seq:], k_new.reshape(dec_batch, dec_seq, C_HEADS, C_HEAD_DIM)], axis=1))
            outs["s_bv"].append(jnp.concatenate(
                [cache_band_v[j][:, dec_seq:], v_new.reshape(dec_batch, dec_seq, C_HEADS, C_HEAD_DIM)], axis=1))

    y_prompt = x[:tp].reshape(batch, seq, d)
    y_sample = x[tp:].reshape(dec_batch, dec_seq, d)
    st = {k: jnp.stack(v) for k, v in outs.items()}
    return (y_prompt, y_sample, st["p_ckv"], st["p_kr"], st["p_C"], st["p_n"], st["p_m"], st["p_bk"], st["p_bv"],
            st["s_ckv"], st["s_kr"], st["s_C"], st["s_n"], st["s_m"], st["s_bk"], st["s_bv"])
```

```python
import functools
import math

import jax
import jax.numpy as jnp
from jax import lax
from jax.experimental import pallas as pl
from jax.experimental.pallas import tpu as pltpu

F32 = jnp.float32
BF16 = jnp.bfloat16

D_MODEL = 1024
CHUNK = 64
A_HEADS = 4
A_HEAD_DIM = 128
A_WIDTH = A_HEADS * A_HEAD_DIM
B_HEADS = 8
B_NOPE = 64
B_ROPE = 32
B_VDIM = 64
B_QK = B_NOPE + B_ROPE
B_WIDTH = B_HEADS * B_VDIM
Q_RANK = 384
KV_RANK = 256
ROPE_THETA = 10000.0
MLA_SCALE = B_QK ** -0.5
C_HEADS = 16
C_HEAD_DIM = 64
C_WIDTH = C_HEADS * C_HEAD_DIM
LEFT_CHUNKS = 8
MAX_REL = 128
C_SCALE = C_HEAD_DIM ** -0.5
EPS = 1e-6
NEG = -1e30
LOG2E = math.log2(math.e)

LANES = 128
MLA_HEAD_PAD = 128
MLA_KV_BLOCK = 512
MLA_Q_TILE = 512
MLA_V_ROWS = B_VDIM + 16
MLA_VT_ROWS = B_HEADS * MLA_V_ROWS
BAND_TILE = 256
BAND_LEFT = LEFT_CHUNKS * CHUNK
BAND_WINDOW = BAND_LEFT + BAND_TILE
BAND_ROLL_WIDTH = BAND_WINDOW + BAND_TILE
VMEM_LIMIT = 56 * 1024 * 1024

SM_CQ = 0
SM_CKV = Q_RANK
SM_KR = Q_RANK + KV_RANK
SM_WIDTH = SM_KR + LANES
GATE_I_LANE = B_ROPE
GATE_F_LANE = B_ROPE + A_HEADS


def _const_spec(shape):
    zeros = (0,) * len(shape)
    return pl.BlockSpec(shape, lambda *_: zeros, pipeline_mode=pl.Buffered(1))


def _params(semantics):
    return pltpu.CompilerParams(dimension_semantics=semantics, vmem_limit_bytes=VMEM_LIMIT)


def _rms(x, g):
    return x * lax.rsqrt(jnp.mean(x * x, axis=-1, keepdims=True) + EPS) * g


def _dot(a, b):
    return jnp.dot(a, b, preferred_element_type=F32)


def _dot_nt(a, b):
    return lax.dot_general(a, b, (((1,), (1,)), ((), ())), preferred_element_type=F32)


def _dot_tn(a, b):
    return lax.dot_general(a, b, (((0,), (0,)), ((), ())), preferred_element_type=F32)


def _row_specs(parts, tm, p_tiles):
    width = parts[0].shape[1]
    if len(parts) == 1:
        return [pl.BlockSpec((tm, width), lambda i: (i, 0))]
    assert parts[1].shape[0] == tm
    return [pl.BlockSpec((tm, width), lambda i: (jnp.minimum(i, p_tiles - 1), 0)),
            pl.BlockSpec((tm, width), lambda i: (0, 0), pipeline_mode=pl.Buffered(1))]


def _row_load(refs, p_tiles):
    if len(refs) == 1:
        return refs[0][...]
    return jnp.where(pl.program_id(0) < p_tiles, refs[0][...], refs[1][...])


def _norm_proj_body(*refs, n_x, p_tiles, plan):
    x_refs = refs[:n_x]
    g_ref, w_ref = refs[n_x:n_x + 2]
    out_refs = refs[n_x + 2:]
    h = _rms(_row_load(x_refs, p_tiles), g_ref[...]).astype(BF16)
    for w0, width, dests in plan:
        z = _dot(h, w_ref[:, w0:w0 + width])
        for out_idx, o0, scale in dests:
            o_ref = out_refs[out_idx]
            zz = z if scale == 1.0 else z * scale
            o_ref[:, o0:o0 + width] = zz.astype(o_ref.dtype)


def norm_proj(x_parts, g, w, plan, out_widths, out_dtypes, tm):
    t = sum(p.shape[0] for p in x_parts)
    d = x_parts[0].shape[1]
    n = w.shape[1]
    p_tiles = x_parts[0].shape[0] // tm
    return pl.pallas_call(
        functools.partial(_norm_proj_body, n_x=len(x_parts), p_tiles=p_tiles, plan=plan),
        grid=(t // tm,),
        in_specs=_row_specs(x_parts, tm, p_tiles) + [_const_spec((1, d)), _const_spec((d, n))],
        out_specs=[pl.BlockSpec((tm, ow), lambda i: (i, 0)) for ow in out_widths],
        out_shape=[jax.ShapeDtypeStruct((t, ow), dt) for ow, dt in zip(out_widths, out_dtypes)],
        compiler_params=_params(("parallel",)),
        name="norm_proj",
    )(*x_parts, g, w)


def _mix_ffn_body(*refs, n_x, mix_counts, p_tiles, ff_chunks, final):
    x_refs = refs[:n_x]
    pos = n_x
    a_groups = []
    for count in mix_counts:
        a_groups.append(refs[pos:pos + count])
        pos += count
    wo_refs = refs[pos:pos + len(mix_counts)]
    pos += len(mix_counts)
    g_ref, wg_ref, wu_ref, wd_ref = refs[pos:pos + 4]
    pos += 4
    if final:
        gf_ref = refs[pos]
        pos += 1
    out_refs = refs[pos:-1]
    act_ref = refs[-1]
    x = _row_load(x_refs, p_tiles)
    for a_refs, wo_ref in zip(a_groups, wo_refs):
        x = x + _dot(_row_load(a_refs, p_tiles), wo_ref[...])
    h = _rms(x, g_ref[...]).astype(BF16)
    for c0, cw in ff_chunks:
        gate = _dot(h, wg_ref[:, c0:c0 + cw])
        up = _dot(h, wu_ref[:, c0:c0 + cw])
        act_ref[:, c0:c0 + cw] = (gate * jax.nn.sigmoid(gate) * up).astype(BF16)
    y = x + _dot(act_ref[...], wd_ref[...])
    if final:
        y = _rms(y, gf_ref[...])
    if len(out_refs) == 1:
        out_refs[0][...] = y
    else:
        i = pl.program_id(0)

        @pl.when(i < p_tiles)
        def _():
            out_refs[0][...] = y

        @pl.when(i >= p_tiles)
        def _():
            out_refs[1][...] = y


def mix_ffn(x_parts, mix_in, mix_w, g, wg, wu, wd, g_final, tm, p_rows, split_out):
    t = sum(p.shape[0] for p in x_parts)
    d = x_parts[0].shape[1]
    f = wg.shape[1]
    p_tiles = p_rows // tm
    chunk = 512
    ff_chunks = [(c0, min(chunk, f - c0)) for c0 in range(0, f, chunk)]
    final = g_final is not None
    in_specs = _row_specs(x_parts, tm, p_tiles)
    args = list(x_parts)
    for parts in mix_in:
        in_specs += _row_specs(parts, tm, p_tiles)
        args += list(parts)
    in_specs += [_const_spec(w.shape) for w in mix_w]
    in_specs += [_const_spec((1, d)), _const_spec((d, f)), _const_spec((d, f)), _const_spec((f, d))]
    args += [*mix_w, g, wg, wu, wd]
    if final:
        in_specs.append(_const_spec((1, d)))
        args.append(g_final)
    if split_out:
        assert t - p_rows == tm
        out_specs = [pl.BlockSpec((tm, d), lambda i: (jnp.minimum(i, p_tiles - 1), 0)),
                     pl.BlockSpec((tm, d), lambda i: (0, 0), pipeline_mode=pl.Buffered(1))]
        out_shape = [jax.ShapeDtypeStruct((p_rows, d), F32), jax.ShapeDtypeStruct((t - p_rows, d), F32)]
    else:
        out_specs = pl.BlockSpec((tm, d), lambda i: (i, 0))
        out_shape = jax.ShapeDtypeStruct((t, d), F32)
    return pl.pallas_call(
        functools.partial(_mix_ffn_body, n_x=len(x_parts), mix_counts=tuple(len(p) for p in mix_in),
                          p_tiles=p_tiles, ff_chunks=ff_chunks, final=final),
        grid=(t // tm,),
        in_specs=in_specs,
        out_specs=out_specs,
        out_shape=out_shape,
        scratch_shapes=[pltpu.VMEM((tm, f), BF16)],
        compiler_params=_params(("arbitrary",)),
        name="mix_ffn",
    )(*args)


def _mla_prep_q_body(sm_ref, qn_ref, kvn_ref, wq_ref, rc_ref, rs1_ref, rs2_ref, cos_t_ref, sin_t_ref,
                     ckv_ref, kr_ref, qt_ref):
    sm = sm_ref[...]
    cq = _rms(sm[:, SM_CQ:SM_CQ + Q_RANK], qn_ref[...]).astype(BF16)
    ckv_ref[...] = _rms(sm[:, SM_CKV:SM_CKV + KV_RANK], kvn_ref[...])
    grp = sm[:, SM_KR:SM_KR + LANES]
    half = B_ROPE // 2
    rot = (grp * rc_ref[...] + pltpu.roll(grp, half, 1) * rs1_ref[...]
           + pltpu.roll(grp, LANES - half, 1) * rs2_ref[...])
    kr_ref[...] = rot[:, :B_ROPE]
    qt = _dot_nt(wq_ref[...], cq) * (MLA_SCALE * LOG2E)
    cos_t = cos_t_ref[...]
    sin_t = sin_t_ref[...]
    for h in range(B_HEADS):
        r0 = h * MLA_HEAD_PAD
        x1 = qt[r0 + B_NOPE:r0 + B_NOPE + half]
        x2 = qt[r0 + B_NOPE + half:r0 + B_QK]
        qt_ref[0, r0:r0 + B_NOPE, :] = qt[r0:r0 + B_NOPE].astype(BF16)
        rot_q = jnp.concatenate([x1 * cos_t - x2 * sin_t, x1 * sin_t + x2 * cos_t], axis=0)
        qt_ref[0, r0 + B_NOPE:r0 + B_QK, :] = rot_q.astype(BF16)
        qt_ref[0, r0 + B_QK:r0 + MLA_HEAD_PAD, :] = qt[r0 + B_QK:r0 + MLA_HEAD_PAD].astype(BF16)


def mla_prep_q(small, qn, kvn, wq_t, rope_tabs, tab_index, tm):
    t = small.shape[0]
    rc, rs1, rs2, cos_t, sin_t = rope_tabs
    half = B_ROPE // 2
    row_tab = pl.BlockSpec((tm, LANES), lambda i: (tab_index(i), 0))
    col_tab = pl.BlockSpec((half, tm), lambda i: (0, tab_index(i)))
    return pl.pallas_call(
        _mla_prep_q_body,
        grid=(t // tm,),
        in_specs=[pl.BlockSpec((tm, SM_WIDTH), lambda i: (i, 0)), _const_spec(qn.shape), _const_spec(kvn.shape),
                  _const_spec(wq_t.shape), row_tab, row_tab, row_tab, col_tab, col_tab],
        out_specs=[pl.BlockSpec((tm, KV_RANK), lambda i: (i, 0)), pl.BlockSpec((tm, B_ROPE), lambda i: (i, 0)),
                   pl.BlockSpec((1, B_HEADS * MLA_HEAD_PAD, tm), lambda i: (i, 0, 0))],
        out_shape=[jax.ShapeDtypeStruct((t, KV_RANK), F32), jax.ShapeDtypeStruct((t, B_ROPE), F32),
                   jax.ShapeDtypeStruct((t // tm, B_HEADS * MLA_HEAD_PAD, tm), BF16)],
        compiler_params=_params(("parallel",)),
        name="mla_prep_q",
    )(small, qn, kvn, wq_t, rc, rs1, rs2, cos_t, sin_t)


def _mla_prep_kv_body(ckv_ref, kr_ref, wk_ref, place_ref, wv_ref, kp_ref, vt_ref):
    c = ckv_ref[...].astype(BF16)
    kr = kr_ref[...].astype(BF16)
    kp_ref[...] = (_dot(c, wk_ref[...]) + _dot(kr, place_ref[...])).astype(BF16)
    vt = _dot_nt(wv_ref[...], c).astype(BF16)
    ones = jnp.ones((MLA_V_ROWS - B_VDIM, vt.shape[1]), BF16)
    for h in range(B_HEADS):
        vt_ref[0, h * MLA_V_ROWS:h * MLA_V_ROWS + B_VDIM, :] = vt[h * B_VDIM:(h + 1) * B_VDIM]
        vt_ref[0, h * MLA_V_ROWS + B_VDIM:(h + 1) * MLA_V_ROWS, :] = ones


def mla_prep_kv(ckv, krope, wk_pad, place, wv_t, n_tiles, tm):
    kw = B_HEADS * MLA_HEAD_PAD
    return pl.pallas_call(
        _mla_prep_kv_body,
        grid=(n_tiles,),
        in_specs=[pl.BlockSpec((tm, KV_RANK), lambda i: (i, 0)), pl.BlockSpec((tm, B_ROPE), lambda i: (i, 0)),
                  _const_spec(wk_pad.shape), _const_spec(place.shape), _const_spec(wv_t.shape)],
        out_specs=[pl.BlockSpec((tm, kw), lambda i: (i, 0)), pl.BlockSpec((1, MLA_VT_ROWS, tm), lambda i: (i, 0, 0))],
        out_shape=[jax.ShapeDtypeStruct((n_tiles * tm, kw), BF16),
                   jax.ShapeDtypeStruct((n_tiles, MLA_VT_ROWS, tm), BF16)],
        compiler_params=_params(("parallel",)),
        name="mla_prep_kv",
    )(ckv, krope, wk_pad, place, wv_t)


def _mla_attn_body(qt_ref, kp_ref, vt_ref, o_ref, m_ref, l_ref, acc_ref, *, tq, tk, q0, n_diag):
    i = pl.program_id(1)
    start = q0 + i * tq
    n_full = start // tk
    m_ref[...] = jnp.full(m_ref.shape, NEG, F32)
    l_ref[...] = jnp.zeros(l_ref.shape, F32)
    acc_ref[...] = jnp.zeros(acc_ref.shape, F32)
    key_chunk = lax.broadcasted_iota(jnp.int32, (tk, tq), 0) // CHUNK
    qry_chunk = lax.broadcasted_iota(jnp.int32, (tk, tq), 1) // CHUNK

    def step(j, masked):
        row0 = pl.multiple_of(j * tk, tk)
        if masked:
            visible = key_chunk + (j * tk - start) // CHUNK <= qry_chunk

        def scores(h):
            k_h = kp_ref[pl.ds(row0, tk), h * MLA_HEAD_PAD:(h + 1) * MLA_HEAD_PAD]
            q_h = qt_ref[0, h * MLA_HEAD_PAD:(h + 1) * MLA_HEAD_PAD, :]
            return _dot(k_h, q_h)

        s_next = scores(0)
        for h in range(B_HEADS):
            s = s_next
            if h + 1 < B_HEADS:
                s_next = scores(h + 1)
            if masked:
                s = jnp.where(visible, s, NEG)
            m_prev = m_ref[h:h + 1, :]
            m_new = jnp.maximum(m_prev, jnp.max(s, axis=0, keepdims=True))
            alpha = jnp.exp2(m_prev - m_new)
            p = jnp.exp2(s - m_new).astype(BF16)
            m_ref[h:h + 1, :] = m_new
            pv = _dot(vt_ref[j, h * MLA_V_ROWS:(h + 1) * MLA_V_ROWS, :], p)
            l_ref[h:h + 1, :] = alpha * l_ref[h:h + 1, :] + pv[B_VDIM:B_VDIM + 1]
            rows = slice(h * B_VDIM, (h + 1) * B_VDIM)
            acc_ref[rows, :] = alpha * acc_ref[rows, :] + pv[:B_VDIM]

    def full_step(j, carry):
        step(j, False)
        return carry

    lax.fori_loop(0, n_full, full_step, 0)
    for d in range(n_diag):
        step(n_full + d, True)
    for h in range(B_HEADS):
        rows = slice(h * B_VDIM, (h + 1) * B_VDIM)
        acc_ref[rows, :] = acc_ref[rows, :] / l_ref[h:h + 1, :]
    o_ref[...] = jnp.transpose(acc_ref[...]).astype(o_ref.dtype)


def mla_attn(qt, kp, vt, groups, nq, tq, tk, n_blocks, q0, q_frames):
    kw = B_HEADS * MLA_HEAD_PAD
    assert q0 % tk == 0 and (tq % tk == 0 or (nq == 1 and q_frames <= tk))
    n_diag = pl.cdiv(q_frames, tk)
    return pl.pallas_call(
        functools.partial(_mla_attn_body, tq=tq, tk=tk, q0=q0, n_diag=n_diag),
        grid=(groups, nq),
        in_specs=[pl.BlockSpec((1, kw, tq), lambda g, i: (g * nq + i, 0, 0)),
                  pl.BlockSpec((n_blocks * tk, kw), lambda g, i: (g, 0), pipeline_mode=pl.Buffered(1)),
                  pl.BlockSpec((n_blocks, MLA_VT_ROWS, tk), lambda g, i: (g, 0, 0), pipeline_mode=pl.Buffered(1))],
        out_specs=pl.BlockSpec((tq, B_WIDTH), lambda g, i: (g * nq + i, 0)),
        out_shape=jax.ShapeDtypeStruct((groups * nq * tq, B_WIDTH), BF16),
        scratch_shapes=[pltpu.VMEM((B_HEADS, tq), F32), pltpu.VMEM((B_HEADS, tq), F32),
                        pltpu.VMEM((B_WIDTH, tq), F32)],
        compiler_params=_params(("parallel", "arbitrary")),
        name="mla_attn",
    )(qt, kp, vt)


def _mla_sample_body(sm_ref, qn_ref, wq_ref, rc_ref, rs1_ref, rs2_ref, wabs_ref, sel_ref, cckv_ref, ckr_ref,
                     nckv_ref, nkr_ref, wuv_ref, o_ref, m_ref, l_ref, acc_ref, *, n_chunks, chunk):
    frames = sm_ref.shape[0]
    half = B_ROPE // 2
    cq = _rms(sm_ref[:, SM_CQ:SM_CQ + Q_RANK], qn_ref[...]).astype(BF16)
    qf = _dot(cq, wq_ref[...]) * (MLA_SCALE * LOG2E)
    width = qf.shape[1]
    qb = (qf * rc_ref[...] + pltpu.roll(qf, half, 1) * rs1_ref[...]
          + pltpu.roll(qf, width - half, 1) * rs2_ref[...]).astype(BF16)
    q_lat, q_rope = [], []
    for h in range(B_HEADS):
        q_h = qb[:, h * MLA_HEAD_PAD:(h + 1) * MLA_HEAD_PAD]
        q_lat.append(_dot(q_h, wabs_ref[h]).astype(BF16))
        q_rope.append(_dot(q_h, sel_ref[...]).astype(BF16))
    q_lat = jnp.concatenate(q_lat, axis=0)
    q_rope = jnp.concatenate(q_rope, axis=0)
    m_ref[...] = jnp.full(m_ref.shape, NEG, F32)
    l_ref[...] = jnp.zeros(l_ref.shape, F32)
    acc_ref[...] = jnp.zeros(acc_ref.shape, F32)

    def attend(lat, rope):
        lat = lat.astype(BF16)
        s = _dot_nt(q_lat, lat) + _dot_nt(q_rope, rope.astype(BF16))
        m_prev = m_ref[...]
        m_new = jnp.maximum(m_prev, jnp.max(s, axis=1, keepdims=True))
        alpha = jnp.exp2(m_prev - m_new)
        p = jnp.exp2(s - m_new)
        l_ref[...] = alpha * l_ref[...] + jnp.sum(p, axis=1, keepdims=True)
        acc_ref[...] = alpha * acc_ref[...] + _dot(p.astype(BF16), lat)
        m_ref[...] = m_new

    def cached(c, carry):
        rows = pl.ds(pl.multiple_of(c * chunk, chunk), chunk)
        attend(cckv_ref[0, 0, rows, :], ckr_ref[0, 0, rows, :])
        return carry

    lax.fori_loop(0, n_chunks, cached, 0)
    attend(nckv_ref[...], nkr_ref[...])
    o_lat = (acc_ref[...] / l_ref[...]).astype(BF16)
    out = _dot(o_lat[:frames], wuv_ref[0])
    for h in range(1, B_HEADS):
        out = out + _dot(o_lat[h * frames:(h + 1) * frames], wuv_ref[h])
    o_ref[...] = out.astype(o_ref.dtype)


def mla_sample(small, ckv, krope, cache_ckv, cache_krope, layer, qn, wq, rope_full, wabs, sel, wuv_place,
               streams, frames, row0):
    past = cache_ckv.shape[2]
    chunk = 512
    base = row0 // frames
    rows_q = B_HEADS * frames
    rc, rs1, rs2 = rope_full
    return pl.pallas_call(
        functools.partial(_mla_sample_body, n_chunks=past // chunk, chunk=chunk),
        grid=(streams,),
        in_specs=[pl.BlockSpec((frames, SM_WIDTH), lambda g: (base + g, 0)), _const_spec(qn.shape),
                  _const_spec(wq.shape), _const_spec(rc.shape), _const_spec(rs1.shape), _const_spec(rs2.shape),
                  _const_spec(wabs.shape), _const_spec(sel.shape),
                  pl.BlockSpec((1, 1, past, KV_RANK), lambda g: (layer, g, 0, 0)),
                  pl.BlockSpec((1, 1, past, B_ROPE), lambda g: (layer, g, 0, 0)),
                  pl.BlockSpec((frames, KV_RANK), lambda g: (base + g, 0)),
                  pl.BlockSpec((frames, B_ROPE), lambda g: (base + g, 0)),
                  _const_spec(wuv_place.shape)],
        out_specs=pl.BlockSpec((frames, B_WIDTH), lambda g: (g, 0)),
        out_shape=jax.ShapeDtypeStruct((streams * frames, B_WIDTH), BF16),
        scratch_shapes=[pltpu.VMEM((rows_q, 1), F32), pltpu.VMEM((rows_q, 1), F32),
                        pltpu.VMEM((rows_q, KV_RANK), F32)],
        compiler_params=_params(("parallel",)),
        name="mla_sample",
    )(small, qn, wq, rc, rs1, rs2, wabs, sel, cache_ckv, cache_krope, ckv, krope, wuv_place)


def _scan_rows(x, op, fill, length):
    row = lax.broadcasted_iota(jnp.int32, x.shape, 0)
    shift = 1
    while shift < length:
        moved = pltpu.roll(x, shift, 0)
        x = op(x, jnp.where(row >= shift, moved, fill))
        shift *= 2
    return x


def _mlstm_body(*refs, blk, per_step):
    n_in = 5 * per_step
    bias_ref, hn_ref, c0_ref, m0_ref, ha_ref, c_out_ref, m_out_ref, c_scr, m_scr = refs[n_in:]
    dh = A_HEAD_DIM
    step_idx = pl.program_id(1)

    @pl.when(step_idx == 0)
    def _():
        c_scr[...] = c0_ref[...]
        m_scr[...] = m0_ref[...]

    causal = (lax.broadcasted_iota(jnp.int32, (blk, blk), 0) >= lax.broadcasted_iota(jnp.int32, (blk, blk), 1))
    ones_col = (lax.broadcasted_iota(jnp.int32, (blk, dh), 1) == 0).astype(BF16)
    pad = max(blk, LANES) - blk
    prep = []
    for u in range(per_step):
        gt_ref = refs[5 * u + 4]
        gates = gt_ref[...] + bias_ref[...]
        log_f = jax.nn.log_sigmoid(gates)
        b_all = pltpu.roll(_scan_rows(log_f, jnp.add, 0.0, blk), LANES - A_HEADS, 1)
        a_all = gates - b_all
        amax_all = _scan_rows(a_all, jnp.maximum, NEG, blk)
        a_sq = a_all if pad == 0 else jnp.concatenate([a_all, jnp.zeros((pad, LANES), F32)], axis=0)
        prep.append((a_all, b_all, amax_all, jnp.transpose(a_sq)))
    for h in range(A_HEADS):
        for u in range(per_step):
            q_ref, k_ref, v_ref, o_ref = refs[5 * u:5 * u + 4]
            a_all, b_all, amax_all, a_rows = prep[u]
            lane = GATE_I_LANE + h
            cols = slice(h * dh, (h + 1) * dh)
            q = q_ref[:, cols]
            k = k_ref[:, cols]
            v_ext = jnp.concatenate([v_ref[:, cols], ones_col], axis=1)
            a_col = a_all[:, lane:lane + 1]
            b_col = b_all[:, lane:lane + 1]
            a_row = a_rows[lane:lane + 1, :blk]
            m_prev = m_scr[u, h:h + 1, 0:1]
            run_max = jnp.maximum(amax_all[:, lane:lane + 1], m_prev)
            decay_mat = jnp.exp(jnp.where(causal, a_row - run_max, NEG))
            p = (_dot_nt(q, k) * decay_mat).astype(BF16)
            state = c_scr[u, h]
            w_inter = jnp.exp(m_prev - run_max)
            numden = w_inter * _dot(q, state.astype(BF16)) + _dot(p, v_ext)
            den = numden[:, dh:dh + 1]
            hh = numden[:, :dh] / jnp.maximum(jnp.abs(den), jnp.exp(-(b_col + run_max)))
            hh = _rms(hh, hn_ref[:, cols])
            ha_ref[u, :, cols] = (hh * jax.nn.sigmoid(o_ref[:, cols])).astype(ha_ref.dtype)
            max_last = run_max[blk - 1:blk, :]
            w_k = jnp.exp(a_col - max_last)
            k_w = (k.astype(F32) * w_k).astype(BF16)
            c_scr[u, h] = jnp.exp(m_prev - max_last) * state + _dot_tn(k_w, v_ext)
            m_scr[u, h:h + 1, :] = jnp.broadcast_to(b_col[blk - 1:blk, :] + max_last, (1, LANES))

    @pl.when(step_idx == pl.num_programs(1) - 1)
    def _():
        c_out_ref[...] = c_scr[...]
        m_out_ref[...] = m_scr[...]


def mlstm(qkv, o32, small, bias, hnorm, c0, m0, groups, steps, blk, row0, per_step):
    base = row0 // blk
    gate_block = SM_KR // LANES
    state_shape = (per_step, A_HEADS, A_HEAD_DIM, 2 * A_HEAD_DIM)

    def rows(u, col):
        return lambda g, s: (base + (g * per_step + u) * steps + s, col)

    in_specs, args = [], []
    for u in range(per_step):
        in_specs += [pl.BlockSpec((blk, A_WIDTH), rows(u, 0)), pl.BlockSpec((blk, A_WIDTH), rows(u, 1)),
                     pl.BlockSpec((blk, A_WIDTH), rows(u, 2)), pl.BlockSpec((blk, A_WIDTH), rows(u, 0)),
                     pl.BlockSpec((blk, LANES), rows(u, gate_block))]
        args += [qkv, qkv, qkv, o32, small]
    in_specs += [_const_spec((1, LANES)), _const_spec((1, A_WIDTH)),
                 pl.BlockSpec(state_shape, lambda g, s: (g, 0, 0, 0)),
                 pl.BlockSpec((per_step, 8, LANES), lambda g, s: (g, 0, 0))]
    ha, c_out, m_out = pl.pallas_call(
        functools.partial(_mlstm_body, blk=blk, per_step=per_step),
        grid=(groups // per_step, steps),
        in_specs=in_specs,
        out_specs=[pl.BlockSpec((per_step, blk, A_WIDTH), lambda g, s: (g, s, 0)),
                   pl.BlockSpec(state_shape, lambda g, s: (g, 0, 0, 0)),
                   pl.BlockSpec((per_step, 8, LANES), lambda g, s: (g, 0, 0))],
        out_shape=[jax.ShapeDtypeStruct((groups, steps * blk, A_WIDTH), BF16),
                   jax.ShapeDtypeStruct((groups, A_HEADS, A_HEAD_DIM, 2 * A_HEAD_DIM), F32),
                   jax.ShapeDtypeStruct((groups, 8, LANES), F32)],
        scratch_shapes=[pltpu.VMEM(state_shape, F32), pltpu.VMEM((per_step, 8, LANES), F32)],
        compiler_params=_params(("parallel", "arbitrary")),
        name="mlstm",
    )(*args, bias, hnorm, c0, m0)
    return ha.reshape(groups * steps * blk, A_WIDTH), c_out, m_out


def _band_proj_body(x_ref, g_ref, wk_ref, wq_t_ref, wv_t_ref, wkv_ref, k_ref, qt_ref, vt_ref, tail_ref,
                    *, tiles_per_seq, p_tiles):
    i = pl.program_id(0)
    h = _rms(x_ref[...], g_ref[...]).astype(BF16)
    step = 512
    for c0 in range(0, C_WIDTH, step):
        k_ref[:, c0:c0 + step] = _dot(h, wk_ref[:, c0:c0 + step]).astype(BF16)
        qt = _dot_nt(wq_t_ref[c0:c0 + step, :], h) * (C_SCALE * LOG2E)
        qt_ref[0, c0:c0 + step, :] = qt.astype(BF16)
        vt_ref[0, c0:c0 + step, :] = _dot_nt(wv_t_ref[c0:c0 + step, :], h).astype(BF16)

    is_tail = ((i + 1) % tiles_per_seq == 0) | (i >= p_tiles)

    @pl.when(is_tail)
    def _():
        for c0 in range(0, 2 * C_WIDTH, step):
            tail_ref[:, c0:c0 + step] = _dot(h, wkv_ref[:, c0:c0 + step])

    @pl.when(jnp.logical_not(is_tail) & (i % tiles_per_seq == 0))
    def _():
        tail_ref[...] = jnp.zeros(tail_ref.shape, F32)


def band_proj(x, g, wk, wq_t, wv_t, wkv, tm, tiles_per_seq, p_tiles, n_seq):
    t, d = x.shape
    n_tiles = t // tm
    n_tail = n_seq + n_tiles - p_tiles
    cw = C_WIDTH

    def tail_index(i):
        return (jnp.where(i < p_tiles, i // tiles_per_seq, n_seq + i - p_tiles), 0)

    return pl.pallas_call(
        functools.partial(_band_proj_body, tiles_per_seq=tiles_per_seq, p_tiles=p_tiles),
        grid=(n_tiles,),
        in_specs=[pl.BlockSpec((tm, d), lambda i: (i, 0)), _const_spec((1, d)), _const_spec((d, cw)),
                  _const_spec((cw, d)), _const_spec((cw, d)), _const_spec((d, 2 * cw))],
        out_specs=[pl.BlockSpec((tm, cw), lambda i: (i, 0)), pl.BlockSpec((1, cw, tm), lambda i: (i, 0, 0)),
                   pl.BlockSpec((1, cw, tm), lambda i: (i, 0, 0)), pl.BlockSpec((tm, 2 * cw), tail_index)],
        out_shape=[jax.ShapeDtypeStruct((t, cw), BF16), jax.ShapeDtypeStruct((n_tiles, cw, tm), BF16),
                   jax.ShapeDtypeStruct((n_tiles, cw, tm), BF16), jax.ShapeDtypeStruct((n_tail * tm, 2 * cw), F32)],
        compiler_params=_params(("arbitrary",)),
        name="band_proj",
    )(x, g, wk, wq_t, wv_t, wkv)


def _band_table_body(base_ref, tab_ref):
    tq = BAND_TILE
    shape = (tq, BAND_ROLL_WIDTH)
    row = lax.broadcasted_iota(jnp.int32, shape, 0)
    tab = jnp.broadcast_to(base_ref[0], shape)
    shift = 1
    while shift < tq:
        tab = jnp.where((row & shift) != 0, pltpu.roll(tab, shift, 1), tab)
        shift *= 2
    win = lax.broadcasted_iota(jnp.int32, (tq, BAND_WINDOW), 1) // CHUNK
    qch = lax.broadcasted_iota(jnp.int32, (tq, BAND_WINDOW), 0) // CHUNK
    valid = (win >= qch) & (win <= qch + LEFT_CHUNKS)
    tab_ref[0] = jnp.transpose(jnp.where(valid, tab[:, :BAND_WINDOW] * LOG2E, NEG))


def band_table(base_rows):
    return pl.pallas_call(
        _band_table_body,
        grid=(C_HEADS,),
        in_specs=[pl.BlockSpec((1, 1, BAND_ROLL_WIDTH), lambda h: (h, 0, 0))],
        out_specs=pl.BlockSpec((1, BAND_WINDOW, BAND_TILE), lambda h: (h, 0, 0)),
        out_shape=jax.ShapeDtypeStruct((C_HEADS, BAND_WINDOW, BAND_TILE), F32),
        compiler_params=_params(("parallel",)),
        name="band_table",
    )(base_rows)


def _band_body(qt_ref, *refs, clamp_start):
    nb = BAND_WINDOW // BAND_TILE
    k_refs = refs[:nb]
    v_refs = refs[nb:2 * nb]
    tab_ref, o_ref, acc_ref = refs[2 * nb:]
    tq = BAND_TILE
    i = pl.program_id(1)
    row_half = lax.broadcasted_iota(jnp.int32, (LANES, tq), 0) // C_HEAD_DIM
    ones = jnp.ones((16, tq), BF16)

    def run(first_tiles):
        def scores(h):
            pair, half = divmod(h, 2)
            q_pair = qt_ref[0, pair * LANES:(pair + 1) * LANES, :]
            q_h = jnp.where(row_half == half, q_pair, jnp.zeros_like(q_pair))
            parts = []
            for j in range(nb):
                s_j = _dot(k_refs[j][:, pair * LANES:(pair + 1) * LANES], q_h)
                if first_tiles and j < nb - 1:
                    s_j = jnp.where(i + j < nb - 1, NEG, s_j)
                parts.append(s_j)
            return jnp.concatenate(parts, axis=0)

        s_next = scores(0)
        for h in range(C_HEADS):
            s = s_next + tab_ref[h]
            if h + 1 < C_HEADS:
                s_next = scores(h + 1)
            m = jnp.max(s, axis=0, keepdims=True)
            p = jnp.exp2(s - m).astype(BF16)
            rows = slice(h * C_HEAD_DIM, (h + 1) * C_HEAD_DIM)
            pv = None
            for j in range(nb):
                v_ext = jnp.concatenate([v_refs[j][0, rows, :], ones], axis=0)
                part = _dot(v_ext, p[j * BAND_TILE:(j + 1) * BAND_TILE])
                pv = part if pv is None else pv + part
            acc_ref[rows, :] = pv[:C_HEAD_DIM] / pv[C_HEAD_DIM:C_HEAD_DIM + 1]

    if clamp_start:
        @pl.when(i < nb - 1)
        def _():
            run(True)

        @pl.when(i >= nb - 1)
        def _():
            run(False)
    else:
        run(False)
    o_ref[...] = jnp.transpose(acc_ref[...]).astype(o_ref.dtype)


def band_attn(qt_arr, qt_map, k_arr, k_maps, vt_arr, vt_maps, tab, groups, tiles, clamp_start):
    tq = BAND_TILE
    in_specs = [pl.BlockSpec((1, C_WIDTH, tq), qt_map)]
    in_specs += [pl.BlockSpec((tq, C_WIDTH), m) for m in k_maps]
    in_specs += [pl.BlockSpec((1, C_WIDTH, tq), m) for m in vt_maps]
    in_specs.append(_const_spec(tab.shape))
    return pl.pallas_call(
        functools.partial(_band_body, clamp_start=clamp_start),
        grid=(groups, tiles),
        in_specs=in_specs,
        out_specs=pl.BlockSpec((tq, C_WIDTH), lambda g, i: (g * tiles + i, 0)),
        out_shape=jax.ShapeDtypeStruct((groups * tiles * tq, C_WIDTH), BF16),
        scratch_shapes=[pltpu.VMEM((C_WIDTH, tq), F32)],
        compiler_params=_params(("parallel", "arbitrary")),
        name="band_attn",
    )(qt_arr, *([k_arr] * len(k_maps)), *([vt_arr] * len(vt_maps)), tab)


def _cache_roll_body(ck_ref, cv_ref, nk_ref, nv_ref, ok_ref, ov_ref, sem, *, drop):
    keep = ck_ref.shape[2] - drop
    copies = []
    for n, (c_ref, n_ref, o_ref) in enumerate(((ck_ref, nk_ref, ok_ref), (cv_ref, nv_ref, ov_ref))):
        copies.append(pltpu.make_async_copy(c_ref.at[:, :, pl.ds(drop, keep)], o_ref.at[:, :, pl.ds(0, keep)],
                                            sem.at[2 * n]))
        copies.append(pltpu.make_async_copy(n_ref, o_ref.at[:, :, pl.ds(keep, drop)], sem.at[2 * n + 1]))
    for copy in copies:
        copy.start()
    for copy in copies:
        copy.wait()


def cache_roll(cache_k, cache_v, new_k, new_v):
    hbm = pl.BlockSpec(memory_space=pl.ANY)
    out = jax.ShapeDtypeStruct(cache_k.shape, cache_k.dtype)
    return pl.pallas_call(
        functools.partial(_cache_roll_body, drop=new_k.shape[2]),
        in_specs=[hbm] * 4,
        out_specs=[hbm] * 2,
        out_shape=[out, out],
        scratch_shapes=[pltpu.SemaphoreType.DMA((4,))],
        name="cache_roll",
    )(cache_k, cache_v, new_k, new_v)


def _rope_tables(positions):
    half = B_ROPE // 2
    inv = ROPE_THETA ** (-jnp.arange(half, dtype=F32) / half)
    ang = positions.astype(F32)[:, None] * inv[None, :]
    cos, sin = jnp.cos(ang), jnp.sin(ang)
    n = positions.shape[0]
    zeros = jnp.zeros((n, LANES - B_ROPE), F32)
    zh = jnp.zeros((n, half), F32)
    rc = jnp.concatenate([cos, cos, zeros], axis=1)
    rs1 = jnp.concatenate([zh, sin, zeros], axis=1)
    rs2 = jnp.concatenate([-sin, zh, zeros], axis=1)
    return rc, rs1, rs2, cos.T, sin.T


def _band_base_rows(rel_bias):
    x = jnp.arange(BAND_ROLL_WIDTH)
    rel = jnp.where(x < BAND_WINDOW, BAND_LEFT - x, BAND_LEFT + 1)
    idx = jnp.clip(rel, -MAX_REL, MAX_REL) + MAX_REL
    return rel_bias[:, None, idx]


def kernel(x_prompt, x_sample, cache_mla_ckv, cache_mla_krope, state_mlstm_C, state_mlstm_n, state_mlstm_m,
           cache_band_k, cache_band_v, norm_mix, norm_ffn, norm_final, w_in_ab, b_gates, mlstm_hnorm,
           mla_q_norm, mla_kv_norm, mla_w_uq, mla_w_uk, mla_w_uv, w_out_ab, w_qkv_c, w_out_c, rel_bias_c,
           w_gate, w_up, w_down):
    batch, seq, d = x_prompt.shape
    dec_batch, dec_seq, _ = x_sample.shape
    past = cache_mla_ckv.shape[2]
    band_buf = cache_band_k.shape[2]
    depth = norm_mix.shape[0]
    tp = batch * seq
    ts = dec_batch * dec_seq
    t = tp + ts
    tm_big = 1024
    tm = MLA_Q_TILE
    tk = MLA_KV_BLOCK
    assert d == D_MODEL and dec_seq == CHUNK and band_buf == BAND_LEFT and past % tk == 0
    assert tp % tm_big == 0 and ts % tm_big == 0 and seq % tm == 0 and tm % dec_seq == 0 and tm % tk == 0

    assert past % CHUNK == 0 and ts == tm_big
    x_parts = (jnp.concatenate([x_prompt.reshape(tp, d), x_sample.reshape(ts, d)], axis=0),)

    pos_tab = jnp.concatenate([jnp.arange(seq, dtype=jnp.int32),
                               past + (jnp.arange(tm, dtype=jnp.int32) % dec_seq)])
    rope_tabs = _rope_tables(pos_tab)
    head_pat = jnp.concatenate([jnp.ones((dec_seq, B_NOPE), F32), jnp.zeros((dec_seq, LANES - B_NOPE), F32)], axis=1)
    roll_pad = ((0, 0), (B_NOPE, LANES - B_QK))
    rope_full = tuple(jnp.tile(tab, (1, B_HEADS)) for tab in (
        head_pat + jnp.pad(rope_tabs[0][seq:seq + dec_seq, :B_ROPE], roll_pad),
        jnp.pad(rope_tabs[1][seq:seq + dec_seq, :B_ROPE], roll_pad),
        jnp.pad(rope_tabs[2][seq:seq + dec_seq, :B_ROPE], roll_pad)))
    sel = jnp.pad(jnp.eye(B_ROPE, dtype=F32), ((B_NOPE, MLA_HEAD_PAD - B_QK), (0, 0))).astype(BF16)
    band_new = {"k": [], "v": []}
    p_tiles = tp // tm
    seq_tiles = seq // tm

    def tab_index(i):
        return jnp.where(i < p_tiles, i % seq_tiles, seq_tiles)

    outs = {k: [] for k in ("p_ckv", "p_kr", "p_C", "p_n", "p_m", "p_bk", "p_bv",
                            "s_ckv", "s_kr", "s_C", "s_n", "s_m", "s_bk", "s_bv")}
    a4 = 4 * A_WIDTH
    for layer in range(depth):
        j = layer // 2
        g_mix = norm_mix[layer][None, :]
        g_ffn = norm_ffn[layer][None, :]
        last = layer == depth - 1
        g_fin = norm_final[None, :] if last else None
        ffn_w = (w_gate[layer].astype(BF16), w_up[layer].astype(BF16), w_down[layer].astype(BF16))
        if layer % 2 == 0:
            w = w_in_ab[j]
            gate_cols = jnp.concatenate([w[:, a4 + 2 * A_HEADS + Q_RANK + KV_RANK:], w[:, a4:a4 + 2 * A_HEADS],
                                         jnp.zeros((d, LANES - B_ROPE - 2 * A_HEADS), F32)], axis=1)
            w_all = jnp.concatenate([w[:, :a4], w[:, a4 + 2 * A_HEADS:a4 + 2 * A_HEADS + Q_RANK + KV_RANK],
                                     gate_cols], axis=1).astype(BF16)
            aw = A_WIDTH
            plan = [(0, aw, [(0, 0, 1.0)]), (aw, aw, [(0, aw, A_HEAD_DIM ** -0.5)]), (2 * aw, aw, [(0, 2 * aw, 1.0)]),
                    (3 * aw, aw, [(1, 0, 1.0)]), (4 * aw, Q_RANK, [(2, SM_CQ, 1.0)]),
                    (4 * aw + Q_RANK, KV_RANK + LANES, [(2, SM_CKV, 1.0)])]
            qkv, o32, small = norm_proj(x_parts, g_mix, w_all, plan, (3 * aw, aw, SM_WIDTH), (BF16, F32, F32), tm_big)

            wq = mla_w_uq[j].reshape(Q_RANK, B_HEADS, B_QK)
            wq_t = jnp.pad(wq, ((0, 0), (0, 0), (0, MLA_HEAD_PAD - B_QK))).reshape(Q_RANK, -1).T.astype(BF16)
            ckv, krope, qt = mla_prep_q(small, mla_q_norm[j][None, :], mla_kv_norm[j][None, :], wq_t,
                                        rope_tabs, tab_index, tm)
            wk_pad = jnp.pad(mla_w_uk[j], ((0, 0), (0, 0), (0, MLA_HEAD_PAD - B_NOPE))).reshape(KV_RANK, -1).astype(BF16)
            place = jnp.pad(jnp.eye(B_ROPE, dtype=F32), ((0, 0), (B_NOPE, MLA_HEAD_PAD - B_QK)))
            place = jnp.tile(place, (1, B_HEADS)).astype(BF16)
            wv_t = mla_w_uv[j].reshape(KV_RANK, B_WIDTH).T.astype(BF16)
            kp_p, vt_p = mla_prep_kv(ckv, krope, wk_pad, place, wv_t, tp // tk, tk)
            ckv_s = ckv[tp:].reshape(dec_batch, dec_seq, KV_RANK)
            kr_s = krope[tp:].reshape(dec_batch, dec_seq, B_ROPE)

            hb_p = mla_attn(qt, kp_p, vt_p, batch, seq_tiles, tm, tk, seq // tk, 0, tm)
            wabs = jnp.pad(jnp.transpose(mla_w_uk[j], (1, 2, 0)),
                           ((0, 0), (0, MLA_HEAD_PAD - B_NOPE), (0, 0))).astype(BF16)
            wuv_place = jnp.einsum("rhv,hg->hrgv", mla_w_uv[j], jnp.eye(B_HEADS, dtype=F32))
            wuv_place = wuv_place.reshape(B_HEADS, KV_RANK, B_WIDTH).astype(BF16)
            hb_s = mla_sample(small, ckv, krope, cache_mla_ckv, cache_mla_krope, j, mla_q_norm[j][None, :],
                              wq_t.T, rope_full, wabs, sel, wuv_place, dec_batch, dec_seq, tp)

            bias = jnp.zeros((1, LANES), F32).at[0, GATE_I_LANE:GATE_I_LANE + 2 * A_HEADS].set(b_gates[j])
            hn = mlstm_hnorm[j][None, :]
            c0_p = jnp.zeros((batch, A_HEADS, A_HEAD_DIM, 2 * A_HEAD_DIM), F32)
            m0_p = jnp.zeros((batch, 8, LANES), F32)
            blk_p = 256
            ha_p, c_p, m_p = mlstm(qkv, o32, small, bias, hn, c0_p, m0_p, batch, seq // blk_p, blk_p, 0, batch)
            c0_s = jnp.concatenate([state_mlstm_C[j], state_mlstm_n[j][..., None],
                                    jnp.zeros((dec_batch, A_HEADS, A_HEAD_DIM, A_HEAD_DIM - 1), F32)], axis=-1)
            m0_s = jnp.broadcast_to(jnp.pad(state_mlstm_m[j], ((0, 0), (0, 8 - A_HEADS)))[..., None],
                                    (dec_batch, 8, LANES))
            ha_s, c_s, m_s = mlstm(qkv, o32, small, bias, hn, c0_s, m0_s, dec_batch, 1, dec_seq, tp, 4)

            wo = w_out_ab[j].astype(BF16)
            x_parts = mix_ffn(x_parts, ((ha_p, ha_s), (hb_p, hb_s)), (wo[:A_WIDTH], wo[A_WIDTH:]), g_ffn, *ffn_w,
                              g_fin, tm_big, tp, False)
            x_parts = (x_parts,)

            outs["p_ckv"].append(ckv[:tp].reshape(batch, seq, KV_RANK))
            outs["p_kr"].append(krope[:tp].reshape(batch, seq, B_ROPE))
            outs["p_C"].append(c_p[..., :A_HEAD_DIM])
            outs["p_n"].append(c_p[..., A_HEAD_DIM])
            outs["p_m"].append(m_p[:, :A_HEADS, 0])
            outs["s_ckv"].append(ckv_s)
            outs["s_kr"].append(kr_s)
            outs["s_C"].append(c_s[..., :A_HEAD_DIM])
            outs["s_n"].append(c_s[..., A_HEAD_DIM])
            outs["s_m"].append(m_s[:, :A_HEADS, 0])
        else:
            cw = C_WIDTH
            wqkv = w_qkv_c[j].astype(BF16)
            tmb = 512
            tps = seq // tmb
            ptb = tp // tmb
            assert band_buf <= tmb and seq % tmb == 0 and ts % tmb == 0 and tmb % BAND_TILE == 0
            (x,) = x_parts
            k_bf, qt_all, vt_all, kv_tail = band_proj(x, g_mix, wqkv[:, cw:2 * cw], wqkv[:, :cw].T,
                                                       wqkv[:, 2 * cw:].T, wqkv[:, cw:], tmb, tps, ptb, batch)
            tab = band_table(_band_base_rows(rel_bias_c[j]))
            tq = BAND_TILE
            nb = BAND_WINDOW // tq
            sub = tmb // tq
            tiles = seq // tq

            def block(g, i, jj):
                return g * tiles + jnp.maximum(i + jj - (nb - 1), 0)

            o_p = band_attn(
                qt_all, lambda g, i: ((g * tiles + i) // sub, 0, (g * tiles + i) % sub),
                k_bf, [functools.partial(lambda g, i, jj: (block(g, i, jj), 0), jj=jj) for jj in range(nb)],
                vt_all, [functools.partial(lambda g, i, jj: (block(g, i, jj) // sub, 0, block(g, i, jj) % sub), jj=jj)
                         for jj in range(nb)],
                tab, batch, tiles, True)

            tail_s = kv_tail[batch * tmb:]
            k_new = tail_s[:, :cw].reshape(dec_batch, dec_seq, cw)
            v_new = tail_s[:, cw:].reshape(dec_batch, dec_seq, cw)

            def per_stream_t(a):
                a = a.reshape(-1, cw, tmb // dec_seq, dec_seq)
                return jnp.moveaxis(a, 2, 1).reshape(dec_batch, cw, dec_seq)

            qt_s = jnp.pad(per_stream_t(qt_all[ptb:]), ((0, 0), (0, 0), (0, tq - dec_seq)))
            kw_s = jnp.concatenate([cache_band_k[j].reshape(dec_batch, band_buf, cw).astype(BF16),
                                    k_bf[tp:].reshape(dec_batch, dec_seq, cw),
                                    jnp.zeros((dec_batch, tq - dec_seq, cw), BF16)], axis=1).reshape(-1, cw)
            vt_s = jnp.concatenate([jnp.swapaxes(cache_band_v[j].reshape(dec_batch, band_buf, cw), 1, 2).astype(BF16),
                                    per_stream_t(vt_all[ptb:]),
                                    jnp.zeros((dec_batch, cw, tq - dec_seq), BF16)], axis=2)
            o_s = band_attn(
                qt_s, lambda g, i: (g, 0, 0),
                kw_s, [functools.partial(lambda g, i, jj: (g * nb + jj, 0), jj=jj) for jj in range(nb)],
                vt_s, [functools.partial(lambda g, i, jj: (g, 0, jj), jj=jj) for jj in range(nb)],
                tab, dec_batch, 1, False)
            o_s = o_s.reshape(dec_batch, tq, cw)[:, :dec_seq].reshape(ts, cw)
            x_parts = mix_ffn(x_parts, ((o_p, o_s),), (w_out_c[j].astype(BF16),), g_ffn, *ffn_w, g_fin, tm_big,
                              tp, False)
            x_parts = (x_parts,)

            kv_p =kv_tail[:batch * tmb].reshape(batch, tmb, 2, C_HEADS, C_HEAD_DIM)[:, tmb - band_buf:]
            outs["p_bk"].append(kv_p[:, :, 0])
            outs["p_bv"].append(kv_p[:, :, 1])
            band_new["k"].append(k_new.reshape(dec_batch, dec_seq, C_HEADS, C_HEAD_DIM))
            band_new["v"].append(v_new.reshape(dec_batch, dec_seq, C_HEADS, C_HEAD_DIM))

    y_prompt = x_parts[0][:tp].reshape(batch, seq, d)
    y_sample = x_parts[0][tp:].reshape(dec_batch, dec_seq, d)
    s_bk, s_bv = cache_roll(cache_band_k, cache_band_v, jnp.stack(band_new["k"]), jnp.stack(band_new["v"]))
    st = {k: jnp.stack(v) for k, v in outs.items() if v}
    return (y_prompt, y_sample, st["p_ckv"], st["p_kr"], st["p_C"], st["p_n"], st["p_m"], st["p_bk"], st["p_bv"],
            st["s_ckv"], st["s_kr"], st["s_C"], st["s_n"], st["s_m"], s_bk, s_bv)
```

```python
import functools
import math

import jax
import jax.numpy as jnp
from jax import lax
from jax.experimental import pallas as pl
from jax.experimental.pallas import tpu as pltpu

F32 = jnp.float32
BF16 = jnp.bfloat16

D_MODEL = 1024
CHUNK = 64
A_HEADS = 4
A_HEAD_DIM = 128
A_WIDTH = A_HEADS * A_HEAD_DIM
B_HEADS = 8
B_NOPE = 64
B_ROPE = 32
B_VDIM = 64
B_QK = B_NOPE + B_ROPE
B_WIDTH = B_HEADS * B_VDIM
Q_RANK = 384
KV_RANK = 256
ROPE_THETA = 10000.0
MLA_SCALE = B_QK ** -0.5
C_HEADS = 16
C_HEAD_DIM = 64
C_WIDTH = C_HEADS * C_HEAD_DIM
LEFT_CHUNKS = 8
MAX_REL = 128
C_SCALE = C_HEAD_DIM ** -0.5
EPS = 1e-6
NEG = -1e30
LOG2E = math.log2(math.e)

LANES = 128
MLA_HEAD_PAD = 128
MLA_KV_BLOCK = 512
MLA_Q_TILE = 512
MLA_V_ROWS = B_VDIM + 16
MLA_VT_ROWS = B_HEADS * MLA_V_ROWS
BAND_TILE = 256
BAND_LEFT = LEFT_CHUNKS * CHUNK
BAND_WINDOW = BAND_LEFT + BAND_TILE
BAND_ROLL_WIDTH = BAND_WINDOW + BAND_TILE
VMEM_LIMIT = 56 * 1024 * 1024

SM_CQ = 0
SM_CKV = Q_RANK
SM_KR = Q_RANK + KV_RANK
SM_WIDTH = SM_KR + LANES
GATE_I_LANE = B_ROPE
GATE_F_LANE = B_ROPE + A_HEADS


def _const_spec(shape):
    zeros = (0,) * len(shape)
    return pl.BlockSpec(shape, lambda *_: zeros, pipeline_mode=pl.Buffered(1))


def _params(semantics):
    return pltpu.CompilerParams(dimension_semantics=semantics, vmem_limit_bytes=VMEM_LIMIT)


def _rms(x, g):
    return x * lax.rsqrt(jnp.mean(x * x, axis=-1, keepdims=True) + EPS) * g


def _dot(a, b):
    return jnp.dot(a, b, preferred_element_type=F32)


def _dot_nt(a, b):
    return lax.dot_general(a, b, (((1,), (1,)), ((), ())), preferred_element_type=F32)


def _dot_tn(a, b):
    return lax.dot_general(a, b, (((0,), (0,)), ((), ())), preferred_element_type=F32)


def _row_specs(parts, tm, p_tiles):
    width = parts[0].shape[1]
    if len(parts) == 1:
        return [pl.BlockSpec((tm, width), lambda i: (i, 0))]
    assert parts[1].shape[0] == tm
    return [pl.BlockSpec((tm, width), lambda i: (jnp.minimum(i, p_tiles - 1), 0)),
            pl.BlockSpec((tm, width), lambda i: (0, 0), pipeline_mode=pl.Buffered(1))]


def _row_load(refs, p_tiles):
    if len(refs) == 1:
        return refs[0][...]
    return jnp.where(pl.program_id(0) < p_tiles, refs[0][...], refs[1][...])


def _norm_proj_body(*refs, n_x, p_tiles, plan):
    x_refs = refs[:n_x]
    g_ref, w_ref = refs[n_x:n_x + 2]
    out_refs = refs[n_x + 2:]
    h = _rms(_row_load(x_refs, p_tiles), g_ref[...]).astype(BF16)
    for w0, width, dests in plan:
        z = _dot(h, w_ref[:, w0:w0 + width])
        for out_idx, o0, scale in dests:
            o_ref = out_refs[out_idx]
            zz = z if scale == 1.0 else z * scale
            o_ref[:, o0:o0 + width] = zz.astype(o_ref.dtype)


def norm_proj(x_parts, g, w, plan, out_widths, out_dtypes, tm):
    t = sum(p.shape[0] for p in x_parts)
    d = x_parts[0].shape[1]
    n = w.shape[1]
    p_tiles = x_parts[0].shape[0] // tm
    return pl.pallas_call(
        functools.partial(_norm_proj_body, n_x=len(x_parts), p_tiles=p_tiles, plan=plan),
        grid=(t // tm,),
        in_specs=_row_specs(x_parts, tm, p_tiles) + [_const_spec((1, d)), _const_spec((d, n))],
        out_specs=[pl.BlockSpec((tm, ow), lambda i: (i, 0)) for ow in out_widths],
        out_shape=[jax.ShapeDtypeStruct((t, ow), dt) for ow, dt in zip(out_widths, out_dtypes)],
        compiler_params=_params(("parallel",)),
        name="norm_proj",
    )(*x_parts, g, w)


def _mix_ffn_body(*refs, n_x, mix_counts, p_tiles, ff_chunks, final):
    x_refs = refs[:n_x]
    pos = n_x
    a_groups = []
    for count in mix_counts:
        a_groups.append(refs[pos:pos + count])
        pos += count
    wo_refs = refs[pos:pos + len(mix_counts)]
    pos += len(mix_counts)
    g_ref, wg_ref, wu_ref, wd_ref = refs[pos:pos + 4]
    pos += 4
    if final:
        gf_ref = refs[pos]
        pos += 1
    out_refs = refs[pos:-1]
    act_ref = refs[-1]
    x = _row_load(x_refs, p_tiles)
    for a_refs, wo_ref in zip(a_groups, wo_refs):
        x = x + _dot(_row_load(a_refs, p_tiles), wo_ref[...])
    h = _rms(x, g_ref[...]).astype(BF16)
    for c0, cw in ff_chunks:
        gate = _dot(h, wg_ref[:, c0:c0 + cw])
        up = _dot(h, wu_ref[:, c0:c0 + cw])
        act_ref[:, c0:c0 + cw] = (gate * jax.nn.sigmoid(gate) * up).astype(BF16)
    y = x + _dot(act_ref[...], wd_ref[...])
    if final:
        y = _rms(y, gf_ref[...])
    if len(out_refs) == 1:
        out_refs[0][...] = y
    else:
        i = pl.program_id(0)

        @pl.when(i < p_tiles)
        def _():
            out_refs[0][...] = y

        @pl.when(i >= p_tiles)
        def _():
            out_refs[1][...] = y


def mix_ffn(x_parts, mix_in, mix_w, g, wg, wu, wd, g_final, tm, p_rows, split_out):
    t = sum(p.shape[0] for p in x_parts)
    d = x_parts[0].shape[1]
    f = wg.shape[1]
    p_tiles = p_rows // tm
    chunk = 512
    ff_chunks = [(c0, min(chunk, f - c0)) for c0 in range(0, f, chunk)]
    final = g_final is not None
    in_specs = _row_specs(x_parts, tm, p_tiles)
    args = list(x_parts)
    for parts in mix_in:
        in_specs += _row_specs(parts, tm, p_tiles)
        args += list(parts)
    in_specs += [_const_spec(w.shape) for w in mix_w]
    in_specs += [_const_spec((1, d)), _const_spec((d, f)), _const_spec((d, f)), _const_spec((f, d))]
    args += [*mix_w, g, wg, wu, wd]
    if final:
        in_specs.append(_const_spec((1, d)))
        args.append(g_final)
    if split_out:
        assert t - p_rows == tm
        out_specs = [pl.BlockSpec((tm, d), lambda i: (jnp.minimum(i, p_tiles - 1), 0)),
                     pl.BlockSpec((tm, d), lambda i: (0, 0), pipeline_mode=pl.Buffered(1))]
        out_shape = [jax.ShapeDtypeStruct((p_rows, d), F32), jax.ShapeDtypeStruct((t - p_rows, d), F32)]
    else:
        out_specs = pl.BlockSpec((tm, d), lambda i: (i, 0))
        out_shape = jax.ShapeDtypeStruct((t, d), F32)
    return pl.pallas_call(
        functools.partial(_mix_ffn_body, n_x=len(x_parts), mix_counts=tuple(len(p) for p in mix_in),
                          p_tiles=p_tiles, ff_chunks=ff_chunks, final=final),
        grid=(t // tm,),
        in_specs=in_specs,
        out_specs=out_specs,
        out_shape=out_shape,
        scratch_shapes=[pltpu.VMEM((tm, f), BF16)],
        compiler_params=_params(("arbitrary",)),
        name="mix_ffn",
    )(*args)


def _mla_prep_q_body(sm_ref, qn_ref, kvn_ref, wq_ref, rc_ref, rs1_ref, rs2_ref, cos_t_ref, sin_t_ref,
                     ckv_ref, kr_ref, qt_ref):
    sm = sm_ref[...]
    cq = _rms(sm[:, SM_CQ:SM_CQ + Q_RANK], qn_ref[...]).astype(BF16)
    ckv_ref[...] = _rms(sm[:, SM_CKV:SM_CKV + KV_RANK], kvn_ref[...])
    grp = sm[:, SM_KR:SM_KR + LANES]
    half = B_ROPE // 2
    rot = (grp * rc_ref[...] + pltpu.roll(grp, half, 1) * rs1_ref[...]
           + pltpu.roll(grp, LANES - half, 1) * rs2_ref[...])
    kr_ref[...] = rot[:, :B_ROPE]
    qt = _dot_nt(wq_ref[...], cq) * (MLA_SCALE * LOG2E)
    cos_t = cos_t_ref[...]
    sin_t = sin_t_ref[...]
    for h in range(B_HEADS):
        r0 = h * MLA_HEAD_PAD
        x1 = qt[r0 + B_NOPE:r0 + B_NOPE + half]
        x2 = qt[r0 + B_NOPE + half:r0 + B_QK]
        qt_ref[0, r0:r0 + B_NOPE, :] = qt[r0:r0 + B_NOPE].astype(BF16)
        rot_q = jnp.concatenate([x1 * cos_t - x2 * sin_t, x1 * sin_t + x2 * cos_t], axis=0)
        qt_ref[0, r0 + B_NOPE:r0 + B_QK, :] = rot_q.astype(BF16)
        qt_ref[0, r0 + B_QK:r0 + MLA_HEAD_PAD, :] = qt[r0 + B_QK:r0 + MLA_HEAD_PAD].astype(BF16)


def mla_prep_q(small, qn, kvn, wq_t, rope_tabs, tab_index, tm):
    t = small.shape[0]
    rc, rs1, rs2, cos_t, sin_t = rope_tabs
    half = B_ROPE // 2
    row_tab = pl.BlockSpec((tm, LANES), lambda i: (tab_index(i), 0))
    col_tab = pl.BlockSpec((half, tm), lambda i: (0, tab_index(i)))
    return pl.pallas_call(
        _mla_prep_q_body,
        grid=(t // tm,),
        in_specs=[pl.BlockSpec((tm, SM_WIDTH), lambda i: (i, 0)), _const_spec(qn.shape), _const_spec(kvn.shape),
                  _const_spec(wq_t.shape), row_tab, row_tab, row_tab, col_tab, col_tab],
        out_specs=[pl.BlockSpec((tm, KV_RANK), lambda i: (i, 0)), pl.BlockSpec((tm, B_ROPE), lambda i: (i, 0)),
                   pl.BlockSpec((1, B_HEADS * MLA_HEAD_PAD, tm), lambda i: (i, 0, 0))],
        out_shape=[jax.ShapeDtypeStruct((t, KV_RANK), F32), jax.ShapeDtypeStruct((t, B_ROPE), F32),
                   jax.ShapeDtypeStruct((t // tm, B_HEADS * MLA_HEAD_PAD, tm), BF16)],
        compiler_params=_params(("parallel",)),
        name="mla_prep_q",
    )(small, qn, kvn, wq_t, rc, rs1, rs2, cos_t, sin_t)


def _mla_prep_kv_body(ckv_ref, kr_ref, wk_ref, place_ref, wv_ref, kp_ref, vt_ref):
    c = ckv_ref[...].astype(BF16)
    kr = kr_ref[...].astype(BF16)
    kp_ref[...] = (_dot(c, wk_ref[...]) + _dot(kr, place_ref[...])).astype(BF16)
    vt = _dot_nt(wv_ref[...], c).astype(BF16)
    ones = jnp.ones((MLA_V_ROWS - B_VDIM, vt.shape[1]), BF16)
    for h in range(B_HEADS):
        vt_ref[0, h * MLA_V_ROWS:h * MLA_V_ROWS + B_VDIM, :] = vt[h * B_VDIM:(h + 1) * B_VDIM]
        vt_ref[0, h * MLA_V_ROWS + B_VDIM:(h + 1) * MLA_V_ROWS, :] = ones


def mla_prep_kv(ckv, krope, wk_pad, place, wv_t, n_tiles, tm):
    kw = B_HEADS * MLA_HEAD_PAD
    return pl.pallas_call(
        _mla_prep_kv_body,
        grid=(n_tiles,),
        in_specs=[pl.BlockSpec((tm, KV_RANK), lambda i: (i, 0)), pl.BlockSpec((tm, B_ROPE), lambda i: (i, 0)),
                  _const_spec(wk_pad.shape), _const_spec(place.shape), _const_spec(wv_t.shape)],
        out_specs=[pl.BlockSpec((tm, kw), lambda i: (i, 0)), pl.BlockSpec((1, MLA_VT_ROWS, tm), lambda i: (i, 0, 0))],
        out_shape=[jax.ShapeDtypeStruct((n_tiles * tm, kw), BF16),
                   jax.ShapeDtypeStruct((n_tiles, MLA_VT_ROWS, tm), BF16)],
        compiler_params=_params(("parallel",)),
        name="mla_prep_kv",
    )(ckv, krope, wk_pad, place, wv_t)


def _mla_attn_body(qt_ref, kp_ref, vt_ref, o_ref, m_ref, l_ref, acc_ref, *, tq, tk, q0, n_diag):
    i = pl.program_id(1)
    start = q0 + i * tq
    n_full = start // tk
    m_ref[...] = jnp.full(m_ref.shape, NEG, F32)
    l_ref[...] = jnp.zeros(l_ref.shape, F32)
    acc_ref[...] = jnp.zeros(acc_ref.shape, F32)
    key_chunk = lax.broadcasted_iota(jnp.int32, (tk, tq), 0) // CHUNK
    qry_chunk = lax.broadcasted_iota(jnp.int32, (tk, tq), 1) // CHUNK

    def step(j, masked):
        row0 = pl.multiple_of(j * tk, tk)
        if masked:
            visible = key_chunk + (j * tk - start) // CHUNK <= qry_chunk

        def scores(h):
            k_h = kp_ref[pl.ds(row0, tk), h * MLA_HEAD_PAD:(h + 1) * MLA_HEAD_PAD]
            q_h = qt_ref[0, h * MLA_HEAD_PAD:(h + 1) * MLA_HEAD_PAD, :]
            return _dot(k_h, q_h)

        s_next = scores(0)
        for h in range(B_HEADS):
            s = s_next
            if h + 1 < B_HEADS:
                s_next = scores(h + 1)
            if masked:
                s = jnp.where(visible, s, NEG)
            m_prev = m_ref[h:h + 1, :]
            m_new = jnp.maximum(m_prev, jnp.max(s, axis=0, keepdims=True))
            alpha = jnp.exp2(m_prev - m_new)
            p = jnp.exp2(s - m_new).astype(BF16)
            m_ref[h:h + 1, :] = m_new
            pv = _dot(vt_ref[j, h * MLA_V_ROWS:(h + 1) * MLA_V_ROWS, :], p)
            l_ref[h:h + 1, :] = alpha * l_ref[h:h + 1, :] + pv[B_VDIM:B_VDIM + 1]
            rows = slice(h * B_VDIM, (h + 1) * B_VDIM)
            acc_ref[rows, :] = alpha * acc_ref[rows, :] + pv[:B_VDIM]

    def full_step(j, carry):
        step(j, False)
        return carry

    lax.fori_loop(0, n_full, full_step, 0)
    for d in range(n_diag):
        step(n_full + d, True)
    for h in range(B_HEADS):
        rows = slice(h * B_VDIM, (h + 1) * B_VDIM)
        acc_ref[rows, :] = acc_ref[rows, :] / l_ref[h:h + 1, :]
    o_ref[...] = jnp.transpose(acc_ref[...]).astype(o_ref.dtype)


def mla_attn(qt, kp, vt, groups, nq, tq, tk, n_blocks, q0, q_frames):
    kw = B_HEADS * MLA_HEAD_PAD
    assert q0 % tk == 0 and (tq % tk == 0 or (nq == 1 and q_frames <= tk))
    n_diag = pl.cdiv(q_frames, tk)
    return pl.pallas_call(
        functools.partial(_mla_attn_body, tq=tq, tk=tk, q0=q0, n_diag=n_diag),
        grid=(groups, nq),
        in_specs=[pl.BlockSpec((1, kw, tq), lambda g, i: (g * nq + i, 0, 0)),
                  pl.BlockSpec((n_blocks * tk, kw), lambda g, i: (g, 0), pipeline_mode=pl.Buffered(1)),
                  pl.BlockSpec((n_blocks, MLA_VT_ROWS, tk), lambda g, i: (g, 0, 0), pipeline_mode=pl.Buffered(1))],
        out_specs=pl.BlockSpec((tq, B_WIDTH), lambda g, i: (g * nq + i, 0)),
        out_shape=jax.ShapeDtypeStruct((groups * nq * tq, B_WIDTH), BF16),
        scratch_shapes=[pltpu.VMEM((B_HEADS, tq), F32), pltpu.VMEM((B_HEADS, tq), F32),
                        pltpu.VMEM((B_WIDTH, tq), F32)],
        compiler_params=_params(("parallel", "arbitrary")),
        name="mla_attn",
    )(qt, kp, vt)


def _mla_sample_body(sm_ref, qn_ref, wq_ref, rc_ref, rs1_ref, rs2_ref, wabs_ref, sel_ref, cckv_ref, ckr_ref,
                     nckv_ref, nkr_ref, wuv_ref, o_ref, m_ref, l_ref, acc_ref, *, n_chunks, chunk):
    frames = sm_ref.shape[0]
    half = B_ROPE // 2
    cq = _rms(sm_ref[:, SM_CQ:SM_CQ + Q_RANK], qn_ref[...]).astype(BF16)
    qf = _dot(cq, wq_ref[...]) * (MLA_SCALE * LOG2E)
    width = qf.shape[1]
    qb = (qf * rc_ref[...] + pltpu.roll(qf, half, 1) * rs1_ref[...]
          + pltpu.roll(qf, width - half, 1) * rs2_ref[...]).astype(BF16)
    q_lat, q_rope = [], []
    for h in range(B_HEADS):
        q_h = qb[:, h * MLA_HEAD_PAD:(h + 1) * MLA_HEAD_PAD]
        q_lat.append(_dot(q_h, wabs_ref[h]).astype(BF16))
        q_rope.append(_dot(q_h, sel_ref[...]).astype(BF16))
    q_lat = jnp.concatenate(q_lat, axis=0)
    q_rope = jnp.concatenate(q_rope, axis=0)
    m_ref[...] = jnp.full(m_ref.shape, NEG, F32)
    l_ref[...] = jnp.zeros(l_ref.shape, F32)
    acc_ref[...] = jnp.zeros(acc_ref.shape, F32)

    def attend(lat, rope, rope_feature_major):
        lat = lat.astype(BF16)
        rope = rope.astype(BF16)
        s_rope = _dot(q_rope, rope) if rope_feature_major else _dot_nt(q_rope, rope)
        s = _dot_nt(q_lat, lat) + s_rope
        m_prev = m_ref[...]
        m_new = jnp.maximum(m_prev, jnp.max(s, axis=1, keepdims=True))
        alpha = jnp.exp2(m_prev - m_new)
        p = jnp.exp2(s - m_new)
        l_ref[...] = alpha * l_ref[...] + jnp.sum(p, axis=1, keepdims=True)
        acc_ref[...] = alpha * acc_ref[...] + _dot(p.astype(BF16), lat)
        m_ref[...] = m_new

    for c in range(n_chunks):
        attend(cckv_ref[0, 0, c * chunk:(c + 1) * chunk, :], ckr_ref[0, 0, :, c * chunk:(c + 1) * chunk], True)
    attend(nckv_ref[...], nkr_ref[...], False)
    o_lat = (acc_ref[...] / l_ref[...]).astype(BF16)
    out = _dot(o_lat[:frames], wuv_ref[0])
    for h in range(1, B_HEADS):
        out = out + _dot(o_lat[h * frames:(h + 1) * frames], wuv_ref[h])
    o_ref[...] = out.astype(o_ref.dtype)


def mla_sample(small, ckv, krope, cache_ckv, cache_krope_t, layer, qn, wq, rope_full, wabs, sel, wuv_place,
               streams, frames, row0):
    past = cache_ckv.shape[2]
    chunk = 512
    base = row0 // frames
    rows_q = B_HEADS * frames
    rc, rs1, rs2 = rope_full
    return pl.pallas_call(
        functools.partial(_mla_sample_body, n_chunks=past // chunk, chunk=chunk),
        grid=(streams,),
        in_specs=[pl.BlockSpec((frames, SM_WIDTH), lambda g: (base + g, 0)), _const_spec(qn.shape),
                  _const_spec(wq.shape), _const_spec(rc.shape), _const_spec(rs1.shape), _const_spec(rs2.shape),
                  _const_spec(wabs.shape), _const_spec(sel.shape),
                  pl.BlockSpec((1, 1, past, KV_RANK), lambda g: (layer, g, 0, 0)),
                  pl.BlockSpec((1, 1, B_ROPE, past), lambda g: (layer, g, 0, 0)),
                  pl.BlockSpec((frames, KV_RANK), lambda g: (base + g, 0)),
                  pl.BlockSpec((frames, B_ROPE), lambda g: (base + g, 0)),
                  _const_spec(wuv_place.shape)],
        out_specs=pl.BlockSpec((frames, B_WIDTH), lambda g: (g, 0)),
        out_shape=jax.ShapeDtypeStruct((streams * frames, B_WIDTH), BF16),
        scratch_shapes=[pltpu.VMEM((rows_q, 1), F32), pltpu.VMEM((rows_q, 1), F32),
                        pltpu.VMEM((rows_q, KV_RANK), F32)],
        compiler_params=_params(("parallel",)),
        name="mla_sample",
    )(small, qn, wq, rc, rs1, rs2, wabs, sel, cache_ckv, cache_krope_t, ckv, krope, wuv_place)


def _scan_rows(x, op, fill, length):
    row = lax.broadcasted_iota(jnp.int32, x.shape, 0)
    shift = 1
    while shift < length:
        moved = pltpu.roll(x, shift, 0)
        x = op(x, jnp.where(row >= shift, moved, fill))
        shift *= 2
    return x


def _mlstm_body(*refs, blk, per_step):
    n_in = 5 * per_step
    bias_ref, hn_ref, c0_ref, m0_ref, ha_ref, c_out_ref, m_out_ref, c_scr, m_scr = refs[n_in:]
    dh = A_HEAD_DIM
    step_idx = pl.program_id(1)

    @pl.when(step_idx == 0)
    def _():
        c_scr[...] = c0_ref[...]
        m_scr[...] = m0_ref[...]

    causal = (lax.broadcasted_iota(jnp.int32, (blk, blk), 0) >= lax.broadcasted_iota(jnp.int32, (blk, blk), 1))
    ones_col = (lax.broadcasted_iota(jnp.int32, (blk, dh), 1) == 0).astype(BF16)
    pad = max(blk, LANES) - blk
    prep = []
    for u in range(per_step):
        gt_ref = refs[5 * u + 4]
        gates = gt_ref[...] + bias_ref[...]
        log_f = jax.nn.log_sigmoid(gates)
        b_all = pltpu.roll(_scan_rows(log_f, jnp.add, 0.0, blk), LANES - A_HEADS, 1)
        a_all = gates - b_all
        amax_all = _scan_rows(a_all, jnp.maximum, NEG, blk)
        a_sq = a_all if pad == 0 else jnp.concatenate([a_all, jnp.zeros((pad, LANES), F32)], axis=0)
        prep.append((a_all, b_all, amax_all, jnp.transpose(a_sq)))
    for h in range(A_HEADS):
        for u in range(per_step):
            q_ref, k_ref, v_ref, o_ref = refs[5 * u:5 * u + 4]
            a_all, b_all, amax_all, a_rows = prep[u]
            lane = GATE_I_LANE + h
            cols = slice(h * dh, (h + 1) * dh)
            q = q_ref[:, cols]
            k = k_ref[:, cols]
            v_ext = jnp.concatenate([v_ref[:, cols], ones_col], axis=1)
            a_col = a_all[:, lane:lane + 1]
            b_col = b_all[:, lane:lane + 1]
            a_row = a_rows[lane:lane + 1, :blk]
            m_prev = m_scr[u, h:h + 1, 0:1]
            run_max = jnp.maximum(amax_all[:, lane:lane + 1], m_prev)
            decay_mat = jnp.exp(jnp.where(causal, a_row - run_max, NEG))
            p = (_dot_nt(q, k) * decay_mat).astype(BF16)
            state = c_scr[u, h]
            w_inter = jnp.exp(m_prev - run_max)
            numden = w_inter * _dot(q, state.astype(BF16)) + _dot(p, v_ext)
            den = numden[:, dh:dh + 1]
            hh = numden[:, :dh] / jnp.maximum(jnp.abs(den), jnp.exp(-(b_col + run_max)))
            hh = _rms(hh, hn_ref[:, cols])
            ha_ref[u, :, cols] = (hh * jax.nn.sigmoid(o_ref[:, cols])).astype(ha_ref.dtype)
            max_last = run_max[blk - 1:blk, :]
            w_k = jnp.exp(a_col - max_last)
            k_w = (k.astype(F32) * w_k).astype(BF16)
            c_scr[u, h] = jnp.exp(m_prev - max_last) * state + _dot_tn(k_w, v_ext)
            m_scr[u, h:h + 1, :] = jnp.broadcast_to(b_col[blk - 1:blk, :] + max_last, (1, LANES))

    @pl.when(step_idx == pl.num_programs(1) - 1)
    def _():
        c_out_ref[...] = c_scr[...]
        m_out_ref[...] = m_scr[...]


def mlstm(qkv, o32, small, bias, hnorm, c0, m0, groups, steps, blk, row0, per_step):
    base = row0 // blk
    gate_block = SM_KR // LANES
    state_shape = (per_step, A_HEADS, A_HEAD_DIM, 2 * A_HEAD_DIM)

    def rows(u, col):
        return lambda g, s: (base + (g * per_step + u) * steps + s, col)

    in_specs, args = [], []
    for u in range(per_step):
        in_specs += [pl.BlockSpec((blk, A_WIDTH), rows(u, 0)), pl.BlockSpec((blk, A_WIDTH), rows(u, 1)),
                     pl.BlockSpec((blk, A_WIDTH), rows(u, 2)), pl.BlockSpec((blk, A_WIDTH), rows(u, 0)),
                     pl.BlockSpec((blk, LANES), rows(u, gate_block))]
        args += [qkv, qkv, qkv, o32, small]
    in_specs += [_const_spec((1, LANES)), _const_spec((1, A_WIDTH)),
                 pl.BlockSpec(state_shape, lambda g, s: (g, 0, 0, 0)),
                 pl.BlockSpec((per_step, 8, LANES), lambda g, s: (g, 0, 0))]
    ha, c_out, m_out = pl.pallas_call(
        functools.partial(_mlstm_body, blk=blk, per_step=per_step),
        grid=(groups // per_step, steps),
        in_specs=in_specs,
        out_specs=[pl.BlockSpec((per_step, blk, A_WIDTH), lambda g, s: (g, s, 0)),
                   pl.BlockSpec(state_shape, lambda g, s: (g, 0, 0, 0)),
                   pl.BlockSpec((per_step, 8, LANES), lambda g, s: (g, 0, 0))],
        out_shape=[jax.ShapeDtypeStruct((groups, steps * blk, A_WIDTH), BF16),
                   jax.ShapeDtypeStruct((groups, A_HEADS, A_HEAD_DIM, 2 * A_HEAD_DIM), F32),
                   jax.ShapeDtypeStruct((groups, 8, LANES), F32)],
        scratch_shapes=[pltpu.VMEM(state_shape, F32), pltpu.VMEM((per_step, 8, LANES), F32)],
        compiler_params=_params(("parallel", "arbitrary")),
        name="mlstm",
    )(*args, bias, hnorm, c0, m0)
    return ha.reshape(groups * steps * blk, A_WIDTH), c_out, m_out


def _band_proj_body(x_ref, g_ref, wk_ref, wq_t_ref, wv_t_ref, wkv_t_ref, k_ref, qt_ref, vt_ref, tail_ref,
                    *, tiles_per_seq, p_tiles):
    i = pl.program_id(0)
    h = _rms(x_ref[...], g_ref[...]).astype(BF16)
    step = 512
    for c0 in range(0, C_WIDTH, step):
        k_ref[:, c0:c0 + step] = _dot(h, wk_ref[:, c0:c0 + step]).astype(BF16)
        qt = _dot_nt(wq_t_ref[c0:c0 + step, :], h) * (C_SCALE * LOG2E)
        qt_ref[0, c0:c0 + step, :] = qt.astype(BF16)
        vt_ref[0, c0:c0 + step, :] = _dot_nt(wv_t_ref[c0:c0 + step, :], h).astype(BF16)

    is_tail = ((i + 1) % tiles_per_seq == 0) | (i >= p_tiles)

    @pl.when(is_tail)
    def _():
        for c0 in range(0, 2 * C_WIDTH, step):
            tail_ref[0, c0:c0 + step, :] = _dot_nt(wkv_t_ref[c0:c0 + step, :], h)

    @pl.when(jnp.logical_not(is_tail) & (i % tiles_per_seq == 0))
    def _():
        tail_ref[...] = jnp.zeros(tail_ref.shape, F32)


def band_proj(x, g, wk, wq_t, wv_t, wkv_t, tm, tiles_per_seq, p_tiles, n_seq):
    t, d = x.shape
    n_tiles = t // tm
    n_tail = n_seq + n_tiles - p_tiles
    cw = C_WIDTH

    def tail_index(i):
        return (jnp.where(i < p_tiles, i // tiles_per_seq, n_seq + i - p_tiles), 0, 0)

    return pl.pallas_call(
        functools.partial(_band_proj_body, tiles_per_seq=tiles_per_seq, p_tiles=p_tiles),
        grid=(n_tiles,),
        in_specs=[pl.BlockSpec((tm, d), lambda i: (i, 0)), _const_spec((1, d)), _const_spec((d, cw)),
                  _const_spec((cw, d)), _const_spec((cw, d)), _const_spec((2 * cw, d))],
        out_specs=[pl.BlockSpec((tm, cw), lambda i: (i, 0)), pl.BlockSpec((1, cw, tm), lambda i: (i, 0, 0)),
                   pl.BlockSpec((1, cw, tm), lambda i: (i, 0, 0)), pl.BlockSpec((1, 2 * cw, tm), tail_index)],
        out_shape=[jax.ShapeDtypeStruct((t, cw), BF16), jax.ShapeDtypeStruct((n_tiles, cw, tm), BF16),
                   jax.ShapeDtypeStruct((n_tiles, cw, tm), BF16), jax.ShapeDtypeStruct((n_tail, 2 * cw, tm), F32)],
        compiler_params=_params(("arbitrary",)),
        name="band_proj",
    )(x, g, wk, wq_t, wv_t, wkv_t)


def _band_table_body(base_ref, tab_ref):
    tq = BAND_TILE
    shape = (tq, BAND_ROLL_WIDTH)
    row = lax.broadcasted_iota(jnp.int32, shape, 0)
    tab = jnp.broadcast_to(base_ref[0], shape)
    shift = 1
    while shift < tq:
        tab = jnp.where((row & shift) != 0, pltpu.roll(tab, shift, 1), tab)
        shift *= 2
    win = lax.broadcasted_iota(jnp.int32, (tq, BAND_WINDOW), 1) // CHUNK
    qch = lax.broadcasted_iota(jnp.int32, (tq, BAND_WINDOW), 0) // CHUNK
    valid = (win >= qch) & (win <= qch + LEFT_CHUNKS)
    tab_ref[0] = jnp.transpose(jnp.where(valid, tab[:, :BAND_WINDOW] * LOG2E, NEG))


def band_table(base_rows):
    return pl.pallas_call(
        _band_table_body,
        grid=(C_HEADS,),
        in_specs=[pl.BlockSpec((1, 1, BAND_ROLL_WIDTH), lambda h: (h, 0, 0))],
        out_specs=pl.BlockSpec((1, BAND_WINDOW, BAND_TILE), lambda h: (h, 0, 0)),
        out_shape=jax.ShapeDtypeStruct((C_HEADS, BAND_WINDOW, BAND_TILE), F32),
        compiler_params=_params(("parallel",)),
        name="band_table",
    )(base_rows)


def _band_body(qt_ref, *refs, clamp_start):
    nb = BAND_WINDOW // BAND_TILE
    k_refs = refs[:nb]
    v_refs = refs[nb:2 * nb]
    tab_ref, o_ref, acc_ref = refs[2 * nb:]
    tq = BAND_TILE
    i = pl.program_id(1)
    row_half = lax.broadcasted_iota(jnp.int32, (LANES, tq), 0) // C_HEAD_DIM
    ones = jnp.ones((16, tq), BF16)

    def run(first_tiles):
        def scores(h):
            pair, half = divmod(h, 2)
            q_pair = qt_ref[0, pair * LANES:(pair + 1) * LANES, :]
            q_h = jnp.where(row_half == half, q_pair, jnp.zeros_like(q_pair))
            parts = []
            for j in range(nb):
                s_j = _dot(k_refs[j][:, pair * LANES:(pair + 1) * LANES], q_h)
                if first_tiles and j < nb - 1:
                    s_j = jnp.where(i + j < nb - 1, NEG, s_j)
                parts.append(s_j)
            return jnp.concatenate(parts, axis=0)

        s_next = scores(0)
        for h in range(C_HEADS):
            s = s_next + tab_ref[h]
            if h + 1 < C_HEADS:
                s_next = scores(h + 1)
            m = jnp.max(s, axis=0, keepdims=True)
            p = jnp.exp2(s - m).astype(BF16)
            rows = slice(h * C_HEAD_DIM, (h + 1) * C_HEAD_DIM)
            pv = None
            for j in range(nb):
                v_ext = jnp.concatenate([v_refs[j][0, rows, :], ones], axis=0)
                part = _dot(v_ext, p[j * BAND_TILE:(j + 1) * BAND_TILE])
                pv = part if pv is None else pv + part
            acc_ref[rows, :] = pv[:C_HEAD_DIM] / pv[C_HEAD_DIM:C_HEAD_DIM + 1]

    if clamp_start:
        @pl.when(i < nb - 1)
        def _():
            run(True)

        @pl.when(i >= nb - 1)
        def _():
            run(False)
    else:
        run(False)
    o_ref[...] = jnp.transpose(acc_ref[...]).astype(o_ref.dtype)


def band_attn(qt_arr, qt_map, k_arr, k_maps, vt_arr, vt_maps, tab, groups, tiles, clamp_start):
    tq = BAND_TILE
    in_specs = [pl.BlockSpec((1, C_WIDTH, tq), qt_map)]
    in_specs += [pl.BlockSpec((tq, C_WIDTH), m) for m in k_maps]
    in_specs += [pl.BlockSpec((1, C_WIDTH, tq), m) for m in vt_maps]
    in_specs.append(_const_spec(tab.shape))
    return pl.pallas_call(
        functools.partial(_band_body, clamp_start=clamp_start),
        grid=(groups, tiles),
        in_specs=in_specs,
        out_specs=pl.BlockSpec((tq, C_WIDTH), lambda g, i: (g * tiles + i, 0)),
        out_shape=jax.ShapeDtypeStruct((groups * tiles * tq, C_WIDTH), BF16),
        scratch_shapes=[pltpu.VMEM((C_WIDTH, tq), F32)],
        compiler_params=_params(("parallel", "arbitrary")),
        name="band_attn",
    )(qt_arr, *([k_arr] * len(k_maps)), *([vt_arr] * len(vt_maps)), tab)


def _cache_roll_body(c_ref, *refs, new_frames):
    tail_refs = refs[:-1]
    o_ref = refs[-1]
    layer = pl.program_id(0)
    stream = pl.program_id(1)
    buf = c_ref.shape[3]
    rolled = pltpu.roll(c_ref[0, 0], buf - new_frames, 1)
    new = tail_refs[0][0]
    for n in range(1, len(tail_refs)):
        new = jnp.where(layer == n, tail_refs[n][0], new)
    upper_half = (stream % (LANES // new_frames)) == 1
    new = jnp.where(upper_half, new, pltpu.roll(new, new_frames, 1))
    o_ref[0, 0] = rolled
    lane = lax.broadcasted_iota(jnp.int32, new.shape, 1)
    o_ref[0, 0, :, buf - LANES:] = jnp.where(lane < LANES - new_frames, rolled[:, buf - LANES:], new)


def cache_roll(cache_t, tails, row_block, first_tile, new_frames):
    layers, streams, feat, buf = cache_t.shape
    tm = tails[0].shape[2]
    per_tile = tm // new_frames
    groups = LANES // new_frames
    assert 2 * new_frames == LANES and layers == len(tails)

    def tail_map(l, s):
        return (first_tile + s // per_tile, row_block, (s % per_tile) // groups)

    return pl.pallas_call(
        functools.partial(_cache_roll_body, new_frames=new_frames),
        grid=(layers, streams),
        in_specs=[pl.BlockSpec((1, 1, feat, buf), lambda l, s: (l, s, 0, 0))]
                 + [pl.BlockSpec((1, feat, LANES), tail_map)] * layers,
        out_specs=pl.BlockSpec((1, 1, feat, buf), lambda l, s: (l, s, 0, 0)),
        out_shape=jax.ShapeDtypeStruct(cache_t.shape, cache_t.dtype),
        compiler_params=_params(("parallel", "parallel")),
        name="cache_roll",
    )(cache_t, *tails)


def _rope_tables(positions):
    half = B_ROPE // 2
    inv = ROPE_THETA ** (-jnp.arange(half, dtype=F32) / half)
    ang = positions.astype(F32)[:, None] * inv[None, :]
    cos, sin = jnp.cos(ang), jnp.sin(ang)
    n = positions.shape[0]
    zeros = jnp.zeros((n, LANES - B_ROPE), F32)
    zh = jnp.zeros((n, half), F32)
    rc = jnp.concatenate([cos, cos, zeros], axis=1)
    rs1 = jnp.concatenate([zh, sin, zeros], axis=1)
    rs2 = jnp.concatenate([-sin, zh, zeros], axis=1)
    return rc, rs1, rs2, cos.T, sin.T


def _band_base_rows(rel_bias):
    x = jnp.arange(BAND_ROLL_WIDTH)
    rel = jnp.where(x < BAND_WINDOW, BAND_LEFT - x, BAND_LEFT + 1)
    idx = jnp.clip(rel, -MAX_REL, MAX_REL) + MAX_REL
    return rel_bias[:, None, idx]


def kernel(x_prompt, x_sample, cache_mla_ckv, cache_mla_krope, state_mlstm_C, state_mlstm_n, state_mlstm_m,
           cache_band_k, cache_band_v, norm_mix, norm_ffn, norm_final, w_in_ab, b_gates, mlstm_hnorm,
           mla_q_norm, mla_kv_norm, mla_w_uq, mla_w_uk, mla_w_uv, w_out_ab, w_qkv_c, w_out_c, rel_bias_c,
           w_gate, w_up, w_down):
    batch, seq, d = x_prompt.shape
    dec_batch, dec_seq, _ = x_sample.shape
    past = cache_mla_ckv.shape[2]
    band_buf = cache_band_k.shape[2]
    depth = norm_mix.shape[0]
    tp = batch * seq
    ts = dec_batch * dec_seq
    t = tp + ts
    tm_big = 1024
    tm = MLA_Q_TILE
    tk = MLA_KV_BLOCK
    assert d == D_MODEL and dec_seq == CHUNK and band_buf == BAND_LEFT and past % tk == 0
    assert tp % tm_big == 0 and ts % tm_big == 0 and seq % tm == 0 and tm % dec_seq == 0 and tm % tk == 0

    assert past % CHUNK == 0 and ts == tm_big
    x_parts = (jnp.concatenate([x_prompt.reshape(tp, d), x_sample.reshape(ts, d)], axis=0),)

    pos_tab = jnp.concatenate([jnp.arange(seq, dtype=jnp.int32),
                               past + (jnp.arange(tm, dtype=jnp.int32) % dec_seq)])
    rope_tabs = _rope_tables(pos_tab)
    head_pat = jnp.concatenate([jnp.ones((dec_seq, B_NOPE), F32), jnp.zeros((dec_seq, LANES - B_NOPE), F32)], axis=1)
    roll_pad = ((0, 0), (B_NOPE, LANES - B_QK))
    rope_full = tuple(jnp.tile(tab, (1, B_HEADS)) for tab in (
        head_pat + jnp.pad(rope_tabs[0][seq:seq + dec_seq, :B_ROPE], roll_pad),
        jnp.pad(rope_tabs[1][seq:seq + dec_seq, :B_ROPE], roll_pad),
        jnp.pad(rope_tabs[2][seq:seq + dec_seq, :B_ROPE], roll_pad)))
    sel = jnp.pad(jnp.eye(B_ROPE, dtype=F32), ((B_NOPE, MLA_HEAD_PAD - B_QK), (0, 0))).astype(BF16)
    cache_krope_t = jnp.swapaxes(cache_mla_krope, 2, 3)
    band_tails = []
    p_tiles = tp // tm
    seq_tiles = seq // tm

    def tab_index(i):
        return jnp.where(i < p_tiles, i % seq_tiles, seq_tiles)

    outs = {k: [] for k in ("p_ckv", "p_kr", "p_C", "p_n", "p_m", "p_bk", "p_bv",
                            "s_ckv", "s_kr", "s_C", "s_n", "s_m", "s_bk", "s_bv")}
    a4 = 4 * A_WIDTH
    for layer in range(depth):
        j = layer // 2
        g_mix = norm_mix[layer][None, :]
        g_ffn = norm_ffn[layer][None, :]
        last = layer == depth - 1
        g_fin = norm_final[None, :] if last else None
        ffn_w = (w_gate[layer].astype(BF16), w_up[layer].astype(BF16), w_down[layer].astype(BF16))
        if layer % 2 == 0:
            w = w_in_ab[j]
            gate_cols = jnp.concatenate([w[:, a4 + 2 * A_HEADS + Q_RANK + KV_RANK:], w[:, a4:a4 + 2 * A_HEADS],
                                         jnp.zeros((d, LANES - B_ROPE - 2 * A_HEADS), F32)], axis=1)
            w_all = jnp.concatenate([w[:, :a4], w[:, a4 + 2 * A_HEADS:a4 + 2 * A_HEADS + Q_RANK + KV_RANK],
                                     gate_cols], axis=1).astype(BF16)
            aw = A_WIDTH
            plan = [(0, aw, [(0, 0, 1.0)]), (aw, aw, [(0, aw, A_HEAD_DIM ** -0.5)]), (2 * aw, aw, [(0, 2 * aw, 1.0)]),
                    (3 * aw, aw, [(1, 0, 1.0)]), (4 * aw, Q_RANK, [(2, SM_CQ, 1.0)]),
                    (4 * aw + Q_RANK, KV_RANK + LANES, [(2, SM_CKV, 1.0)])]
            qkv, o32, small = norm_proj(x_parts, g_mix, w_all, plan, (3 * aw, aw, SM_WIDTH), (BF16, F32, F32), tm_big)

            wq = mla_w_uq[j].reshape(Q_RANK, B_HEADS, B_QK)
            wq_t = jnp.pad(wq, ((0, 0), (0, 0), (0, MLA_HEAD_PAD - B_QK))).reshape(Q_RANK, -1).T.astype(BF16)
            ckv, krope, qt = mla_prep_q(small, mla_q_norm[j][None, :], mla_kv_norm[j][None, :], wq_t,
                                        rope_tabs, tab_index, tm)
            wk_pad = jnp.pad(mla_w_uk[j], ((0, 0), (0, 0), (0, MLA_HEAD_PAD - B_NOPE))).reshape(KV_RANK, -1).astype(BF16)
            place = jnp.pad(jnp.eye(B_ROPE, dtype=F32), ((0, 0), (B_NOPE, MLA_HEAD_PAD - B_QK)))
            place = jnp.tile(place, (1, B_HEADS)).astype(BF16)
            wv_t = mla_w_uv[j].reshape(KV_RANK, B_WIDTH).T.astype(BF16)
            kp_p, vt_p = mla_prep_kv(ckv, krope, wk_pad, place, wv_t, tp // tk, tk)
            ckv_s = ckv[tp:].reshape(dec_batch, dec_seq, KV_RANK)
            kr_s = krope[tp:].reshape(dec_batch, dec_seq, B_ROPE)

            hb_p = mla_attn(qt, kp_p, vt_p, batch, seq_tiles, tm, tk, seq // tk, 0, tm)
            wabs = jnp.pad(jnp.transpose(mla_w_uk[j], (1, 2, 0)),
                           ((0, 0), (0, MLA_HEAD_PAD - B_NOPE), (0, 0))).astype(BF16)
            wuv_place = jnp.einsum("rhv,hg->hrgv", mla_w_uv[j], jnp.eye(B_HEADS, dtype=F32))
            wuv_place = wuv_place.reshape(B_HEADS, KV_RANK, B_WIDTH).astype(BF16)
            hb_s = mla_sample(small, ckv, krope, cache_mla_ckv, cache_krope_t, j, mla_q_norm[j][None, :],
                              wq_t.T, rope_full, wabs, sel, wuv_place, dec_batch, dec_seq, tp)

            bias = jnp.zeros((1, LANES), F32).at[0, GATE_I_LANE:GATE_I_LANE + 2 * A_HEADS].set(b_gates[j])
            hn = mlstm_hnorm[j][None, :]
            c0_p = jnp.zeros((batch, A_HEADS, A_HEAD_DIM, 2 * A_HEAD_DIM), F32)
            m0_p = jnp.zeros((batch, 8, LANES), F32)
            blk_p = 256
            ha_p, c_p, m_p = mlstm(qkv, o32, small, bias, hn, c0_p, m0_p, batch, seq // blk_p, blk_p, 0, batch)
            c0_s = jnp.concatenate([state_mlstm_C[j], state_mlstm_n[j][..., None],
                                    jnp.zeros((dec_batch, A_HEADS, A_HEAD_DIM, A_HEAD_DIM - 1), F32)], axis=-1)
            m0_s = jnp.broadcast_to(jnp.pad(state_mlstm_m[j], ((0, 0), (0, 8 - A_HEADS)))[..., None],
                                    (dec_batch, 8, LANES))
            ha_s, c_s, m_s = mlstm(qkv, o32, small, bias, hn, c0_s, m0_s, dec_batch, 1, dec_seq, tp, 4)

            wo = w_out_ab[j].astype(BF16)
            x_parts = mix_ffn(x_parts, ((ha_p, ha_s), (hb_p, hb_s)), (wo[:A_WIDTH], wo[A_WIDTH:]), g_ffn, *ffn_w,
                              g_fin, tm_big, tp, False)
            x_parts = (x_parts,)

            outs["p_ckv"].append(ckv[:tp].reshape(batch, seq, KV_RANK))
            outs["p_kr"].append(krope[:tp].reshape(batch, seq, B_ROPE))
            outs["p_C"].append(c_p[..., :A_HEAD_DIM])
            outs["p_n"].append(c_p[..., A_HEAD_DIM])
            outs["p_m"].append(m_p[:, :A_HEADS, 0])
            outs["s_ckv"].append(ckv_s)
            outs["s_kr"].append(kr_s)
            outs["s_C"].append(c_s[..., :A_HEAD_DIM])
            outs["s_n"].append(c_s[..., A_HEAD_DIM])
            outs["s_m"].append(m_s[:, :A_HEADS, 0])
        else:
            cw = C_WIDTH
            wqkv = w_qkv_c[j].astype(BF16)
            tmb = 512
            tps = seq // tmb
            ptb = tp // tmb
            assert band_buf <= tmb and seq % tmb == 0 and ts % tmb == 0 and tmb % BAND_TILE == 0
            (x,) = x_parts
            k_bf, qt_all, vt_all, kv_tail = band_proj(x, g_mix, wqkv[:, cw:2 * cw], wqkv[:, :cw].T,
                                                       wqkv[:, 2 * cw:].T, wqkv[:, cw:].T, tmb, tps, ptb, batch)
            tab = band_table(_band_base_rows(rel_bias_c[j]))
            tq = BAND_TILE
            nb = BAND_WINDOW // tq
            sub = tmb // tq
            tiles = seq // tq

            def block(g, i, jj):
                return g * tiles + jnp.maximum(i + jj - (nb - 1), 0)

            o_p = band_attn(
                qt_all, lambda g, i: ((g * tiles + i) // sub, 0, (g * tiles + i) % sub),
                k_bf, [functools.partial(lambda g, i, jj: (block(g, i, jj), 0), jj=jj) for jj in range(nb)],
                vt_all, [functools.partial(lambda g, i, jj: (block(g, i, jj) // sub, 0, block(g, i, jj) % sub), jj=jj)
                         for jj in range(nb)],
                tab, batch, tiles, True)

            def per_stream_t(a):
                a = a.reshape(-1, cw, tmb // dec_seq, dec_seq)
                return jnp.moveaxis(a, 2, 1).reshape(dec_batch, cw, dec_seq)

            qt_s = jnp.pad(per_stream_t(qt_all[ptb:]), ((0, 0), (0, 0), (0, tq - dec_seq)))
            kw_s = jnp.concatenate([cache_band_k[j].reshape(dec_batch, band_buf, cw).astype(BF16),
                                    k_bf[tp:].reshape(dec_batch, dec_seq, cw),
                                    jnp.zeros((dec_batch, tq - dec_seq, cw), BF16)], axis=1).reshape(-1, cw)
            vt_s = jnp.concatenate([jnp.swapaxes(cache_band_v[j].reshape(dec_batch, band_buf, cw), 1, 2).astype(BF16),
                                    per_stream_t(vt_all[ptb:]),
                                    jnp.zeros((dec_batch, cw, tq - dec_seq), BF16)], axis=2)
            o_s = band_attn(
                qt_s, lambda g, i: (g, 0, 0),
                kw_s, [functools.partial(lambda g, i, jj: (g * nb + jj, 0), jj=jj) for jj in range(nb)],
                vt_s, [functools.partial(lambda g, i, jj: (g, 0, jj), jj=jj) for jj in range(nb)],
                tab, dec_batch, 1, False)
            o_s = o_s.reshape(dec_batch, tq, cw)[:, :dec_seq].reshape(ts, cw)
            x_parts = mix_ffn(x_parts, ((o_p, o_s),), (w_out_c[j].astype(BF16),), g_ffn, *ffn_w, g_fin, tm_big,
                              tp, False)
            x_parts = (x_parts,)

            kv_p = kv_tail[:batch, :, tmb - band_buf:].reshape(batch, 2, C_HEADS, C_HEAD_DIM, band_buf)
            outs["p_bk"].append(jnp.transpose(kv_p[:, 0], (0, 3, 1, 2)))
            outs["p_bv"].append(jnp.transpose(kv_p[:, 1], (0, 3, 1, 2)))
            band_tails.append(kv_tail)

    y_prompt = x_parts[0][:tp].reshape(batch, seq, d)
    y_sample = x_parts[0][tp:].reshape(dec_batch, dec_seq, d)

    def frames_last(c):
        return jnp.transpose(c, (0, 1, 3, 4, 2)).reshape(c.shape[0], dec_batch, C_WIDTH, band_buf)

    def frames_first(c):
        return jnp.transpose(c.reshape(-1, dec_batch, C_HEADS, C_HEAD_DIM, band_buf), (0, 1, 4, 2, 3))

    s_bk = frames_first(cache_roll(frames_last(cache_band_k), band_tails, 0, batch, dec_seq))
    s_bv = frames_first(cache_roll(frames_last(cache_band_v), band_tails, 1, batch, dec_seq))
    st = {k: jnp.stack(v) for k, v in outs.items() if v}
    return (y_prompt, y_sample, st["p_ckv"], st["p_kr"], st["p_C"], st["p_n"], st["p_m"], st["p_bk"], st["p_bv"],
            st["s_ckv"], st["s_kr"], st["s_C"], st["s_n"], st["s_m"], s_bk, s_bv)
```

```python
import functools
import math

import jax
import jax.numpy as jnp
from jax import lax
from jax.experimental import pallas as pl
from jax.experimental.pallas import tpu as pltpu

F32 = jnp.float32
BF16 = jnp.bfloat16

D_MODEL = 1024
CHUNK = 64
A_HEADS = 4
A_HEAD_DIM = 128
A_WIDTH = A_HEADS * A_HEAD_DIM
B_HEADS = 8
B_NOPE = 64
B_ROPE = 32
B_VDIM = 64
B_QK = B_NOPE + B_ROPE
B_WIDTH = B_HEADS * B_VDIM
Q_RANK = 384
KV_RANK = 256
ROPE_THETA = 10000.0
MLA_SCALE = B_QK ** -0.5
C_HEADS = 16
C_HEAD_DIM = 64
C_WIDTH = C_HEADS * C_HEAD_DIM
LEFT_CHUNKS = 8
MAX_REL = 128
C_SCALE = C_HEAD_DIM ** -0.5
EPS = 1e-6
NEG = -1e30
LOG2E = math.log2(math.e)

LANES = 128
MLA_HEAD_PAD = 128
MLA_KV_BLOCK = 512
MLA_Q_TILE = 512
MLA_MAX_OVERSHOOT = 64.0
MLA_V_ROWS = B_VDIM + 16
MLA_VT_ROWS = B_HEADS * MLA_V_ROWS
BAND_TILE = 256
BAND_LEFT = LEFT_CHUNKS * CHUNK
BAND_WINDOW = BAND_LEFT + BAND_TILE
BAND_ROLL_WIDTH = BAND_WINDOW + BAND_TILE
VMEM_LIMIT = 56 * 1024 * 1024

SM_CQ = 0
SM_CKV = Q_RANK
SM_KR = Q_RANK + KV_RANK
SM_WIDTH = SM_KR + LANES
GATE_I_LANE = B_ROPE
GATE_F_LANE = B_ROPE + A_HEADS


def _const_spec(shape):
    zeros = (0,) * len(shape)
    return pl.BlockSpec(shape, lambda *_: zeros, pipeline_mode=pl.Buffered(1))


def _params(semantics, flags=None):
    return pltpu.CompilerParams(dimension_semantics=semantics, vmem_limit_bytes=VMEM_LIMIT, flags=flags)


def _rms(x, g):
    return x * lax.rsqrt(jnp.mean(x * x, axis=-1, keepdims=True) + EPS) * g


def _dot(a, b):
    return jnp.dot(a, b, preferred_element_type=F32)


def _dot_nt(a, b):
    return lax.dot_general(a, b, (((1,), (1,)), ((), ())), preferred_element_type=F32)


def _dot_tn(a, b):
    return lax.dot_general(a, b, (((0,), (0,)), ((), ())), preferred_element_type=F32)


def _row_specs(parts, tm, p_tiles):
    width = parts[0].shape[1]
    if len(parts) == 1:
        return [pl.BlockSpec((tm, width), lambda i: (i, 0))]
    assert parts[1].shape[0] == tm
    return [pl.BlockSpec((tm, width), lambda i: (jnp.minimum(i, p_tiles - 1), 0)),
            pl.BlockSpec((tm, width), lambda i: (0, 0), pipeline_mode=pl.Buffered(1))]


def _row_load(refs, p_tiles):
    if len(refs) == 1:
        return refs[0][...]
    return jnp.where(pl.program_id(0) < p_tiles, refs[0][...], refs[1][...])


def _norm_proj_body(*refs, n_x, p_tiles, plan):
    x_refs = refs[:n_x]
    g_ref, w_ref = refs[n_x:n_x + 2]
    out_refs = refs[n_x + 2:]
    h = _rms(_row_load(x_refs, p_tiles), g_ref[...]).astype(BF16)
    for w0, width, dests in plan:
        z = _dot(h, w_ref[:, w0:w0 + width])
        for out_idx, o0, scale in dests:
            o_ref = out_refs[out_idx]
            zz = z if scale == 1.0 else z * scale
            o_ref[:, o0:o0 + width] = zz.astype(o_ref.dtype)


def norm_proj(x_parts, g, w, plan, out_widths, out_dtypes, tm):
    t = sum(p.shape[0] for p in x_parts)
    d = x_parts[0].shape[1]
    n = w.shape[1]
    p_tiles = x_parts[0].shape[0] // tm
    return pl.pallas_call(
        functools.partial(_norm_proj_body, n_x=len(x_parts), p_tiles=p_tiles, plan=plan),
        grid=(t // tm,),
        in_specs=_row_specs(x_parts, tm, p_tiles) + [_const_spec((1, d)), _const_spec((d, n))],
        out_specs=[pl.BlockSpec((tm, ow), lambda i: (i, 0)) for ow in out_widths],
        out_shape=[jax.ShapeDtypeStruct((t, ow), dt) for ow, dt in zip(out_widths, out_dtypes)],
        compiler_params=_params(("parallel",)),
        name="norm_proj",
    )(*x_parts, g, w)


def _mix_ffn_body(*refs, n_x, mix_counts, p_tiles, ff_chunks, final):
    x_refs = refs[:n_x]
    pos = n_x
    a_groups = []
    for count in mix_counts:
        a_groups.append(refs[pos:pos + count])
        pos += count
    wo_refs = refs[pos:pos + len(mix_counts)]
    pos += len(mix_counts)
    g_ref, wg_ref, wu_ref, wd_ref = refs[pos:pos + 4]
    pos += 4
    if final:
        gf_ref = refs[pos]
        pos += 1
    out_refs = refs[pos:-1]
    act_ref = refs[-1]
    x = _row_load(x_refs, p_tiles)
    for a_refs, wo_ref in zip(a_groups, wo_refs):
        x = x + _dot(_row_load(a_refs, p_tiles), wo_ref[...])
    h = _rms(x, g_ref[...]).astype(BF16)
    for c0, cw in ff_chunks:
        gate = _dot(h, wg_ref[:, c0:c0 + cw])
        up = _dot(h, wu_ref[:, c0:c0 + cw])
        act_ref[:, c0:c0 + cw] = (gate * jax.nn.sigmoid(gate) * up).astype(BF16)
    y = x + _dot(act_ref[...], wd_ref[...])
    if final:
        y = _rms(y, gf_ref[...])
    if len(out_refs) == 1:
        out_refs[0][...] = y
    else:
        i = pl.program_id(0)

        @pl.when(i < p_tiles)
        def _():
            out_refs[0][...] = y

        @pl.when(i >= p_tiles)
        def _():
            out_refs[1][...] = y


def mix_ffn(x_parts, mix_in, mix_w, g, wg, wu, wd, g_final, tm, p_rows, split_out):
    t = sum(p.shape[0] for p in x_parts)
    d = x_parts[0].shape[1]
    f = wg.shape[1]
    p_tiles = p_rows // tm
    chunk = 512
    ff_chunks = [(c0, min(chunk, f - c0)) for c0 in range(0, f, chunk)]
    final = g_final is not None
    in_specs = _row_specs(x_parts, tm, p_tiles)
    args = list(x_parts)
    for parts in mix_in:
        in_specs += _row_specs(parts, tm, p_tiles)
        args += list(parts)
    in_specs += [_const_spec(w.shape) for w in mix_w]
    in_specs += [_const_spec((1, d)), _const_spec((d, f)), _const_spec((d, f)), _const_spec((f, d))]
    args += [*mix_w, g, wg, wu, wd]
    if final:
        in_specs.append(_const_spec((1, d)))
        args.append(g_final)
    if split_out:
        assert t - p_rows == tm
        out_specs = [pl.BlockSpec((tm, d), lambda i: (jnp.minimum(i, p_tiles - 1), 0)),
                     pl.BlockSpec((tm, d), lambda i: (0, 0), pipeline_mode=pl.Buffered(1))]
        out_shape = [jax.ShapeDtypeStruct((p_rows, d), F32), jax.ShapeDtypeStruct((t - p_rows, d), F32)]
    else:
        out_specs = pl.BlockSpec((tm, d), lambda i: (i, 0))
        out_shape = jax.ShapeDtypeStruct((t, d), F32)
    return pl.pallas_call(
        functools.partial(_mix_ffn_body, n_x=len(x_parts), mix_counts=tuple(len(p) for p in mix_in),
                          p_tiles=p_tiles, ff_chunks=ff_chunks, final=final),
        grid=(t // tm,),
        in_specs=in_specs,
        out_specs=out_specs,
        out_shape=out_shape,
        scratch_shapes=[pltpu.VMEM((tm, f), BF16)],
        compiler_params=_params(("arbitrary",)),
        name="mix_ffn",
    )(*args)


def _mla_prep_q_body(sm_ref, qn_ref, kvn_ref, wq_ref, rc_ref, rs1_ref, rs2_ref, cos_t_ref, sin_t_ref,
                     ckv_ref, kr_ref, qt_ref):
    sm = sm_ref[...]
    cq = _rms(sm[:, SM_CQ:SM_CQ + Q_RANK], qn_ref[...]).astype(BF16)
    ckv_ref[...] = _rms(sm[:, SM_CKV:SM_CKV + KV_RANK], kvn_ref[...])
    grp = sm[:, SM_KR:SM_KR + LANES]
    half = B_ROPE // 2
    rot = (grp * rc_ref[...] + pltpu.roll(grp, half, 1) * rs1_ref[...]
           + pltpu.roll(grp, LANES - half, 1) * rs2_ref[...])
    kr_ref[...] = rot[:, :B_ROPE]
    qt = _dot_nt(wq_ref[...], cq) * (MLA_SCALE * LOG2E)
    cos_t = cos_t_ref[...]
    sin_t = sin_t_ref[...]
    for h in range(B_HEADS):
        r0 = h * MLA_HEAD_PAD
        x1 = qt[r0 + B_NOPE:r0 + B_NOPE + half]
        x2 = qt[r0 + B_NOPE + half:r0 + B_QK]
        qt_ref[0, r0:r0 + B_NOPE, :] = qt[r0:r0 + B_NOPE].astype(BF16)
        rot_q = jnp.concatenate([x1 * cos_t - x2 * sin_t, x1 * sin_t + x2 * cos_t], axis=0)
        qt_ref[0, r0 + B_NOPE:r0 + B_QK, :] = rot_q.astype(BF16)
        qt_ref[0, r0 + B_QK:r0 + MLA_HEAD_PAD, :] = qt[r0 + B_QK:r0 + MLA_HEAD_PAD].astype(BF16)


def mla_prep_q(small, qn, kvn, wq_t, rope_tabs, tab_index, tm):
    t = small.shape[0]
    rc, rs1, rs2, cos_t, sin_t = rope_tabs
    half = B_ROPE // 2
    row_tab = pl.BlockSpec((tm, LANES), lambda i: (tab_index(i), 0))
    col_tab = pl.BlockSpec((half, tm), lambda i: (0, tab_index(i)))
    return pl.pallas_call(
        _mla_prep_q_body,
        grid=(t // tm,),
        in_specs=[pl.BlockSpec((tm, SM_WIDTH), lambda i: (i, 0)), _const_spec(qn.shape), _const_spec(kvn.shape),
                  _const_spec(wq_t.shape), row_tab, row_tab, row_tab, col_tab, col_tab],
        out_specs=[pl.BlockSpec((tm, KV_RANK), lambda i: (i, 0)), pl.BlockSpec((tm, B_ROPE), lambda i: (i, 0)),
                   pl.BlockSpec((1, B_HEADS * MLA_HEAD_PAD, tm), lambda i: (i, 0, 0))],
        out_shape=[jax.ShapeDtypeStruct((t, KV_RANK), F32), jax.ShapeDtypeStruct((t, B_ROPE), F32),
                   jax.ShapeDtypeStruct((t // tm, B_HEADS * MLA_HEAD_PAD, tm), BF16)],
        compiler_params=_params(("parallel",)),
        name="mla_prep_q",
    )(small, qn, kvn, wq_t, rc, rs1, rs2, cos_t, sin_t)


def _mla_prep_kv_body(ckv_ref, kr_ref, wk_ref, place_ref, wv_ref, kp_ref, vt_ref):
    c = ckv_ref[...].astype(BF16)
    kr = kr_ref[...].astype(BF16)
    kp_ref[...] = (_dot(c, wk_ref[...]) + _dot(kr, place_ref[...])).astype(BF16)
    vt = _dot_nt(wv_ref[...], c).astype(BF16)
    ones = jnp.ones((MLA_V_ROWS - B_VDIM, vt.shape[1]), BF16)
    for h in range(B_HEADS):
        vt_ref[0, h * MLA_V_ROWS:h * MLA_V_ROWS + B_VDIM, :] = vt[h * B_VDIM:(h + 1) * B_VDIM]
        vt_ref[0, h * MLA_V_ROWS + B_VDIM:(h + 1) * MLA_V_ROWS, :] = ones


def mla_prep_kv(ckv, krope, wk_pad, place, wv_t, n_tiles, tm):
    kw = B_HEADS * MLA_HEAD_PAD
    return pl.pallas_call(
        _mla_prep_kv_body,
        grid=(n_tiles,),
        in_specs=[pl.BlockSpec((tm, KV_RANK), lambda i: (i, 0)), pl.BlockSpec((tm, B_ROPE), lambda i: (i, 0)),
                  _const_spec(wk_pad.shape), _const_spec(place.shape), _const_spec(wv_t.shape)],
        out_specs=[pl.BlockSpec((tm, kw), lambda i: (i, 0)), pl.BlockSpec((1, MLA_VT_ROWS, tm), lambda i: (i, 0, 0))],
        out_shape=[jax.ShapeDtypeStruct((n_tiles * tm, kw), BF16),
                   jax.ShapeDtypeStruct((n_tiles, MLA_VT_ROWS, tm), BF16)],
        compiler_params=_params(("parallel",)),
        name="mla_prep_kv",
    )(ckv, krope, wk_pad, place, wv_t)


def _mla_attn_body(qt_ref, kp_ref, vt_ref, o_ref, m_ref, l_ref, acc_ref, gap_ref, *, tq, tk, q0, n_diag):
    i = pl.program_id(1)
    start = q0 + i * tq
    n_full = start // tk
    key_chunk = lax.broadcasted_iota(jnp.int32, (tk, tq), 0) // CHUNK
    qry_chunk = lax.broadcasted_iota(jnp.int32, (tk, tq), 1) // CHUNK

    def step(j, masked, stale_max):
        row0 = pl.multiple_of(j * tk, tk)
        if masked:
            visible = key_chunk + (j * tk - start) // CHUNK <= qry_chunk

        def scores(h):
            k_h = kp_ref[pl.ds(row0, tk), h * MLA_HEAD_PAD:(h + 1) * MLA_HEAD_PAD]
            q_h = qt_ref[0, h * MLA_HEAD_PAD:(h + 1) * MLA_HEAD_PAD, :]
            return _dot(k_h, q_h)

        s_next = scores(0)
        for h in range(B_HEADS):
            s = s_next
            if h + 1 < B_HEADS:
                s_next = scores(h + 1)
            if masked:
                s = jnp.where(visible, s, NEG)
            rows = slice(h * B_VDIM, (h + 1) * B_VDIM)
            m_prev = m_ref[h:h + 1, :]
            blk_max = jnp.max(s, axis=0, keepdims=True)
            m_new = jnp.maximum(m_prev, blk_max)
            alpha = jnp.exp2(m_prev - m_new)
            p = jnp.exp2(s - (m_prev if stale_max else m_new)).astype(BF16)
            pv = _dot(vt_ref[j, h * MLA_V_ROWS:(h + 1) * MLA_V_ROWS, :], p)
            if stale_max:
                gap_ref[h:h + 1, :] = jnp.maximum(gap_ref[h:h + 1, :], blk_max - m_prev)
                l_ref[h:h + 1, :] = alpha * (l_ref[h:h + 1, :] + pv[B_VDIM:B_VDIM + 1])
                acc_ref[rows, :] = alpha * (acc_ref[rows, :] + pv[:B_VDIM])
            else:
                l_ref[h:h + 1, :] = alpha * l_ref[h:h + 1, :] + pv[B_VDIM:B_VDIM + 1]
                acc_ref[rows, :] = alpha * acc_ref[rows, :] + pv[:B_VDIM]
            m_ref[h:h + 1, :] = m_new

    def sweep(stale_max):
        m_ref[...] = jnp.full(m_ref.shape, NEG, F32)
        l_ref[...] = jnp.zeros(l_ref.shape, F32)
        acc_ref[...] = jnp.zeros(acc_ref.shape, F32)
        first = 0
        if stale_max:
            @pl.when(n_full > 0)
            def _():
                step(0, False, False)

            first = 1

        def full_step(j, carry):
            step(j, False, stale_max)
            return carry

        lax.fori_loop(first, n_full, full_step, 0)
        for d in range(n_diag):
            if stale_max and d == 0:
                @pl.when(n_full > 0)
                def _():
                    step(n_full, True, True)

                @pl.when(n_full == 0)
                def _():
                    step(n_full, True, False)
            else:
                step(n_full + d, True, stale_max)

    gap_ref[...] = jnp.zeros(gap_ref.shape, F32)
    sweep(True)

    @pl.when(jnp.max(gap_ref[...]) > MLA_MAX_OVERSHOOT)
    def _():
        sweep(False)

    for h in range(B_HEADS):
        rows = slice(h * B_VDIM, (h + 1) * B_VDIM)
        acc_ref[rows, :] = acc_ref[rows, :] / l_ref[h:h + 1, :]
    o_ref[...] = jnp.transpose(acc_ref[...]).astype(o_ref.dtype)


def mla_attn(qt, kp, vt, groups, nq, tq, tk, n_blocks, q0, q_frames):
    kw = B_HEADS * MLA_HEAD_PAD
    assert q0 % tk == 0 and (tq % tk == 0 or (nq == 1 and q_frames <= tk))
    n_diag = pl.cdiv(q_frames, tk)
    return pl.pallas_call(
        functools.partial(_mla_attn_body, tq=tq, tk=tk, q0=q0, n_diag=n_diag),
        grid=(groups, nq),
        in_specs=[pl.BlockSpec((1, kw, tq), lambda g, i: (g * nq + i, 0, 0)),
                  pl.BlockSpec((n_blocks * tk, kw), lambda g, i: (g, 0), pipeline_mode=pl.Buffered(1)),
                  pl.BlockSpec((n_blocks, MLA_VT_ROWS, tk), lambda g, i: (g, 0, 0), pipeline_mode=pl.Buffered(1))],
        out_specs=pl.BlockSpec((tq, B_WIDTH), lambda g, i: (g * nq + i, 0)),
        out_shape=jax.ShapeDtypeStruct((groups * nq * tq, B_WIDTH), BF16),
        scratch_shapes=[pltpu.VMEM((B_HEADS, tq), F32), pltpu.VMEM((B_HEADS, tq), F32),
                        pltpu.VMEM((B_WIDTH, tq), F32), pltpu.VMEM((B_HEADS, tq), F32)],
        compiler_params=_params(("parallel", "arbitrary")),
        name="mla_attn",
    )(qt, kp, vt)


def _mla_sample_body(sm_ref, qn_ref, wq_ref, rc_ref, rs1_ref, rs2_ref, wabs_ref, sel_ref, cckv_ref, ckr_ref,
                     nckv_ref, nkr_ref, wuv_ref, o_ref, m_ref, l_ref, acc_ref, *, n_chunks, chunk):
    frames = sm_ref.shape[0]
    half = B_ROPE // 2
    cq = _rms(sm_ref[:, SM_CQ:SM_CQ + Q_RANK], qn_ref[...]).astype(BF16)
    qf = _dot(cq, wq_ref[...]) * (MLA_SCALE * LOG2E)
    width = qf.shape[1]
    qb = (qf * rc_ref[...] + pltpu.roll(qf, half, 1) * rs1_ref[...]
          + pltpu.roll(qf, width - half, 1) * rs2_ref[...]).astype(BF16)
    q_lat, q_rope = [], []
    for h in range(B_HEADS):
        q_h = qb[:, h * MLA_HEAD_PAD:(h + 1) * MLA_HEAD_PAD]
        q_lat.append(_dot(q_h, wabs_ref[h]).astype(BF16))
        q_rope.append(_dot(q_h, sel_ref[...]).astype(BF16))
    q_lat = jnp.concatenate(q_lat, axis=0)
    q_rope = jnp.concatenate(q_rope, axis=0)
    m_ref[...] = jnp.full(m_ref.shape, NEG, F32)
    l_ref[...] = jnp.zeros(l_ref.shape, F32)
    acc_ref[...] = jnp.zeros(acc_ref.shape, F32)

    def attend(lat, rope, rope_feature_major):
        lat = lat.astype(BF16)
        rope = rope.astype(BF16)
        s_rope = _dot(q_rope, rope) if rope_feature_major else _dot_nt(q_rope, rope)
        s = _dot_nt(q_lat, lat) + s_rope
        m_prev = m_ref[...]
        m_new = jnp.maximum(m_prev, jnp.max(s, axis=1, keepdims=True))
        alpha = jnp.exp2(m_prev - m_new)
        p = jnp.exp2(s - m_new)
        l_ref[...] = alpha * l_ref[...] + jnp.sum(p, axis=1, keepdims=True)
        acc_ref[...] = alpha * acc_ref[...] + _dot(p.astype(BF16), lat)
        m_ref[...] = m_new

    for c in range(n_chunks):
        attend(cckv_ref[0, 0, c * chunk:(c + 1) * chunk, :], ckr_ref[0, 0, :, c * chunk:(c + 1) * chunk], True)
    attend(nckv_ref[...], nkr_ref[...], False)
    o_lat = (acc_ref[...] / l_ref[...]).astype(BF16)
    out = _dot(o_lat[:frames], wuv_ref[0])
    for h in range(1, B_HEADS):
        out = out + _dot(o_lat[h * frames:(h + 1) * frames], wuv_ref[h])
    o_ref[...] = out.astype(o_ref.dtype)


def mla_sample(small, ckv, krope, cache_ckv, cache_krope_t, layer, qn, wq, rope_full, wabs, sel, wuv_place,
               streams, frames, row0):
    past = cache_ckv.shape[2]
    chunk = 512
    base = row0 // frames
    rows_q = B_HEADS * frames
    rc, rs1, rs2 = rope_full
    return pl.pallas_call(
        functools.partial(_mla_sample_body, n_chunks=past // chunk, chunk=chunk),
        grid=(streams,),
        in_specs=[pl.BlockSpec((frames, SM_WIDTH), lambda g: (base + g, 0)), _const_spec(qn.shape),
                  _const_spec(wq.shape), _const_spec(rc.shape), _const_spec(rs1.shape), _const_spec(rs2.shape),
                  _const_spec(wabs.shape), _const_spec(sel.shape),
                  pl.BlockSpec((1, 1, past, KV_RANK), lambda g: (layer, g, 0, 0)),
                  pl.BlockSpec((1, 1, B_ROPE, past), lambda g: (layer, g, 0, 0)),
                  pl.BlockSpec((frames, KV_RANK), lambda g: (base + g, 0)),
                  pl.BlockSpec((frames, B_ROPE), lambda g: (base + g, 0)),
                  _const_spec(wuv_place.shape)],
        out_specs=pl.BlockSpec((frames, B_WIDTH), lambda g: (g, 0)),
        out_shape=jax.ShapeDtypeStruct((streams * frames, B_WIDTH), BF16),
        scratch_shapes=[pltpu.VMEM((rows_q, 1), F32), pltpu.VMEM((rows_q, 1), F32),
                        pltpu.VMEM((rows_q, KV_RANK), F32)],
        compiler_params=_params(("parallel",)),
        name="mla_sample",
    )(small, qn, wq, rc, rs1, rs2, wabs, sel, cache_ckv, cache_krope_t, ckv, krope, wuv_place)


def _scan_rows(x, op, fill, length):
    row = lax.broadcasted_iota(jnp.int32, x.shape, 0)
    shift = 1
    while shift < length:
        moved = pltpu.roll(x, shift, 0)
        x = op(x, jnp.where(row >= shift, moved, fill))
        shift *= 2
    return x


def _mlstm_body(*refs, blk, per_step):
    n_in = 5 * per_step
    bias_ref, hn_ref, c0_ref, m0_ref, ha_ref, c_out_ref, m_out_ref, c_scr, m_scr = refs[n_in:]
    dh = A_HEAD_DIM
    step_idx = pl.program_id(1)

    @pl.when(step_idx == 0)
    def _():
        c_scr[...] = c0_ref[...]
        m_scr[...] = m0_ref[...]

    causal = (lax.broadcasted_iota(jnp.int32, (blk, blk), 0) >= lax.broadcasted_iota(jnp.int32, (blk, blk), 1))
    ones_col = (lax.broadcasted_iota(jnp.int32, (blk, dh), 1) == 0).astype(BF16)
    pad = max(blk, LANES) - blk
    prep = []
    for u in range(per_step):
        gt_ref = refs[5 * u + 4]
        gates = gt_ref[...] + bias_ref[...]
        log_f = jax.nn.log_sigmoid(gates)
        b_all = pltpu.roll(_scan_rows(log_f, jnp.add, 0.0, blk), LANES - A_HEADS, 1)
        a_all = gates - b_all
        amax_all = _scan_rows(a_all, jnp.maximum, NEG, blk)
        a_sq = a_all if pad == 0 else jnp.concatenate([a_all, jnp.zeros((pad, LANES), F32)], axis=0)
        prep.append((a_all, b_all, amax_all, jnp.transpose(a_sq)))
    chains = [(u, h) for u in range(per_step) for h in range(A_HEADS)]

    def cols(h):
        return slice(h * dh, (h + 1) * dh)

    scores, v_exts, run_maxes, m_prevs = [], [], [], []
    for u, h in chains:
        q_ref, k_ref, v_ref = refs[5 * u:5 * u + 3]
        scores.append(_dot_nt(q_ref[:, cols(h)], k_ref[:, cols(h)]))
        v_exts.append(jnp.concatenate([v_ref[:, cols(h)], ones_col], axis=1))
        m_prev = m_scr[u, h:h + 1, 0:1]
        lane = GATE_I_LANE + h
        m_prevs.append(m_prev)
        run_maxes.append(jnp.maximum(prep[u][2][:, lane:lane + 1], m_prev))
    probs = []
    for n, (u, h) in enumerate(chains):
        lane = GATE_I_LANE + h
        a_row = prep[u][3][lane:lane + 1, :blk]
        decay_mat = jnp.exp(jnp.where(causal, a_row - run_maxes[n], NEG))
        probs.append((scores[n] * decay_mat).astype(BF16))
    numdens = []
    for n, (u, h) in enumerate(chains):
        q_ref = refs[5 * u]
        state = c_scr[u, h]
        w_inter = jnp.exp(m_prevs[n] - run_maxes[n])
        numdens.append(w_inter * _dot(q_ref[:, cols(h)], state.astype(BF16)) + _dot(probs[n], v_exts[n]))
    for n, (u, h) in enumerate(chains):
        o_ref = refs[5 * u + 3]
        lane = GATE_I_LANE + h
        b_col = prep[u][1][:, lane:lane + 1]
        numden = numdens[n]
        den = numden[:, dh:dh + 1]
        hh = numden[:, :dh] / jnp.maximum(jnp.abs(den), jnp.exp(-(b_col + run_maxes[n])))
        hh = _rms(hh, hn_ref[:, cols(h)])
        ha_ref[u, :, cols(h)] = (hh * jax.nn.sigmoid(o_ref[:, cols(h)])).astype(ha_ref.dtype)
    for n, (u, h) in enumerate(chains):
        k_ref = refs[5 * u + 1]
        lane = GATE_I_LANE + h
        a_col = prep[u][0][:, lane:lane + 1]
        b_col = prep[u][1][:, lane:lane + 1]
        max_last = run_maxes[n][blk - 1:blk, :]
        k_w = (k_ref[:, cols(h)].astype(F32) * jnp.exp(a_col - max_last)).astype(BF16)
        c_scr[u, h] = jnp.exp(m_prevs[n] - max_last) * c_scr[u, h] + _dot_tn(k_w, v_exts[n])
        m_scr[u, h:h + 1, :] = jnp.broadcast_to(b_col[blk - 1:blk, :] + max_last, (1, LANES))

    @pl.when(step_idx == pl.num_programs(1) - 1)
    def _():
        c_out_ref[...] = c_scr[...]
        m_out_ref[...] = m_scr[...]


def mlstm(qkv, o32, small, bias, hnorm, c0, m0, groups, steps, blk, row0, per_step):
    base = row0 // blk
    gate_block = SM_KR // LANES
    state_shape = (per_step, A_HEADS, A_HEAD_DIM, 2 * A_HEAD_DIM)

    def rows(u, col):
        return lambda g, s: (base + (g * per_step + u) * steps + s, col)

    in_specs, args = [], []
    for u in range(per_step):
        in_specs += [pl.BlockSpec((blk, A_WIDTH), rows(u, 0)), pl.BlockSpec((blk, A_WIDTH), rows(u, 1)),
                     pl.BlockSpec((blk, A_WIDTH), rows(u, 2)), pl.BlockSpec((blk, A_WIDTH), rows(u, 0)),
                     pl.BlockSpec((blk, LANES), rows(u, gate_block))]
        args += [qkv, qkv, qkv, o32, small]
    in_specs += [_const_spec((1, LANES)), _const_spec((1, A_WIDTH)),
                 pl.BlockSpec(state_shape, lambda g, s: (g, 0, 0, 0)),
                 pl.BlockSpec((per_step, 8, LANES), lambda g, s: (g, 0, 0))]
    ha, c_out, m_out = pl.pallas_call(
        functools.partial(_mlstm_body, blk=blk, per_step=per_step),
        grid=(groups // per_step, steps),
        in_specs=in_specs,
        out_specs=[pl.BlockSpec((per_step, blk, A_WIDTH), lambda g, s: (g, s, 0)),
                   pl.BlockSpec(state_shape, lambda g, s: (g, 0, 0, 0)),
                   pl.BlockSpec((per_step, 8, LANES), lambda g, s: (g, 0, 0))],
        out_shape=[jax.ShapeDtypeStruct((groups, steps * blk, A_WIDTH), BF16),
                   jax.ShapeDtypeStruct((groups, A_HEADS, A_HEAD_DIM, 2 * A_HEAD_DIM), F32),
                   jax.ShapeDtypeStruct((groups, 8, LANES), F32)],
        scratch_shapes=[pltpu.VMEM(state_shape, F32), pltpu.VMEM((per_step, 8, LANES), F32)],
        compiler_params=_params(("parallel", "arbitrary")),
        name="mlstm",
    )(*args, bias, hnorm, c0, m0)
    return ha.reshape(groups * steps * blk, A_WIDTH), c_out, m_out


def _band_proj_body(x_ref, g_ref, wk_ref, wq_t_ref, wv_t_ref, wkv_t_ref, k_ref, qt_ref, vt_ref, tail_ref,
                    *, tiles_per_seq, p_tiles):
    i = pl.program_id(0)
    h = _rms(x_ref[...], g_ref[...]).astype(BF16)
    step = 512
    for c0 in range(0, C_WIDTH, step):
        k_ref[:, c0:c0 + step] = _dot(h, wk_ref[:, c0:c0 + step]).astype(BF16)
        qt = _dot_nt(wq_t_ref[c0:c0 + step, :], h) * (C_SCALE * LOG2E)
        qt_ref[0, c0:c0 + step, :] = qt.astype(BF16)
        vt_ref[0, c0:c0 + step, :] = _dot_nt(wv_t_ref[c0:c0 + step, :], h).astype(BF16)

    is_tail = ((i + 1) % tiles_per_seq == 0) | (i >= p_tiles)

    @pl.when(is_tail)
    def _():
        for c0 in range(0, 2 * C_WIDTH, step):
            tail_ref[0, c0:c0 + step, :] = _dot_nt(wkv_t_ref[c0:c0 + step, :], h)

    @pl.when(jnp.logical_not(is_tail) & (i % tiles_per_seq == 0))
    def _():
        tail_ref[...] = jnp.zeros(tail_ref.shape, F32)


def band_proj(x, g, wk, wq_t, wv_t, wkv_t, tm, tiles_per_seq, p_tiles, n_seq):
    t, d = x.shape
    n_tiles = t // tm
    n_tail = n_seq + n_tiles - p_tiles
    cw = C_WIDTH

    def tail_index(i):
        return (jnp.where(i < p_tiles, i // tiles_per_seq, n_seq + i - p_tiles), 0, 0)

    return pl.pallas_call(
        functools.partial(_band_proj_body, tiles_per_seq=tiles_per_seq, p_tiles=p_tiles),
        grid=(n_tiles,),
        in_specs=[pl.BlockSpec((tm, d), lambda i: (i, 0)), _const_spec((1, d)), _const_spec((d, cw)),
                  _const_spec((cw, d)), _const_spec((cw, d)), _const_spec((2 * cw, d))],
        out_specs=[pl.BlockSpec((tm, cw), lambda i: (i, 0)), pl.BlockSpec((1, cw, tm), lambda i: (i, 0, 0)),
                   pl.BlockSpec((1, cw, tm), lambda i: (i, 0, 0)), pl.BlockSpec((1, 2 * cw, tm), tail_index)],
        out_shape=[jax.ShapeDtypeStruct((t, cw), BF16), jax.ShapeDtypeStruct((n_tiles, cw, tm), BF16),
                   jax.ShapeDtypeStruct((n_tiles, cw, tm), BF16), jax.ShapeDtypeStruct((n_tail, 2 * cw, tm), F32)],
        compiler_params=_params(("arbitrary",)),
        name="band_proj",
    )(x, g, wk, wq_t, wv_t, wkv_t)


def _band_table_body(base_ref, tab_ref):
    tq = BAND_TILE
    shape = (tq, BAND_ROLL_WIDTH)
    row = lax.broadcasted_iota(jnp.int32, shape, 0)
    tab = jnp.broadcast_to(base_ref[0], shape)
    shift = 1
    while shift < tq:
        tab = jnp.where((row & shift) != 0, pltpu.roll(tab, shift, 1), tab)
        shift *= 2
    win = lax.broadcasted_iota(jnp.int32, (tq, BAND_WINDOW), 1) // CHUNK
    qch = lax.broadcasted_iota(jnp.int32, (tq, BAND_WINDOW), 0) // CHUNK
    valid = (win >= qch) & (win <= qch + LEFT_CHUNKS)
    tab_ref[0] = jnp.transpose(jnp.where(valid, tab[:, :BAND_WINDOW] * LOG2E, NEG))


def band_table(base_rows):
    return pl.pallas_call(
        _band_table_body,
        grid=(C_HEADS,),
        in_specs=[pl.BlockSpec((1, 1, BAND_ROLL_WIDTH), lambda h: (h, 0, 0))],
        out_specs=pl.BlockSpec((1, BAND_WINDOW, BAND_TILE), lambda h: (h, 0, 0)),
        out_shape=jax.ShapeDtypeStruct((C_HEADS, BAND_WINDOW, BAND_TILE), F32),
        compiler_params=_params(("parallel",)),
        name="band_table",
    )(base_rows)


def _band_body(qt_ref, *refs, clamp_start):
    nb = BAND_WINDOW // BAND_TILE
    k_refs = refs[:nb]
    v_refs = refs[nb:2 * nb]
    tab_ref, o_ref, acc_ref = refs[2 * nb:]
    tq = BAND_TILE
    i = pl.program_id(1)
    row_half = lax.broadcasted_iota(jnp.int32, (LANES, tq), 0) // C_HEAD_DIM
    ones = jnp.ones((16, tq), BF16)

    def run(first_tiles):
        def scores(h):
            pair, half = divmod(h, 2)
            q_pair = qt_ref[0, pair * LANES:(pair + 1) * LANES, :]
            q_h = jnp.where(row_half == half, q_pair, jnp.zeros_like(q_pair))
            parts = []
            for j in range(nb):
                s_j = _dot(k_refs[j][:, pair * LANES:(pair + 1) * LANES], q_h)
                if first_tiles and j < nb - 1:
                    s_j = jnp.where(i + j < nb - 1, NEG, s_j)
                parts.append(s_j)
            return jnp.concatenate(parts, axis=0)

        s_next = scores(0)
        for h in range(C_HEADS):
            s = s_next + tab_ref[h]
            if h + 1 < C_HEADS:
                s_next = scores(h + 1)
            m = jnp.max(s, axis=0, keepdims=True)
            p = jnp.exp2(s - m).astype(BF16)
            rows = slice(h * C_HEAD_DIM, (h + 1) * C_HEAD_DIM)
            pv = None
            for j in range(nb):
                v_ext = jnp.concatenate([v_refs[j][0, rows, :], ones], axis=0)
                part = _dot(v_ext, p[j * BAND_TILE:(j + 1) * BAND_TILE])
                pv = part if pv is None else pv + part
            acc_ref[rows, :] = pv[:C_HEAD_DIM] / pv[C_HEAD_DIM:C_HEAD_DIM + 1]

    if clamp_start:
        @pl.when(i < nb - 1)
        def _():
            run(True)

        @pl.when(i >= nb - 1)
        def _():
            run(False)
    else:
        run(False)
    o_ref[...] = jnp.transpose(acc_ref[...]).astype(o_ref.dtype)


def band_attn(qt_arr, qt_map, k_arr, k_maps, vt_arr, vt_maps, tab, groups, tiles, clamp_start):
    tq = BAND_TILE
    in_specs = [pl.BlockSpec((1, C_WIDTH, tq), qt_map)]
    in_specs += [pl.BlockSpec((tq, C_WIDTH), m) for m in k_maps]
    in_specs += [pl.BlockSpec((1, C_WIDTH, tq), m) for m in vt_maps]
    in_specs.append(_const_spec(tab.shape))
    return pl.pallas_call(
        functools.partial(_band_body, clamp_start=clamp_start),
        grid=(groups, tiles),
        in_specs=in_specs,
        out_specs=pl.BlockSpec((tq, C_WIDTH), lambda g, i: (g * tiles + i, 0)),
        out_shape=jax.ShapeDtypeStruct((groups * tiles * tq, C_WIDTH), BF16),
        scratch_shapes=[pltpu.VMEM((C_WIDTH, tq), F32)],
        compiler_params=_params(("parallel", "arbitrary")),
        name="band_attn",
    )(qt_arr, *([k_arr] * len(k_maps)), *([vt_arr] * len(vt_maps)), tab)


def _cache_roll_body(c_ref, *refs, new_frames):
    tail_refs = refs[:-1]
    o_ref = refs[-1]
    layer = pl.program_id(0)
    stream = pl.program_id(1)
    buf = c_ref.shape[3]
    rolled = pltpu.roll(c_ref[0, 0], buf - new_frames, 1)
    new = tail_refs[0][0]
    for n in range(1, len(tail_refs)):
        new = jnp.where(layer == n, tail_refs[n][0], new)
    upper_half = (stream % (LANES // new_frames)) == 1
    new = jnp.where(upper_half, new, pltpu.roll(new, new_frames, 1))
    o_ref[0, 0] = rolled
    lane = lax.broadcasted_iota(jnp.int32, new.shape, 1)
    o_ref[0, 0, :, buf - LANES:] = jnp.where(lane < LANES - new_frames, rolled[:, buf - LANES:], new)


def cache_roll(cache_t, tails, row_block, first_tile, new_frames):
    layers, streams, feat, buf = cache_t.shape
    tm = tails[0].shape[2]
    per_tile = tm // new_frames
    groups = LANES // new_frames
    assert 2 * new_frames == LANES and layers == len(tails)

    def tail_map(l, s):
        return (first_tile + s // per_tile, row_block, (s % per_tile) // groups)

    return pl.pallas_call(
        functools.partial(_cache_roll_body, new_frames=new_frames),
        grid=(layers, streams),
        in_specs=[pl.BlockSpec((1, 1, feat, buf), lambda l, s: (l, s, 0, 0))]
                 + [pl.BlockSpec((1, feat, LANES), tail_map)] * layers,
        out_specs=pl.BlockSpec((1, 1, feat, buf), lambda l, s: (l, s, 0, 0)),
        out_shape=jax.ShapeDtypeStruct(cache_t.shape, cache_t.dtype),
        compiler_params=_params(("parallel", "parallel")),
        name="cache_roll",
    )(cache_t, *tails)


def _rope_tables(positions):
    half = B_ROPE // 2
    inv = ROPE_THETA ** (-jnp.arange(half, dtype=F32) / half)
    ang = positions.astype(F32)[:, None] * inv[None, :]
    cos, sin = jnp.cos(ang), jnp.sin(ang)
    n = positions.shape[0]
    zeros = jnp.zeros((n, LANES - B_ROPE), F32)
    zh = jnp.zeros((n, half), F32)
    rc = jnp.concatenate([cos, cos, zeros], axis=1)
    rs1 = jnp.concatenate([zh, sin, zeros], axis=1)
    rs2 = jnp.concatenate([-sin, zh, zeros], axis=1)
    return rc, rs1, rs2, cos.T, sin.T


def _band_base_rows(rel_bias):
    x = jnp.arange(BAND_ROLL_WIDTH)
    rel = jnp.where(x < BAND_WINDOW, BAND_LEFT - x, BAND_LEFT + 1)
    idx = jnp.clip(rel, -MAX_REL, MAX_REL) + MAX_REL
    return rel_bias[:, None, idx]


def kernel(x_prompt, x_sample, cache_mla_ckv, cache_mla_krope, state_mlstm_C, state_mlstm_n, state_mlstm_m,
           cache_band_k, cache_band_v, norm_mix, norm_ffn, norm_final, w_in_ab, b_gates, mlstm_hnorm,
           mla_q_norm, mla_kv_norm, mla_w_uq, mla_w_uk, mla_w_uv, w_out_ab, w_qkv_c, w_out_c, rel_bias_c,
           w_gate, w_up, w_down):
    batch, seq, d = x_prompt.shape
    dec_batch, dec_seq, _ = x_sample.shape
    past = cache_mla_ckv.shape[2]
    band_buf = cache_band_k.shape[2]
    depth = norm_mix.shape[0]
    tp = batch * seq
    ts = dec_batch * dec_seq
    t = tp + ts
    tm_big = 1024
    tm = MLA_Q_TILE
    tk = MLA_KV_BLOCK
    assert d == D_MODEL and dec_seq == CHUNK and band_buf == BAND_LEFT and past % tk == 0
    assert tp % tm_big == 0 and ts % tm_big == 0 and seq % tm == 0 and tm % dec_seq == 0 and tm % tk == 0

    assert past % CHUNK == 0 and ts == tm_big
    x_parts = (jnp.concatenate([x_prompt.reshape(tp, d), x_sample.reshape(ts, d)], axis=0),)

    pos_tab = jnp.concatenate([jnp.arange(seq, dtype=jnp.int32),
                               past + (jnp.arange(tm, dtype=jnp.int32) % dec_seq)])
    rope_tabs = _rope_tables(pos_tab)
    head_pat = jnp.concatenate([jnp.ones((dec_seq, B_NOPE), F32), jnp.zeros((dec_seq, LANES - B_NOPE), F32)], axis=1)
    roll_pad = ((0, 0), (B_NOPE, LANES - B_QK))
    rope_full = tuple(jnp.tile(tab, (1, B_HEADS)) for tab in (
        head_pat + jnp.pad(rope_tabs[0][seq:seq + dec_seq, :B_ROPE], roll_pad),
        jnp.pad(rope_tabs[1][seq:seq + dec_seq, :B_ROPE], roll_pad),
        jnp.pad(rope_tabs[2][seq:seq + dec_seq, :B_ROPE], roll_pad)))
    sel = jnp.pad(jnp.eye(B_ROPE, dtype=F32), ((B_NOPE, MLA_HEAD_PAD - B_QK), (0, 0))).astype(BF16)
    cache_krope_t = jnp.swapaxes(cache_mla_krope, 2, 3)
    band_tails = []
    p_tiles = tp // tm
    seq_tiles = seq // tm

    def tab_index(i):
        return jnp.where(i < p_tiles, i % seq_tiles, seq_tiles)

    outs = {k: [] for k in ("p_ckv", "p_kr", "p_C", "p_n", "p_m", "p_bk", "p_bv",
                            "s_ckv", "s_kr", "s_C", "s_n", "s_m", "s_bk", "s_bv")}
    a4 = 4 * A_WIDTH
    for layer in range(depth):
        j = layer // 2
        g_mix = norm_mix[layer][None, :]
        g_ffn = norm_ffn[layer][None, :]
        last = layer == depth - 1
        g_fin = norm_final[None, :] if last else None
        ffn_w = (w_gate[layer].astype(BF16), w_up[layer].astype(BF16), w_down[layer].astype(BF16))
        if layer % 2 == 0:
            w = w_in_ab[j]
            gate_cols = jnp.concatenate([w[:, a4 + 2 * A_HEADS + Q_RANK + KV_RANK:], w[:, a4:a4 + 2 * A_HEADS],
                                         jnp.zeros((d, LANES - B_ROPE - 2 * A_HEADS), F32)], axis=1)
            w_all = jnp.concatenate([w[:, :a4], w[:, a4 + 2 * A_HEADS:a4 + 2 * A_HEADS + Q_RANK + KV_RANK],
                                     gate_cols], axis=1).astype(BF16)
            aw = A_WIDTH
            plan = [(0, aw, [(0, 0, 1.0)]), (aw, aw, [(0, aw, A_HEAD_DIM ** -0.5)]), (2 * aw, aw, [(0, 2 * aw, 1.0)]),
                    (3 * aw, aw, [(1, 0, 1.0)]), (4 * aw, Q_RANK, [(2, SM_CQ, 1.0)]),
                    (4 * aw + Q_RANK, KV_RANK + LANES, [(2, SM_CKV, 1.0)])]
            qkv, o32, small = norm_proj(x_parts, g_mix, w_all, plan, (3 * aw, aw, SM_WIDTH), (BF16, F32, F32), tm_big)

            wq = mla_w_uq[j].reshape(Q_RANK, B_HEADS, B_QK)
            wq_t = jnp.pad(wq, ((0, 0), (0, 0), (0, MLA_HEAD_PAD - B_QK))).reshape(Q_RANK, -1).T.astype(BF16)
            ckv, krope, qt = mla_prep_q(small, mla_q_norm[j][None, :], mla_kv_norm[j][None, :], wq_t,
                                        rope_tabs, tab_index, tm)
            wk_pad = jnp.pad(mla_w_uk[j], ((0, 0), (0, 0), (0, MLA_HEAD_PAD - B_NOPE))).reshape(KV_RANK, -1).astype(BF16)
            place = jnp.pad(jnp.eye(B_ROPE, dtype=F32), ((0, 0), (B_NOPE, MLA_HEAD_PAD - B_QK)))
            place = jnp.tile(place, (1, B_HEADS)).astype(BF16)
            wv_t = mla_w_uv[j].reshape(KV_RANK, B_WIDTH).T.astype(BF16)
            kp_p, vt_p = mla_prep_kv(ckv, krope, wk_pad, place, wv_t, tp // tk, tk)
            ckv_s = ckv[tp:].reshape(dec_batch, dec_seq, KV_RANK)
            kr_s = krope[tp:].reshape(dec_batch, dec_seq, B_ROPE)

            hb_p = mla_attn(qt, kp_p, vt_p, batch, seq_tiles, tm, tk, seq // tk, 0, tm)
            wabs = jnp.pad(jnp.transpose(mla_w_uk[j], (1, 2, 0)),
                           ((0, 0), (0, MLA_HEAD_PAD - B_NOPE), (0, 0))).astype(BF16)
            wuv_place = jnp.einsum("rhv,hg->hrgv", mla_w_uv[j], jnp.eye(B_HEADS, dtype=F32))
            wuv_place = wuv_place.reshape(B_HEADS, KV_RANK, B_WIDTH).astype(BF16)
            hb_s = mla_sample(small, ckv, krope, cache_mla_ckv, cache_krope_t, j, mla_q_norm[j][None, :],
                              wq_t.T, rope_full, wabs, sel, wuv_place, dec_batch, dec_seq, tp)

            bias = jnp.zeros((1, LANES), F32).at[0, GATE_I_LANE:GATE_I_LANE + 2 * A_HEADS].set(b_gates[j])
            hn = mlstm_hnorm[j][None, :]
            c0_p = jnp.zeros((batch, A_HEADS, A_HEAD_DIM, 2 * A_HEAD_DIM), F32)
            m0_p = jnp.zeros((batch, 8, LANES), F32)
            blk_p = 256
            ha_p, c_p, m_p = mlstm(qkv, o32, small, bias, hn, c0_p, m0_p, batch, seq // blk_p, blk_p, 0, batch)
            c0_s = jnp.concatenate([state_mlstm_C[j], state_mlstm_n[j][..., None],
                                    jnp.zeros((dec_batch, A_HEADS, A_HEAD_DIM, A_HEAD_DIM - 1), F32)], axis=-1)
            m0_s = jnp.broadcast_to(jnp.pad(state_mlstm_m[j], ((0, 0), (0, 8 - A_HEADS)))[..., None],
                                    (dec_batch, 8, LANES))
            ha_s, c_s, m_s = mlstm(qkv, o32, small, bias, hn, c0_s, m0_s, dec_batch, 1, dec_seq, tp, 2)

            wo = w_out_ab[j].astype(BF16)
            x_parts = mix_ffn(x_parts, ((ha_p, ha_s), (hb_p, hb_s)), (wo[:A_WIDTH], wo[A_WIDTH:]), g_ffn, *ffn_w,
                              g_fin, tm_big, tp, False)
            x_parts = (x_parts,)

            outs["p_ckv"].append(ckv[:tp].reshape(batch, seq, KV_RANK))
            outs["p_kr"].append(krope[:tp].reshape(batch, seq, B_ROPE))
            outs["p_C"].append(c_p[..., :A_HEAD_DIM])
            outs["p_n"].append(c_p[..., A_HEAD_DIM])
            outs["p_m"].append(m_p[:, :A_HEADS, 0])
            outs["s_ckv"].append(ckv_s)
            outs["s_kr"].append(kr_s)
            outs["s_C"].append(c_s[..., :A_HEAD_DIM])
            outs["s_n"].append(c_s[..., A_HEAD_DIM])
            outs["s_m"].append(m_s[:, :A_HEADS, 0])
        else:
            cw = C_WIDTH
            wqkv = w_qkv_c[j].astype(BF16)
            tmb = 512
            tps = seq // tmb
            ptb = tp // tmb
            assert band_buf <= tmb and seq % tmb == 0 and ts % tmb == 0 and tmb % BAND_TILE == 0
            (x,) = x_parts
            k_bf, qt_all, vt_all, kv_tail = band_proj(x, g_mix, wqkv[:, cw:2 * cw], wqkv[:, :cw].T,
                                                       wqkv[:, 2 * cw:].T, wqkv[:, cw:].T, tmb, tps, ptb, batch)
            tab = band_table(_band_base_rows(rel_bias_c[j]))
            tq = BAND_TILE
            nb = BAND_WINDOW // tq
            sub = tmb // tq
            tiles = seq // tq

            def block(g, i, jj):
                return g * tiles + jnp.maximum(i + jj - (nb - 1), 0)

            o_p = band_attn(
                qt_all, lambda g, i: ((g * tiles + i) // sub, 0, (g * tiles + i) % sub),
                k_bf, [functools.partial(lambda g, i, jj: (block(g, i, jj), 0), jj=jj) for jj in range(nb)],
                vt_all, [functools.partial(lambda g, i, jj: (block(g, i, jj) // sub, 0, block(g, i, jj) % sub), jj=jj)
                         for jj in range(nb)],
                tab, batch, tiles, True)

            def per_stream_t(a):
                a = a.reshape(-1, cw, tmb // dec_seq, dec_seq)
                return jnp.moveaxis(a, 2, 1).reshape(dec_batch, cw, dec_seq)

            qt_s = jnp.pad(per_stream_t(qt_all[ptb:]), ((0, 0), (0, 0), (0, tq - dec_seq)))
            kw_s = jnp.concatenate([cache_band_k[j].reshape(dec_batch, band_buf, cw).astype(BF16),
                                    k_bf[tp:].reshape(dec_batch, dec_seq, cw),
                                    jnp.zeros((dec_batch, tq - dec_seq, cw), BF16)], axis=1).reshape(-1, cw)
            vt_s = jnp.concatenate([jnp.swapaxes(cache_band_v[j].reshape(dec_batch, band_buf, cw), 1, 2).astype(BF16),
                                    per_stream_t(vt_all[ptb:]),
                                    jnp.zeros((dec_batch, cw, tq - dec_seq), BF16)], axis=2)
            o_s = band_attn(
                qt_s, lambda g, i: (g, 0, 0),
                kw_s, [functools.partial(lambda g, i, jj: (g * nb + jj, 0), jj=jj) for jj in range(nb)],
                vt_s, [functools.partial(lambda g, i, jj: (g, 0, jj), jj=jj) for jj in range(nb)],
                tab, dec_batch, 1, False)
            o_s = o_s.reshape(dec_batch, tq, cw)[:, :dec_seq].reshape(ts, cw)
            x_parts = mix_ffn(x_parts, ((o_p, o_s),), (w_out_c[j].astype(BF16),), g_ffn, *ffn_w, g_fin, tm_big,
                              tp, False)
            x_parts = (x_parts,)

            kv_p = kv_tail[:batch, :, tmb - band_buf:].reshape(batch, 2, C_HEADS, C_HEAD_DIM, band_buf)
            outs["p_bk"].append(jnp.transpose(kv_p[:, 0], (0, 3, 1, 2)))
            outs["p_bv"].append(jnp.transpose(kv_p[:, 1], (0, 3, 1, 2)))
            band_tails.append(kv_tail)

    y_prompt = x_parts[0][:tp].reshape(batch, seq, d)
    y_sample = x_parts[0][tp:].reshape(dec_batch, dec_seq, d)

    def frames_last(c):
        return jnp.transpose(c, (0, 1, 3, 4, 2)).reshape(c.shape[0], dec_batch, C_WIDTH, band_buf)

    def frames_first(c):
        return jnp.transpose(c.reshape(-1, dec_batch, C_HEADS, C_HEAD_DIM, band_buf), (0, 1, 4, 2, 3))

    s_bk = frames_first(cache_roll(frames_last(cache_band_k), band_tails, 0, batch, dec_seq))
    s_bv = frames_first(cache_roll(frames_last(cache_band_v), band_tails, 1, batch, dec_seq))
    st = {k: jnp.stack(v) for k, v in outs.items() if v}
    return (y_prompt, y_sample, st["p_ckv"], st["p_kr"], st["p_C"], st["p_n"], st["p_m"], st["p_bk"], st["p_bv"],
            st["s_ckv"], st["s_kr"], st["s_C"], st["s_n"], st["s_m"], s_bk, s_bv)
```

```python
import functools
import math

import jax
import jax.numpy as jnp
from jax import lax
from jax.experimental import pallas as pl
from jax.experimental.pallas import tpu as pltpu

F32 = jnp.float32
BF16 = jnp.bfloat16

D_MODEL = 1024
CHUNK = 64
A_HEADS = 4
A_HEAD_DIM = 128
A_WIDTH = A_HEADS * A_HEAD_DIM
B_HEADS = 8
B_NOPE = 64
B_ROPE = 32
B_VDIM = 64
B_QK = B_NOPE + B_ROPE
B_WIDTH = B_HEADS * B_VDIM
Q_RANK = 384
KV_RANK = 256
ROPE_THETA = 10000.0
MLA_SCALE = B_QK ** -0.5
C_HEADS = 16
C_HEAD_DIM = 64
C_WIDTH = C_HEADS * C_HEAD_DIM
LEFT_CHUNKS = 8
MAX_REL = 128
C_SCALE = C_HEAD_DIM ** -0.5
EPS = 1e-6
NEG = -1e30
LOG2E = math.log2(math.e)

LANES = 128
MLA_HEAD_PAD = 128
MLA_KV_BLOCK = 512
MLA_Q_TILE = 512
MAX_OVERSHOOT = 64.0
MLA_V_ROWS = B_VDIM + 16
MLA_VT_ROWS = B_HEADS * MLA_V_ROWS
BAND_TILE = 256
BAND_LEFT = LEFT_CHUNKS * CHUNK
BAND_WINDOW = BAND_LEFT + BAND_TILE
BAND_ROLL_WIDTH = BAND_WINDOW + BAND_TILE
VMEM_LIMIT = 56 * 1024 * 1024
VMEM_LIMIT_FFN = 61 * 1024 * 1024

SM_CQ = 0
SM_CKV = Q_RANK
SM_KR = Q_RANK + KV_RANK
SM_WIDTH = SM_KR + LANES
GATE_I_LANE = B_ROPE
GATE_F_LANE = B_ROPE + A_HEADS


def _const_spec(shape):
    zeros = (0,) * len(shape)
    return pl.BlockSpec(shape, lambda *_: zeros, pipeline_mode=pl.Buffered(1))


def _params(semantics, vmem_limit=VMEM_LIMIT):
    return pltpu.CompilerParams(dimension_semantics=semantics, vmem_limit_bytes=vmem_limit)


def _rms(x, g):
    return x * lax.rsqrt(jnp.mean(x * x, axis=-1, keepdims=True) + EPS) * g


def _dot(a, b):
    return jnp.dot(a, b, preferred_element_type=F32)


def _dot_nt(a, b):
    return lax.dot_general(a, b, (((1,), (1,)), ((), ())), preferred_element_type=F32)


def _dot_tn(a, b):
    return lax.dot_general(a, b, (((0,), (0,)), ((), ())), preferred_element_type=F32)


def _row_specs(parts, tm, p_tiles):
    width = parts[0].shape[1]
    if len(parts) == 1:
        return [pl.BlockSpec((tm, width), lambda i: (i, 0))]
    assert parts[1].shape[0] == tm
    return [pl.BlockSpec((tm, width), lambda i: (jnp.minimum(i, p_tiles - 1), 0)),
            pl.BlockSpec((tm, width), lambda i: (0, 0), pipeline_mode=pl.Buffered(1))]


def _row_load(refs, p_tiles):
    if len(refs) == 1:
        return refs[0][...]
    return jnp.where(pl.program_id(0) < p_tiles, refs[0][...], refs[1][...])


def _norm_proj_body(*refs, n_x, p_tiles, plan):
    x_refs = refs[:n_x]
    g_ref, w_ref = refs[n_x:n_x + 2]
    out_refs = refs[n_x + 2:]
    h = _rms(_row_load(x_refs, p_tiles), g_ref[...]).astype(BF16)
    for w0, width, dests in plan:
        z = _dot(h, w_ref[:, w0:w0 + width])
        for out_idx, o0, scale in dests:
            o_ref = out_refs[out_idx]
            zz = z if scale == 1.0 else z * scale
            o_ref[:, o0:o0 + width] = zz.astype(o_ref.dtype)


def norm_proj(x_parts, g, w, plan, out_widths, out_dtypes, tm):
    t = sum(p.shape[0] for p in x_parts)
    d = x_parts[0].shape[1]
    n = w.shape[1]
    p_tiles = x_parts[0].shape[0] // tm
    return pl.pallas_call(
        functools.partial(_norm_proj_body, n_x=len(x_parts), p_tiles=p_tiles, plan=plan),
        grid=(t // tm,),
        in_specs=_row_specs(x_parts, tm, p_tiles) + [_const_spec((1, d)), _const_spec((d, n))],
        out_specs=[pl.BlockSpec((tm, ow), lambda i: (i, 0)) for ow in out_widths],
        out_shape=[jax.ShapeDtypeStruct((t, ow), dt) for ow, dt in zip(out_widths, out_dtypes)],
        compiler_params=_params(("parallel",)),
        name="norm_proj",
    )(*x_parts, g, w)


def _mix_ffn_body(*refs, n_x, mix_counts, p_tiles, ff_chunks, final):
    x_refs = refs[:n_x]
    pos = n_x
    a_groups = []
    for count in mix_counts:
        a_groups.append(refs[pos:pos + count])
        pos += count
    wo_refs = refs[pos:pos + len(mix_counts)]
    pos += len(mix_counts)
    g_ref, wg_ref, wu_ref, wd_ref = refs[pos:pos + 4]
    pos += 4
    if final:
        gf_ref = refs[pos]
        pos += 1
    out_refs = refs[pos:-1]
    act_ref = refs[-1]
    x = _row_load(x_refs, p_tiles)
    for a_refs, wo_ref in zip(a_groups, wo_refs):
        x = x + _dot(_row_load(a_refs, p_tiles), wo_ref[...])
    h = _rms(x, g_ref[...]).astype(BF16)
    for c0, cw in ff_chunks:
        gate = _dot(h, wg_ref[:, c0:c0 + cw])
        up = _dot(h, wu_ref[:, c0:c0 + cw])
        act_ref[:, c0:c0 + cw] = (gate * jax.nn.sigmoid(gate) * up).astype(BF16)
    y = x + _dot(act_ref[...], wd_ref[...])
    if final:
        y = _rms(y, gf_ref[...])
    if len(out_refs) == 1:
        out_refs[0][...] = y
    else:
        i = pl.program_id(0)

        @pl.when(i < p_tiles)
        def _():
            out_refs[0][...] = y

        @pl.when(i >= p_tiles)
        def _():
            out_refs[1][...] = y


def mix_ffn(x_parts, mix_in, mix_w, g, wg, wu, wd, g_final, tm, p_rows, split_out):
    t = sum(p.shape[0] for p in x_parts)
    d = x_parts[0].shape[1]
    f = wg.shape[1]
    p_tiles = p_rows // tm
    chunk = 512
    ff_chunks = [(c0, min(chunk, f - c0)) for c0 in range(0, f, chunk)]
    final = g_final is not None
    in_specs = _row_specs(x_parts, tm, p_tiles)
    args = list(x_parts)
    for parts in mix_in:
        in_specs += _row_specs(parts, tm, p_tiles)
        args += list(parts)
    in_specs += [_const_spec(w.shape) for w in mix_w]
    in_specs += [_const_spec((1, d)), _const_spec((d, f)), _const_spec((d, f)), _const_spec((f, d))]
    args += [*mix_w, g, wg, wu, wd]
    if final:
        in_specs.append(_const_spec((1, d)))
        args.append(g_final)
    if split_out:
        assert t - p_rows == tm
        out_specs = [pl.BlockSpec((tm, d), lambda i: (jnp.minimum(i, p_tiles - 1), 0)),
                     pl.BlockSpec((tm, d), lambda i: (0, 0), pipeline_mode=pl.Buffered(1))]
        out_shape = [jax.ShapeDtypeStruct((p_rows, d), F32), jax.ShapeDtypeStruct((t - p_rows, d), F32)]
    else:
        out_specs = pl.BlockSpec((tm, d), lambda i: (i, 0))
        out_shape = jax.ShapeDtypeStruct((t, d), F32)
    return pl.pallas_call(
        functools.partial(_mix_ffn_body, n_x=len(x_parts), mix_counts=tuple(len(p) for p in mix_in),
                          p_tiles=p_tiles, ff_chunks=ff_chunks, final=final),
        grid=(t // tm,),
        in_specs=in_specs,
        out_specs=out_specs,
        out_shape=out_shape,
        scratch_shapes=[pltpu.VMEM((tm, f), BF16)],
        compiler_params=_params(("arbitrary",), VMEM_LIMIT_FFN),
        name="mix_ffn",
    )(*args)


def _mla_prep_q_body(sm_ref, qn_ref, kvn_ref, wq_ref, rc_ref, rs1_ref, rs2_ref, cos_t_ref, sin_t_ref,
                     ckv_ref, kr_ref, qt_ref):
    sm = sm_ref[...]
    cq = _rms(sm[:, SM_CQ:SM_CQ + Q_RANK], qn_ref[...]).astype(BF16)
    ckv_ref[...] = _rms(sm[:, SM_CKV:SM_CKV + KV_RANK], kvn_ref[...])
    grp = sm[:, SM_KR:SM_KR + LANES]
    half = B_ROPE // 2
    rot = (grp * rc_ref[...] + pltpu.roll(grp, half, 1) * rs1_ref[...]
           + pltpu.roll(grp, LANES - half, 1) * rs2_ref[...])
    kr_ref[...] = rot[:, :B_ROPE]
    qt = _dot_nt(wq_ref[...], cq) * (MLA_SCALE * LOG2E)
    cos_t = cos_t_ref[...]
    sin_t = sin_t_ref[...]
    for h in range(B_HEADS):
        r0 = h * MLA_HEAD_PAD
        x1 = qt[r0 + B_NOPE:r0 + B_NOPE + half]
        x2 = qt[r0 + B_NOPE + half:r0 + B_QK]
        qt_ref[0, r0:r0 + B_NOPE, :] = qt[r0:r0 + B_NOPE].astype(BF16)
        rot_q = jnp.concatenate([x1 * cos_t - x2 * sin_t, x1 * sin_t + x2 * cos_t], axis=0)
        qt_ref[0, r0 + B_NOPE:r0 + B_QK, :] = rot_q.astype(BF16)
        qt_ref[0, r0 + B_QK:r0 + MLA_HEAD_PAD, :] = qt[r0 + B_QK:r0 + MLA_HEAD_PAD].astype(BF16)


def mla_prep_q(small, qn, kvn, wq_t, rope_tabs, tab_index, tm):
    t = small.shape[0]
    rc, rs1, rs2, cos_t, sin_t = rope_tabs
    half = B_ROPE // 2
    row_tab = pl.BlockSpec((tm, LANES), lambda i: (tab_index(i), 0))
    col_tab = pl.BlockSpec((half, tm), lambda i: (0, tab_index(i)))
    return pl.pallas_call(
        _mla_prep_q_body,
        grid=(t // tm,),
        in_specs=[pl.BlockSpec((tm, SM_WIDTH), lambda i: (i, 0)), _const_spec(qn.shape), _const_spec(kvn.shape),
                  _const_spec(wq_t.shape), row_tab, row_tab, row_tab, col_tab, col_tab],
        out_specs=[pl.BlockSpec((tm, KV_RANK), lambda i: (i, 0)), pl.BlockSpec((tm, B_ROPE), lambda i: (i, 0)),
                   pl.BlockSpec((1, B_HEADS * MLA_HEAD_PAD, tm), lambda i: (i, 0, 0))],
        out_shape=[jax.ShapeDtypeStruct((t, KV_RANK), F32), jax.ShapeDtypeStruct((t, B_ROPE), F32),
                   jax.ShapeDtypeStruct((t // tm, B_HEADS * MLA_HEAD_PAD, tm), BF16)],
        compiler_params=_params(("parallel",)),
        name="mla_prep_q",
    )(small, qn, kvn, wq_t, rc, rs1, rs2, cos_t, sin_t)


def _mla_prep_kv_body(ckv_ref, kr_ref, wk_ref, place_ref, wv_ref, kp_ref, vt_ref):
    c = ckv_ref[...].astype(BF16)
    kr = kr_ref[...].astype(BF16)
    kp_ref[...] = (_dot(c, wk_ref[...]) + _dot(kr, place_ref[...])).astype(BF16)
    vt = _dot_nt(wv_ref[...], c).astype(BF16)
    ones = jnp.ones((MLA_V_ROWS - B_VDIM, vt.shape[1]), BF16)
    for h in range(B_HEADS):
        vt_ref[0, h * MLA_V_ROWS:h * MLA_V_ROWS + B_VDIM, :] = vt[h * B_VDIM:(h + 1) * B_VDIM]
        vt_ref[0, h * MLA_V_ROWS + B_VDIM:(h + 1) * MLA_V_ROWS, :] = ones


def mla_prep_kv(ckv, krope, wk_pad, place, wv_t, n_tiles, tm):
    kw = B_HEADS * MLA_HEAD_PAD
    return pl.pallas_call(
        _mla_prep_kv_body,
        grid=(n_tiles,),
        in_specs=[pl.BlockSpec((tm, KV_RANK), lambda i: (i, 0)), pl.BlockSpec((tm, B_ROPE), lambda i: (i, 0)),
                  _const_spec(wk_pad.shape), _const_spec(place.shape), _const_spec(wv_t.shape)],
        out_specs=[pl.BlockSpec((tm, kw), lambda i: (i, 0)), pl.BlockSpec((1, MLA_VT_ROWS, tm), lambda i: (i, 0, 0))],
        out_shape=[jax.ShapeDtypeStruct((n_tiles * tm, kw), BF16),
                   jax.ShapeDtypeStruct((n_tiles, MLA_VT_ROWS, tm), BF16)],
        compiler_params=_params(("parallel",)),
        name="mla_prep_kv",
    )(ckv, krope, wk_pad, place, wv_t)


def _mla_attn_body(qt_ref, kp_ref, vt_ref, o_ref, m_ref, l_ref, acc_ref, gap_ref, *, tq, tk, q0, n_diag):
    i = pl.program_id(1)
    start = q0 + i * tq
    n_full = start // tk
    key_chunk = lax.broadcasted_iota(jnp.int32, (tk, tq), 0) // CHUNK
    qry_chunk = lax.broadcasted_iota(jnp.int32, (tk, tq), 1) // CHUNK

    def step(j, masked, stale_max):
        row0 = pl.multiple_of(j * tk, tk)
        if masked:
            visible = key_chunk + (j * tk - start) // CHUNK <= qry_chunk

        def scores(h):
            k_h = kp_ref[pl.ds(row0, tk), h * MLA_HEAD_PAD:(h + 1) * MLA_HEAD_PAD]
            q_h = qt_ref[0, h * MLA_HEAD_PAD:(h + 1) * MLA_HEAD_PAD, :]
            return _dot(k_h, q_h)

        s_next = scores(0)
        for h in range(B_HEADS):
            s = s_next
            if h + 1 < B_HEADS:
                s_next = scores(h + 1)
            if masked:
                s = jnp.where(visible, s, NEG)
            rows = slice(h * B_VDIM, (h + 1) * B_VDIM)
            m_prev = m_ref[h:h + 1, :]
            blk_max = jnp.max(s, axis=0, keepdims=True)
            m_new = jnp.maximum(m_prev, blk_max)
            alpha = jnp.exp2(m_prev - m_new)
            p = jnp.exp2(s - (m_prev if stale_max else m_new)).astype(BF16)
            pv = _dot(vt_ref[j, h * MLA_V_ROWS:(h + 1) * MLA_V_ROWS, :], p)
            if stale_max:
                gap_ref[h:h + 1, :] = jnp.maximum(gap_ref[h:h + 1, :], blk_max - m_prev)
                l_ref[h:h + 1, :] = alpha * (l_ref[h:h + 1, :] + pv[B_VDIM:B_VDIM + 1])
                acc_ref[rows, :] = alpha * (acc_ref[rows, :] + pv[:B_VDIM])
            else:
                l_ref[h:h + 1, :] = alpha * l_ref[h:h + 1, :] + pv[B_VDIM:B_VDIM + 1]
                acc_ref[rows, :] = alpha * acc_ref[rows, :] + pv[:B_VDIM]
            m_ref[h:h + 1, :] = m_new

    def sweep(stale_max):
        m_ref[...] = jnp.full(m_ref.shape, NEG, F32)
        l_ref[...] = jnp.zeros(l_ref.shape, F32)
        acc_ref[...] = jnp.zeros(acc_ref.shape, F32)
        first = 0
        if stale_max:
            @pl.when(n_full > 0)
            def _():
                step(0, False, False)

            first = 1

        def full_step(j, carry):
            step(j, False, stale_max)
            return carry

        lax.fori_loop(first, n_full, full_step, 0)
        for d in range(n_diag):
            if stale_max and d == 0:
                @pl.when(n_full > 0)
                def _():
                    step(n_full, True, True)

                @pl.when(n_full == 0)
                def _():
                    step(n_full, True, False)
            else:
                step(n_full + d, True, stale_max)

    gap_ref[...] = jnp.zeros(gap_ref.shape, F32)
    sweep(True)

    @pl.when(jnp.max(gap_ref[...]) > MAX_OVERSHOOT)
    def _():
        sweep(False)

    for h in range(B_HEADS):
        rows = slice(h * B_VDIM, (h + 1) * B_VDIM)
        acc_ref[rows, :] = acc_ref[rows, :] / l_ref[h:h + 1, :]
    o_ref[...] = jnp.transpose(acc_ref[...]).astype(o_ref.dtype)


def mla_attn(qt, kp, vt, groups, nq, tq, tk, n_blocks, q0, q_frames):
    kw = B_HEADS * MLA_HEAD_PAD
    assert q0 % tk == 0 and (tq % tk == 0 or (nq == 1 and q_frames <= tk))
    n_diag = pl.cdiv(q_frames, tk)
    return pl.pallas_call(
        functools.partial(_mla_attn_body, tq=tq, tk=tk, q0=q0, n_diag=n_diag),
        grid=(groups, nq),
        in_specs=[pl.BlockSpec((1, kw, tq), lambda g, i: (g * nq + i, 0, 0)),
                  pl.BlockSpec((n_blocks * tk, kw), lambda g, i: (g, 0), pipeline_mode=pl.Buffered(1)),
                  pl.BlockSpec((n_blocks, MLA_VT_ROWS, tk), lambda g, i: (g, 0, 0), pipeline_mode=pl.Buffered(1))],
        out_specs=pl.BlockSpec((tq, B_WIDTH), lambda g, i: (g * nq + i, 0)),
        out_shape=jax.ShapeDtypeStruct((groups * nq * tq, B_WIDTH), BF16),
        scratch_shapes=[pltpu.VMEM((B_HEADS, tq), F32), pltpu.VMEM((B_HEADS, tq), F32),
                        pltpu.VMEM((B_WIDTH, tq), F32), pltpu.VMEM((B_HEADS, tq), F32)],
        compiler_params=_params(("parallel", "arbitrary")),
        name="mla_attn",
    )(qt, kp, vt)


def _mla_sample_body(sm_ref, qn_ref, wq_ref, rc_ref, rs1_ref, rs2_ref, wabs_ref, sel_ref, cckv_ref, ckr_ref,
                     nckv_ref, nkr_ref, wuv_ref, o_ref, m_ref, l_ref, acc_ref, *, n_chunks, chunk):
    frames = sm_ref.shape[0]
    half = B_ROPE // 2
    cq = _rms(sm_ref[:, SM_CQ:SM_CQ + Q_RANK], qn_ref[...]).astype(BF16)
    qf = _dot(cq, wq_ref[...]) * (MLA_SCALE * LOG2E)
    width = qf.shape[1]
    qb = (qf * rc_ref[...] + pltpu.roll(qf, half, 1) * rs1_ref[...]
          + pltpu.roll(qf, width - half, 1) * rs2_ref[...]).astype(BF16)
    q_lat, q_rope = [], []
    for h in range(B_HEADS):
        q_h = qb[:, h * MLA_HEAD_PAD:(h + 1) * MLA_HEAD_PAD]
        q_lat.append(_dot(q_h, wabs_ref[h]).astype(BF16))
        q_rope.append(_dot(q_h, sel_ref[...]).astype(BF16))
    q_lat = jnp.concatenate(q_lat, axis=0)
    q_rope = jnp.concatenate(q_rope, axis=0)
    m_ref[...] = jnp.full(m_ref.shape, NEG, F32)
    l_ref[...] = jnp.zeros(l_ref.shape, F32)
    acc_ref[...] = jnp.zeros(acc_ref.shape, F32)

    def attend(lat, rope, rope_feature_major):
        lat = lat.astype(BF16)
        rope = rope.astype(BF16)
        s_rope = _dot(q_rope, rope) if rope_feature_major else _dot_nt(q_rope, rope)
        s = _dot_nt(q_lat, lat) + s_rope
        m_prev = m_ref[...]
        m_new = jnp.maximum(m_prev, jnp.max(s, axis=1, keepdims=True))
        alpha = jnp.exp2(m_prev - m_new)
        p = jnp.exp2(s - m_new)
        l_ref[...] = alpha * l_ref[...] + jnp.sum(p, axis=1, keepdims=True)
        acc_ref[...] = alpha * acc_ref[...] + _dot(p.astype(BF16), lat)
        m_ref[...] = m_new

    for c in range(n_chunks):
        attend(cckv_ref[0, 0, c * chunk:(c + 1) * chunk, :], ckr_ref[0, 0, :, c * chunk:(c + 1) * chunk], True)
    attend(nckv_ref[...], nkr_ref[...], False)
    o_lat = (acc_ref[...] / l_ref[...]).astype(BF16)
    out = _dot(o_lat[:frames], wuv_ref[0])
    for h in range(1, B_HEADS):
        out = out + _dot(o_lat[h * frames:(h + 1) * frames], wuv_ref[h])
    o_ref[...] = out.astype(o_ref.dtype)


def mla_sample(small, ckv, krope, cache_ckv, cache_krope_t, layer, qn, wq, rope_full, wabs, sel, wuv_place,
               streams, frames, row0):
    past = cache_ckv.shape[2]
    chunk = 512
    base = row0 // frames
    rows_q = B_HEADS * frames
    rc, rs1, rs2 = rope_full
    return pl.pallas_call(
        functools.partial(_mla_sample_body, n_chunks=past // chunk, chunk=chunk),
        grid=(streams,),
        in_specs=[pl.BlockSpec((frames, SM_WIDTH), lambda g: (base + g, 0)), _const_spec(qn.shape),
                  _const_spec(wq.shape), _const_spec(rc.shape), _const_spec(rs1.shape), _const_spec(rs2.shape),
                  _const_spec(wabs.shape), _const_spec(sel.shape),
                  pl.BlockSpec((1, 1, past, KV_RANK), lambda g: (layer, g, 0, 0)),
                  pl.BlockSpec((1, 1, B_ROPE, past), lambda g: (layer, g, 0, 0)),
                  pl.BlockSpec((frames, KV_RANK), lambda g: (base + g, 0)),
                  pl.BlockSpec((frames, B_ROPE), lambda g: (base + g, 0)),
                  _const_spec(wuv_place.shape)],
        out_specs=pl.BlockSpec((frames, B_WIDTH), lambda g: (g, 0)),
        out_shape=jax.ShapeDtypeStruct((streams * frames, B_WIDTH), BF16),
        scratch_shapes=[pltpu.VMEM((rows_q, 1), F32), pltpu.VMEM((rows_q, 1), F32),
                        pltpu.VMEM((rows_q, KV_RANK), F32)],
        compiler_params=_params(("parallel",)),
        name="mla_sample",
    )(small, qn, wq, rc, rs1, rs2, wabs, sel, cache_ckv, cache_krope_t, ckv, krope, wuv_place)


def _scan_rows(x, op, fill, length):
    row = lax.broadcasted_iota(jnp.int32, x.shape, 0)
    shift = 1
    while shift < length:
        moved = pltpu.roll(x, shift, 0)
        x = op(x, jnp.where(row >= shift, moved, fill))
        shift *= 2
    return x


def _mlstm_body(*refs, blk, per_step):
    n_in = 5 * per_step
    bias_ref, hn_ref, c0_ref, m0_ref, ha_ref, c_out_ref, m_out_ref, c_scr, m_scr = refs[n_in:]
    dh = A_HEAD_DIM
    step_idx = pl.program_id(1)

    @pl.when(step_idx == 0)
    def _():
        c_scr[...] = c0_ref[...]
        m_scr[...] = m0_ref[...]

    causal = (lax.broadcasted_iota(jnp.int32, (blk, blk), 0) >= lax.broadcasted_iota(jnp.int32, (blk, blk), 1))
    ones_col = (lax.broadcasted_iota(jnp.int32, (blk, dh), 1) == 0).astype(BF16)
    pad = max(blk, LANES) - blk
    prep = []
    for u in range(per_step):
        gt_ref = refs[5 * u + 4]
        gates = gt_ref[...] + bias_ref[...]
        log_f = jax.nn.log_sigmoid(gates)
        b_all = pltpu.roll(_scan_rows(log_f, jnp.add, 0.0, blk), LANES - A_HEADS, 1)
        a_all = gates - b_all
        amax_all = _scan_rows(a_all, jnp.maximum, NEG, blk)
        a_sq = a_all if pad == 0 else jnp.concatenate([a_all, jnp.zeros((pad, LANES), F32)], axis=0)
        prep.append((a_all, b_all, amax_all, jnp.transpose(a_sq)))
    chains = [(u, h) for u in range(per_step) for h in range(A_HEADS)]

    def cols(h):
        return slice(h * dh, (h + 1) * dh)

    scores, v_exts, run_maxes, m_prevs = [], [], [], []
    for u, h in chains:
        q_ref, k_ref, v_ref = refs[5 * u:5 * u + 3]
        scores.append(_dot_nt(q_ref[:, cols(h)], k_ref[:, cols(h)]))
        v_exts.append(jnp.concatenate([v_ref[:, cols(h)], ones_col], axis=1))
        m_prev = m_scr[u, h:h + 1, 0:1]
        lane = GATE_I_LANE + h
        m_prevs.append(m_prev)
        run_maxes.append(jnp.maximum(prep[u][2][:, lane:lane + 1], m_prev))
    probs = []
    for n, (u, h) in enumerate(chains):
        lane = GATE_I_LANE + h
        a_row = prep[u][3][lane:lane + 1, :blk]
        decay_mat = jnp.exp(jnp.where(causal, a_row - run_maxes[n], NEG))
        probs.append((scores[n] * decay_mat).astype(BF16))
    numdens = []
    for n, (u, h) in enumerate(chains):
        q_ref = refs[5 * u]
        state = c_scr[u, h]
        w_inter = jnp.exp(m_prevs[n] - run_maxes[n])
        numdens.append(w_inter * _dot(q_ref[:, cols(h)], state.astype(BF16)) + _dot(probs[n], v_exts[n]))
    for n, (u, h) in enumerate(chains):
        o_ref = refs[5 * u + 3]
        lane = GATE_I_LANE + h
        b_col = prep[u][1][:, lane:lane + 1]
        numden = numdens[n]
        den = numden[:, dh:dh + 1]
        hh = numden[:, :dh] / jnp.maximum(jnp.abs(den), jnp.exp(-(b_col + run_maxes[n])))
        hh = _rms(hh, hn_ref[:, cols(h)])
        ha_ref[u, :, cols(h)] = (hh * jax.nn.sigmoid(o_ref[:, cols(h)])).astype(ha_ref.dtype)
    for n, (u, h) in enumerate(chains):
        k_ref = refs[5 * u + 1]
        lane = GATE_I_LANE + h
        a_col = prep[u][0][:, lane:lane + 1]
        b_col = prep[u][1][:, lane:lane + 1]
        max_last = run_maxes[n][blk - 1:blk, :]
        k_w = (k_ref[:, cols(h)].astype(F32) * jnp.exp(a_col - max_last)).astype(BF16)
        c_scr[u, h] = jnp.exp(m_prevs[n] - max_last) * c_scr[u, h] + _dot_tn(k_w, v_exts[n])
        m_scr[u, h:h + 1, :] = jnp.broadcast_to(b_col[blk - 1:blk, :] + max_last, (1, LANES))

    @pl.when(step_idx == pl.num_programs(1) - 1)
    def _():
        c_out_ref[...] = c_scr[...]
        m_out_ref[...] = m_scr[...]


def mlstm(qkv, o32, small, bias, hnorm, c0, m0, groups, steps, blk, row0, per_step):
    base = row0 // blk
    gate_block = SM_KR // LANES
    state_shape = (per_step, A_HEADS, A_HEAD_DIM, 2 * A_HEAD_DIM)

    def rows(u, col):
        return lambda g, s: (base + (g * per_step + u) * steps + s, col)

    in_specs, args = [], []
    for u in range(per_step):
        in_specs += [pl.BlockSpec((blk, A_WIDTH), rows(u, 0)), pl.BlockSpec((blk, A_WIDTH), rows(u, 1)),
                     pl.BlockSpec((blk, A_WIDTH), rows(u, 2)), pl.BlockSpec((blk, A_WIDTH), rows(u, 0)),
                     pl.BlockSpec((blk, LANES), rows(u, gate_block))]
        args += [qkv, qkv, qkv, o32, small]
    in_specs += [_const_spec((1, LANES)), _const_spec((1, A_WIDTH)),
                 pl.BlockSpec(state_shape, lambda g, s: (g, 0, 0, 0)),
                 pl.BlockSpec((per_step, 8, LANES), lambda g, s: (g, 0, 0))]
    ha, c_out, m_out = pl.pallas_call(
        functools.partial(_mlstm_body, blk=blk, per_step=per_step),
        grid=(groups // per_step, steps),
        in_specs=in_specs,
        out_specs=[pl.BlockSpec((per_step, blk, A_WIDTH), lambda g, s: (g, s, 0)),
                   pl.BlockSpec(state_shape, lambda g, s: (g, 0, 0, 0)),
                   pl.BlockSpec((per_step, 8, LANES), lambda g, s: (g, 0, 0))],
        out_shape=[jax.ShapeDtypeStruct((groups, steps * blk, A_WIDTH), BF16),
                   jax.ShapeDtypeStruct((groups, A_HEADS, A_HEAD_DIM, 2 * A_HEAD_DIM), F32),
                   jax.ShapeDtypeStruct((groups, 8, LANES), F32)],
        scratch_shapes=[pltpu.VMEM(state_shape, F32), pltpu.VMEM((per_step, 8, LANES), F32)],
        compiler_params=_params(("parallel", "arbitrary")),
        name="mlstm",
    )(*args, bias, hnorm, c0, m0)
    return ha.reshape(groups * steps * blk, A_WIDTH), c_out, m_out


def _band_proj_body(x_ref, g_ref, wk_ref, wq_t_ref, wv_t_ref, wkv_t_ref, k_ref, qt_ref, vt_ref, tail_ref,
                    *, tiles_per_seq, p_tiles):
    i = pl.program_id(0)
    h = _rms(x_ref[...], g_ref[...]).astype(BF16)
    step = 512
    for c0 in range(0, C_WIDTH, step):
        k_ref[:, c0:c0 + step] = _dot(h, wk_ref[:, c0:c0 + step]).astype(BF16)
        qt = _dot_nt(wq_t_ref[c0:c0 + step, :], h) * (C_SCALE * LOG2E)
        qt_ref[0, c0:c0 + step, :] = qt.astype(BF16)
        vt_ref[0, c0:c0 + step, :] = _dot_nt(wv_t_ref[c0:c0 + step, :], h).astype(BF16)

    is_tail = ((i + 1) % tiles_per_seq == 0) | (i >= p_tiles)

    @pl.when(is_tail)
    def _():
        for c0 in range(0, 2 * C_WIDTH, step):
            tail_ref[0, c0:c0 + step, :] = _dot_nt(wkv_t_ref[c0:c0 + step, :], h)

    @pl.when(jnp.logical_not(is_tail) & (i % tiles_per_seq == 0))
    def _():
        tail_ref[...] = jnp.zeros(tail_ref.shape, F32)


def band_proj(x, g, wk, wq_t, wv_t, wkv_t, tm, tiles_per_seq, p_tiles, n_seq):
    t, d = x.shape
    n_tiles = t // tm
    n_tail = n_seq + n_tiles - p_tiles
    cw = C_WIDTH

    def tail_index(i):
        return (jnp.where(i < p_tiles, i // tiles_per_seq, n_seq + i - p_tiles), 0, 0)

    return pl.pallas_call(
        functools.partial(_band_proj_body, tiles_per_seq=tiles_per_seq, p_tiles=p_tiles),
        grid=(n_tiles,),
        in_specs=[pl.BlockSpec((tm, d), lambda i: (i, 0)), _const_spec((1, d)), _const_spec((d, cw)),
                  _const_spec((cw, d)), _const_spec((cw, d)), _const_spec((2 * cw, d))],
        out_specs=[pl.BlockSpec((tm, cw), lambda i: (i, 0)), pl.BlockSpec((1, cw, tm), lambda i: (i, 0, 0)),
                   pl.BlockSpec((1, cw, tm), lambda i: (i, 0, 0)), pl.BlockSpec((1, 2 * cw, tm), tail_index)],
        out_shape=[jax.ShapeDtypeStruct((t, cw), BF16), jax.ShapeDtypeStruct((n_tiles, cw, tm), BF16),
                   jax.ShapeDtypeStruct((n_tiles, cw, tm), BF16), jax.ShapeDtypeStruct((n_tail, 2 * cw, tm), F32)],
        compiler_params=_params(("arbitrary",)),
        name="band_proj",
    )(x, g, wk, wq_t, wv_t, wkv_t)


def _band_table_body(base_ref, tab_ref):
    tq = BAND_TILE
    shape = (tq, BAND_ROLL_WIDTH)
    row = lax.broadcasted_iota(jnp.int32, shape, 0)
    tab = jnp.broadcast_to(base_ref[0], shape)
    shift = 1
    while shift < tq:
        tab = jnp.where((row & shift) != 0, pltpu.roll(tab, shift, 1), tab)
        shift *= 2
    win = lax.broadcasted_iota(jnp.int32, (tq, BAND_WINDOW), 1) // CHUNK
    qch = lax.broadcasted_iota(jnp.int32, (tq, BAND_WINDOW), 0) // CHUNK
    valid = (win >= qch) & (win <= qch + LEFT_CHUNKS)
    tab_ref[0] = jnp.transpose(jnp.where(valid, tab[:, :BAND_WINDOW] * LOG2E, NEG))


def band_table(base_rows):
    return pl.pallas_call(
        _band_table_body,
        grid=(C_HEADS,),
        in_specs=[pl.BlockSpec((1, 1, BAND_ROLL_WIDTH), lambda h: (h, 0, 0))],
        out_specs=pl.BlockSpec((1, BAND_WINDOW, BAND_TILE), lambda h: (h, 0, 0)),
        out_shape=jax.ShapeDtypeStruct((C_HEADS, BAND_WINDOW, BAND_TILE), F32),
        compiler_params=_params(("parallel",)),
        name="band_table",
    )(base_rows)


def _band_body(qt_ref, *refs, clamp_start, cache_values):
    nb = BAND_WINDOW // BAND_TILE
    k_refs = refs[:nb]
    v_refs = refs[nb:2 * nb]
    tab_ref, o_ref, acc_ref = refs[2 * nb:]
    tq = BAND_TILE
    i = pl.program_id(1)
    row_half = lax.broadcasted_iota(jnp.int32, (LANES, tq), 0) // C_HEAD_DIM
    ones = jnp.ones((16, tq), BF16)

    def value_rows(j, rows):
        if cache_values and j < nb - 1:
            v = v_refs[j][0, 0, rows, :].astype(BF16)
        else:
            v = v_refs[j][0, rows, :]
        return jnp.concatenate([v, ones], axis=0)

    def run(first_tiles):
        def scores(h):
            pair, half = divmod(h, 2)
            q_pair = qt_ref[0, pair * LANES:(pair + 1) * LANES, :]
            q_h = jnp.where(row_half == half, q_pair, jnp.zeros_like(q_pair))
            parts = []
            for j in range(nb):
                if cache_values and j < nb - 1:
                    k_t = k_refs[j][0, 0, pair * LANES:(pair + 1) * LANES, :]
                    s_j = _dot_tn(k_t.astype(BF16), q_h)
                else:
                    s_j = _dot(k_refs[j][:, pair * LANES:(pair + 1) * LANES], q_h)
                if first_tiles and j < nb - 1:
                    s_j = jnp.where(i + j < nb - 1, NEG, s_j)
                parts.append(s_j)
            return jnp.concatenate(parts, axis=0)

        s_next = scores(0)
        for h in range(C_HEADS):
            s = s_next + tab_ref[h]
            if h + 1 < C_HEADS:
                s_next = scores(h + 1)
            m = jnp.max(s, axis=0, keepdims=True)
            p = jnp.exp2(s - m).astype(BF16)
            rows = slice(h * C_HEAD_DIM, (h + 1) * C_HEAD_DIM)
            pv = None
            for j in range(nb):
                part = _dot(value_rows(j, rows), p[j * BAND_TILE:(j + 1) * BAND_TILE])
                pv = part if pv is None else pv + part
            acc_ref[rows, :] = pv[:C_HEAD_DIM] / pv[C_HEAD_DIM:C_HEAD_DIM + 1]

    if clamp_start:
        @pl.when(i < nb - 1)
        def _():
            run(True)

        @pl.when(i >= nb - 1)
        def _():
            run(False)
    else:
        run(False)
    o_ref[...] = jnp.transpose(acc_ref[...]).astype(o_ref.dtype)


def band_attn(qt_arr, qt_map, k_arrs, k_specs, v_arrs, v_specs, tab, groups, tiles, clamp_start, cache_values):
    tq = BAND_TILE
    in_specs = [pl.BlockSpec((1, C_WIDTH, tq), qt_map)]
    in_specs += list(k_specs)
    in_specs += list(v_specs)
    in_specs.append(_const_spec(tab.shape))
    return pl.pallas_call(
        functools.partial(_band_body, clamp_start=clamp_start, cache_values=cache_values),
        grid=(groups, tiles),
        in_specs=in_specs,
        out_specs=pl.BlockSpec((tq, C_WIDTH), lambda g, i: (g * tiles + i, 0)),
        out_shape=jax.ShapeDtypeStruct((groups * tiles * tq, C_WIDTH), BF16),
        scratch_shapes=[pltpu.VMEM((C_WIDTH, tq), F32)],
        compiler_params=_params(("parallel", "arbitrary")),
        name="band_attn",
    )(qt_arr, *k_arrs, *v_arrs, tab)


def _cache_roll_body(c_ref, *refs, new_frames):
    tail_refs = refs[:-1]
    o_ref = refs[-1]
    layer = pl.program_id(0)
    stream = pl.program_id(1)
    buf = c_ref.shape[3]
    rolled = pltpu.roll(c_ref[0, 0], buf - new_frames, 1)
    new = tail_refs[0][0]
    for n in range(1, len(tail_refs)):
        new = jnp.where(layer == n, tail_refs[n][0], new)
    upper_half = (stream % (LANES // new_frames)) == 1
    new = jnp.where(upper_half, new, pltpu.roll(new, new_frames, 1))
    o_ref[0, 0] = rolled
    lane = lax.broadcasted_iota(jnp.int32, new.shape, 1)
    o_ref[0, 0, :, buf - LANES:] = jnp.where(lane < LANES - new_frames, rolled[:, buf - LANES:], new)


def cache_roll(cache_t, tails, row_block, first_tile, new_frames):
    layers, streams, feat, buf = cache_t.shape
    tm = tails[0].shape[2]
    per_tile = tm // new_frames
    groups = LANES // new_frames
    assert 2 * new_frames == LANES and layers == len(tails)

    def tail_map(l, s):
        return (first_tile + s // per_tile, row_block, (s % per_tile) // groups)

    return pl.pallas_call(
        functools.partial(_cache_roll_body, new_frames=new_frames),
        grid=(layers, streams),
        in_specs=[pl.BlockSpec((1, 1, feat, buf), lambda l, s: (l, s, 0, 0))]
                 + [pl.BlockSpec((1, feat, LANES), tail_map)] * layers,
        out_specs=pl.BlockSpec((1, 1, feat, buf), lambda l, s: (l, s, 0, 0)),
        out_shape=jax.ShapeDtypeStruct(cache_t.shape, cache_t.dtype),
        compiler_params=_params(("parallel", "parallel")),
        name="cache_roll",
    )(cache_t, *tails)


def _rope_tables(positions):
    half = B_ROPE // 2
    inv = ROPE_THETA ** (-jnp.arange(half, dtype=F32) / half)
    ang = positions.astype(F32)[:, None] * inv[None, :]
    cos, sin = jnp.cos(ang), jnp.sin(ang)
    n = positions.shape[0]
    zeros = jnp.zeros((n, LANES - B_ROPE), F32)
    zh = jnp.zeros((n, half), F32)
    rc = jnp.concatenate([cos, cos, zeros], axis=1)
    rs1 = jnp.concatenate([zh, sin, zeros], axis=1)
    rs2 = jnp.concatenate([-sin, zh, zeros], axis=1)
    return rc, rs1, rs2, cos.T, sin.T


def _band_base_rows(rel_bias):
    x = jnp.arange(BAND_ROLL_WIDTH)
    rel = jnp.where(x < BAND_WINDOW, BAND_LEFT - x, BAND_LEFT + 1)
    idx = jnp.clip(rel, -MAX_REL, MAX_REL) + MAX_REL
    return rel_bias[:, None, idx]


def kernel(x_prompt, x_sample, cache_mla_ckv, cache_mla_krope, state_mlstm_C, state_mlstm_n, state_mlstm_m,
           cache_band_k, cache_band_v, norm_mix, norm_ffn, norm_final, w_in_ab, b_gates, mlstm_hnorm,
           mla_q_norm, mla_kv_norm, mla_w_uq, mla_w_uk, mla_w_uv, w_out_ab, w_qkv_c, w_out_c, rel_bias_c,
           w_gate, w_up, w_down):
    batch, seq, d = x_prompt.shape
    dec_batch, dec_seq, _ = x_sample.shape
    past = cache_mla_ckv.shape[2]
    band_buf = cache_band_k.shape[2]
    depth = norm_mix.shape[0]
    tp = batch * seq
    ts = dec_batch * dec_seq
    t = tp + ts
    tm_big = 1024
    tm = MLA_Q_TILE
    tk = MLA_KV_BLOCK
    assert d == D_MODEL and dec_seq == CHUNK and band_buf == BAND_LEFT and past % tk == 0
    assert tp % tm_big == 0 and ts % tm_big == 0 and seq % tm == 0 and tm % dec_seq == 0 and tm % tk == 0

    assert past % CHUNK == 0 and ts == tm_big
    x_parts = (x_prompt.reshape(tp, d), x_sample.reshape(ts, d))

    pos_tab = jnp.concatenate([jnp.arange(seq, dtype=jnp.int32),
                               past + (jnp.arange(tm, dtype=jnp.int32) % dec_seq)])
    rope_tabs = _rope_tables(pos_tab)
    head_pat = jnp.concatenate([jnp.ones((dec_seq, B_NOPE), F32), jnp.zeros((dec_seq, LANES - B_NOPE), F32)], axis=1)
    roll_pad = ((0, 0), (B_NOPE, LANES - B_QK))
    rope_full = tuple(jnp.tile(tab, (1, B_HEADS)) for tab in (
        head_pat + jnp.pad(rope_tabs[0][seq:seq + dec_seq, :B_ROPE], roll_pad),
        jnp.pad(rope_tabs[1][seq:seq + dec_seq, :B_ROPE], roll_pad),
        jnp.pad(rope_tabs[2][seq:seq + dec_seq, :B_ROPE], roll_pad)))
    sel = jnp.pad(jnp.eye(B_ROPE, dtype=F32), ((B_NOPE, MLA_HEAD_PAD - B_QK), (0, 0))).astype(BF16)
    cache_krope_t = jnp.swapaxes(cache_mla_krope, 2, 3)

    def frames_last(c):
        return jnp.transpose(c, (0, 1, 3, 4, 2)).reshape(c.shape[0], dec_batch, C_WIDTH, band_buf)

    cache_k_t = frames_last(cache_band_k)
    cache_v_t = frames_last(cache_band_v)
    band_tails = []
    p_tiles = tp // tm
    seq_tiles = seq // tm

    def tab_index(i):
        return jnp.where(i < p_tiles, i % seq_tiles, seq_tiles)

    outs = {k: [] for k in ("p_ckv", "p_kr", "p_C", "p_n", "p_m", "p_bk", "p_bv",
                            "s_ckv", "s_kr", "s_C", "s_n", "s_m", "s_bk", "s_bv")}
    a4 = 4 * A_WIDTH
    for layer in range(depth):
        j = layer // 2
        g_mix = norm_mix[layer][None, :]
        g_ffn = norm_ffn[layer][None, :]
        last = layer == depth - 1
        g_fin = norm_final[None, :] if last else None
        ffn_w = (w_gate[layer].astype(BF16), w_up[layer].astype(BF16), w_down[layer].astype(BF16))
        if layer % 2 == 0:
            w = w_in_ab[j]
            gate_cols = jnp.concatenate([w[:, a4 + 2 * A_HEADS + Q_RANK + KV_RANK:], w[:, a4:a4 + 2 * A_HEADS],
                                         jnp.zeros((d, LANES - B_ROPE - 2 * A_HEADS), F32)], axis=1)
            w_all = jnp.concatenate([w[:, :a4], w[:, a4 + 2 * A_HEADS:a4 + 2 * A_HEADS + Q_RANK + KV_RANK],
                                     gate_cols], axis=1).astype(BF16)
            aw = A_WIDTH
            plan = [(0, aw, [(0, 0, 1.0)]), (aw, aw, [(0, aw, A_HEAD_DIM ** -0.5)]), (2 * aw, aw, [(0, 2 * aw, 1.0)]),
                    (3 * aw, aw, [(1, 0, 1.0)]), (4 * aw, Q_RANK, [(2, SM_CQ, 1.0)]),
                    (4 * aw + Q_RANK, KV_RANK + LANES, [(2, SM_CKV, 1.0)])]
            qkv, o32, small = norm_proj(x_parts, g_mix, w_all, plan, (3 * aw, aw, SM_WIDTH), (BF16, F32, F32), tm_big)

            wq = mla_w_uq[j].reshape(Q_RANK, B_HEADS, B_QK)
            wq_t = jnp.pad(wq, ((0, 0), (0, 0), (0, MLA_HEAD_PAD - B_QK))).reshape(Q_RANK, -1).T.astype(BF16)
            ckv, krope, qt = mla_prep_q(small, mla_q_norm[j][None, :], mla_kv_norm[j][None, :], wq_t,
                                        rope_tabs, tab_index, tm)
            wk_pad = jnp.pad(mla_w_uk[j], ((0, 0), (0, 0), (0, MLA_HEAD_PAD - B_NOPE))).reshape(KV_RANK, -1).astype(BF16)
            place = jnp.pad(jnp.eye(B_ROPE, dtype=F32), ((0, 0), (B_NOPE, MLA_HEAD_PAD - B_QK)))
            place = jnp.tile(place, (1, B_HEADS)).astype(BF16)
            wv_t = mla_w_uv[j].reshape(KV_RANK, B_WIDTH).T.astype(BF16)
            kp_p, vt_p = mla_prep_kv(ckv, krope, wk_pad, place, wv_t, tp // tk, tk)
            ckv_s = ckv[tp:].reshape(dec_batch, dec_seq, KV_RANK)
            kr_s = krope[tp:].reshape(dec_batch, dec_seq, B_ROPE)

            hb_p = mla_attn(qt, kp_p, vt_p, batch, seq_tiles, tm, tk, seq // tk, 0, tm)
            wabs = jnp.pad(jnp.transpose(mla_w_uk[j], (1, 2, 0)),
                           ((0, 0), (0, MLA_HEAD_PAD - B_NOPE), (0, 0))).astype(BF16)
            wuv_place = jnp.einsum("rhv,hg->hrgv", mla_w_uv[j], jnp.eye(B_HEADS, dtype=F32))
            wuv_place = wuv_place.reshape(B_HEADS, KV_RANK, B_WIDTH).astype(BF16)
            hb_s = mla_sample(small, ckv, krope, cache_mla_ckv, cache_krope_t, j, mla_q_norm[j][None, :],
                              wq_t.T, rope_full, wabs, sel, wuv_place, dec_batch, dec_seq, tp)

            bias = jnp.zeros((1, LANES), F32).at[0, GATE_I_LANE:GATE_I_LANE + 2 * A_HEADS].set(b_gates[j])
            hn = mlstm_hnorm[j][None, :]
            c0_p = jnp.zeros((batch, A_HEADS, A_HEAD_DIM, 2 * A_HEAD_DIM), F32)
            m0_p = jnp.zeros((batch, 8, LANES), F32)
            blk_p = 256
            ha_p, c_p, m_p = mlstm(qkv, o32, small, bias, hn, c0_p, m0_p, batch, seq // blk_p, blk_p, 0, batch)
            c0_s = jnp.concatenate([state_mlstm_C[j], state_mlstm_n[j][..., None],
                                    jnp.zeros((dec_batch, A_HEADS, A_HEAD_DIM, A_HEAD_DIM - 1), F32)], axis=-1)
            m0_s = jnp.broadcast_to(jnp.pad(state_mlstm_m[j], ((0, 0), (0, 8 - A_HEADS)))[..., None],
                                    (dec_batch, 8, LANES))
            ha_s, c_s, m_s = mlstm(qkv, o32, small, bias, hn, c0_s, m0_s, dec_batch, 1, dec_seq, tp, 2)

            wo = w_out_ab[j].astype(BF16)
            x_parts = mix_ffn(x_parts, ((ha_p, ha_s), (hb_p, hb_s)), (wo[:A_WIDTH], wo[A_WIDTH:]), g_ffn, *ffn_w,
                              g_fin, tm_big, tp, False)
            x_parts = (x_parts,)

            outs["p_ckv"].append(ckv[:tp].reshape(batch, seq, KV_RANK))
            outs["p_kr"].append(krope[:tp].reshape(batch, seq, B_ROPE))
            outs["p_C"].append(c_p[..., :A_HEAD_DIM])
            outs["p_n"].append(c_p[..., A_HEAD_DIM])
            outs["p_m"].append(m_p[:, :A_HEADS, 0])
            outs["s_ckv"].append(ckv_s)
            outs["s_kr"].append(kr_s)
            outs["s_C"].append(c_s[..., :A_HEAD_DIM])
            outs["s_n"].append(c_s[..., A_HEAD_DIM])
            outs["s_m"].append(m_s[:, :A_HEADS, 0])
        else:
            cw = C_WIDTH
            wqkv = w_qkv_c[j].astype(BF16)
            tmb = 512
            tps = seq // tmb
            ptb = tp // tmb
            assert band_buf <= tmb and seq % tmb == 0 and ts % tmb == 0 and tmb % BAND_TILE == 0
            (x,) = x_parts
            k_bf, qt_all, vt_all, kv_tail = band_proj(x, g_mix, wqkv[:, cw:2 * cw], wqkv[:, :cw].T,
                                                       wqkv[:, 2 * cw:].T, wqkv[:, cw:].T, tmb, tps, ptb, batch)
            tab = band_table(_band_base_rows(rel_bias_c[j]))
            tq = BAND_TILE
            nb = BAND_WINDOW // tq
            sub = tmb // tq
            tiles = seq // tq

            def block(g, i, jj):
                return g * tiles + jnp.maximum(i + jj - (nb - 1), 0)

            o_p = band_attn(
                qt_all, lambda g, i: ((g * tiles + i) // sub, 0, (g * tiles + i) % sub),
                [k_bf] * nb,
                [pl.BlockSpec((tq, cw), functools.partial(lambda g, i, jj: (block(g, i, jj), 0), jj=jj))
                 for jj in range(nb)],
                [vt_all] * nb,
                [pl.BlockSpec((1, cw, tq), functools.partial(
                    lambda g, i, jj: (block(g, i, jj) // sub, 0, block(g, i, jj) % sub), jj=jj)) for jj in range(nb)],
                tab, batch, tiles, True, False)

            def per_stream_t(a):
                a = a.reshape(-1, cw, tmb // dec_seq, dec_seq)
                return jnp.moveaxis(a, 2, 1).reshape(dec_batch, cw, dec_seq)

            qt_s = jnp.pad(per_stream_t(qt_all[ptb:]), ((0, 0), (0, 0), (0, tq - dec_seq)))
            k_new = jnp.pad(k_bf[tp:].reshape(dec_batch, dec_seq, cw), ((0, 0), (0, tq - dec_seq), (0, 0)))
            vt_new = jnp.pad(per_stream_t(vt_all[ptb:]), ((0, 0), (0, 0), (0, tq - dec_seq)))
            cache_spec = [pl.BlockSpec((1, 1, cw, tq), functools.partial(lambda g, i, jj: (j, g, 0, jj), jj=jj))
                          for jj in range(nb - 1)]
            o_s = band_attn(
                qt_s, lambda g, i: (g, 0, 0),
                [cache_k_t] * (nb - 1) + [k_new.reshape(-1, cw)],
                cache_spec + [pl.BlockSpec((tq, cw), lambda g, i: (g, 0))],
                [cache_v_t] * (nb - 1) + [vt_new], cache_spec + [pl.BlockSpec((1, cw, tq), lambda g, i: (g, 0, 0))],
                tab, dec_batch, 1, False, True)
            o_s = o_s.reshape(dec_batch, tq, cw)[:, :dec_seq].reshape(ts, cw)
            x_parts = mix_ffn(x_parts, ((o_p, o_s),), (w_out_c[j].astype(BF16),), g_ffn, *ffn_w, g_fin, tm_big,
                              tp, last)
            x_parts = tuple(x_parts) if last else (x_parts,)

            kv_p = kv_tail[:batch, :, tmb - band_buf:].reshape(batch, 2, C_HEADS, C_HEAD_DIM, band_buf)
            outs["p_bk"].append(jnp.transpose(kv_p[:, 0], (0, 3, 1, 2)))
            outs["p_bv"].append(jnp.transpose(kv_p[:, 1], (0, 3, 1, 2)))
            band_tails.append(kv_tail)

    y_prompt = x_parts[0].reshape(batch, seq, d)
    y_sample = x_parts[1].reshape(dec_batch, dec_seq, d)

    def frames_first(c):
        return jnp.transpose(c.reshape(-1, dec_batch, C_HEADS, C_HEAD_DIM, band_buf), (0, 1, 4, 2, 3))

    s_bk = frames_first(cache_roll(cache_k_t, band_tails, 0, batch, dec_seq))
    s_bv = frames_first(cache_roll(cache_v_t, band_tails, 1, batch, dec_seq))
    st = {k: jnp.stack(v) for k, v in outs.items() if v}
    return (y_prompt, y_sample, st["p_ckv"], st["p_kr"], st["p_C"], st["p_n"], st["p_m"], st["p_bk"], st["p_bv"],
            st["s_ckv"], st["s_kr"], st["s_C"], st["s_n"], st["s_m"], s_bk, s_bv)
```

```python
import functools
import math

import jax
import jax.numpy as jnp
from jax import lax
from jax.experimental import pallas as pl
from jax.experimental.pallas import tpu as pltpu

F32 = jnp.float32
BF16 = jnp.bfloat16

D_MODEL = 1024
CHUNK = 64
A_HEADS = 4
A_HEAD_DIM = 128
A_WIDTH = A_HEADS * A_HEAD_DIM
B_HEADS = 8
B_NOPE = 64
B_ROPE = 32
B_VDIM = 64
B_QK = B_NOPE + B_ROPE
B_WIDTH = B_HEADS * B_VDIM
Q_RANK = 384
KV_RANK = 256
ROPE_THETA = 10000.0
MLA_SCALE = B_QK ** -0.5
C_HEADS = 16
C_HEAD_DIM = 64
C_WIDTH = C_HEADS * C_HEAD_DIM
LEFT_CHUNKS = 8
MAX_REL = 128
C_SCALE = C_HEAD_DIM ** -0.5
EPS = 1e-6
NEG = -1e30
LOG2E = math.log2(math.e)

LANES = 128
MLA_HEAD_PAD = 128
MLA_KV_BLOCK = 512
MLA_Q_TILE = 512
MAX_OVERSHOOT = 64.0
MLA_V_ROWS = B_VDIM + 16
MLA_VT_ROWS = B_HEADS * MLA_V_ROWS
BAND_TILE = 256
BAND_LEFT = LEFT_CHUNKS * CHUNK
BAND_WINDOW = BAND_LEFT + BAND_TILE
BAND_ROLL_WIDTH = BAND_WINDOW + BAND_TILE
VMEM_LIMIT = 56 * 1024 * 1024
VMEM_LIMIT_FFN = 61 * 1024 * 1024

SM_CQ = 0
SM_CKV = Q_RANK
SM_KR = Q_RANK + KV_RANK
SM_WIDTH = SM_KR + LANES
GATE_I_LANE = B_ROPE
GATE_F_LANE = B_ROPE + A_HEADS


def _const_spec(shape):
    zeros = (0,) * len(shape)
    return pl.BlockSpec(shape, lambda *_: zeros, pipeline_mode=pl.Buffered(1))


def _params(semantics, vmem_limit=VMEM_LIMIT):
    return pltpu.CompilerParams(dimension_semantics=semantics, vmem_limit_bytes=vmem_limit)


def _rms(x, g):
    return x * lax.rsqrt(jnp.mean(x * x, axis=-1, keepdims=True) + EPS) * g


def _dot(a, b):
    return jnp.dot(a, b, preferred_element_type=F32)


def _dot_nt(a, b):
    return lax.dot_general(a, b, (((1,), (1,)), ((), ())), preferred_element_type=F32)


def _dot_tn(a, b):
    return lax.dot_general(a, b, (((0,), (0,)), ((), ())), preferred_element_type=F32)


def _row_specs(parts, tm, p_tiles):
    width = parts[0].shape[1]
    if len(parts) == 1:
        return [pl.BlockSpec((tm, width), lambda i: (i, 0))]
    assert parts[1].shape[0] == tm
    return [pl.BlockSpec((tm, width), lambda i: (jnp.minimum(i, p_tiles - 1), 0)),
            pl.BlockSpec((tm, width), lambda i: (0, 0), pipeline_mode=pl.Buffered(1))]


def _row_load(refs, p_tiles):
    if len(refs) == 1:
        return refs[0][...]
    return jnp.where(pl.program_id(0) < p_tiles, refs[0][...], refs[1][...])


def _norm_proj_body(*refs, n_x, p_tiles, plan):
    x_refs = refs[:n_x]
    g_ref, w_ref = refs[n_x:n_x + 2]
    out_refs = refs[n_x + 2:]
    h = _rms(_row_load(x_refs, p_tiles), g_ref[...]).astype(BF16)
    for w0, width, dests in plan:
        z = _dot(h, w_ref[:, w0:w0 + width])
        for out_idx, o0, scale in dests:
            o_ref = out_refs[out_idx]
            zz = z if scale == 1.0 else z * scale
            o_ref[:, o0:o0 + width] = zz.astype(o_ref.dtype)


def norm_proj(x_parts, g, w, plan, out_widths, out_dtypes, tm):
    t = sum(p.shape[0] for p in x_parts)
    d = x_parts[0].shape[1]
    n = w.shape[1]
    p_tiles = x_parts[0].shape[0] // tm
    return pl.pallas_call(
        functools.partial(_norm_proj_body, n_x=len(x_parts), p_tiles=p_tiles, plan=plan),
        grid=(t // tm,),
        in_specs=_row_specs(x_parts, tm, p_tiles) + [_const_spec((1, d)), _const_spec((d, n))],
        out_specs=[pl.BlockSpec((tm, ow), lambda i: (i, 0)) for ow in out_widths],
        out_shape=[jax.ShapeDtypeStruct((t, ow), dt) for ow, dt in zip(out_widths, out_dtypes)],
        compiler_params=_params(("parallel",)),
        name="norm_proj",
    )(*x_parts, g, w)


def _mix_ffn_body(*refs, n_x, mix_counts, p_tiles, ff_chunks, final):
    x_refs = refs[:n_x]
    pos = n_x
    a_groups = []
    for count in mix_counts:
        a_groups.append(refs[pos:pos + count])
        pos += count
    wo_refs = refs[pos:pos + len(mix_counts)]
    pos += len(mix_counts)
    g_ref, wg_ref, wu_ref, wd_ref = refs[pos:pos + 4]
    pos += 4
    if final:
        gf_ref = refs[pos]
        pos += 1
    out_refs = refs[pos:-1]
    act_ref = refs[-1]
    x = _row_load(x_refs, p_tiles)
    for a_refs, wo_ref in zip(a_groups, wo_refs):
        x = x + _dot(_row_load(a_refs, p_tiles), wo_ref[...])
    h = _rms(x, g_ref[...]).astype(BF16)
    for c0, cw in ff_chunks:
        gate = _dot(h, wg_ref[:, c0:c0 + cw])
        up = _dot(h, wu_ref[:, c0:c0 + cw])
        act_ref[:, c0:c0 + cw] = (gate * jax.nn.sigmoid(gate) * up).astype(BF16)
    y = x + _dot(act_ref[...], wd_ref[...])
    if final:
        y = _rms(y, gf_ref[...])
    if len(out_refs) == 1:
        out_refs[0][...] = y
    else:
        i = pl.program_id(0)

        @pl.when(i < p_tiles)
        def _():
            out_refs[0][...] = y

        @pl.when(i >= p_tiles)
        def _():
            out_refs[1][...] = y


def mix_ffn(x_parts, mix_in, mix_w, g, wg, wu, wd, g_final, tm, p_rows, split_out):
    t = sum(p.shape[0] for p in x_parts)
    d = x_parts[0].shape[1]
    f = wg.shape[1]
    p_tiles = p_rows // tm
    chunk = 512
    ff_chunks = [(c0, min(chunk, f - c0)) for c0 in range(0, f, chunk)]
    final = g_final is not None
    in_specs = _row_specs(x_parts, tm, p_tiles)
    args = list(x_parts)
    for parts in mix_in:
        in_specs += _row_specs(parts, tm, p_tiles)
        args += list(parts)
    in_specs += [_const_spec(w.shape) for w in mix_w]
    in_specs += [_const_spec((1, d)), _const_spec((d, f)), _const_spec((d, f)), _const_spec((f, d))]
    args += [*mix_w, g, wg, wu, wd]
    if final:
        in_specs.append(_const_spec((1, d)))
        args.append(g_final)
    if split_out:
        assert t - p_rows == tm
        out_specs = [pl.BlockSpec((tm, d), lambda i: (jnp.minimum(i, p_tiles - 1), 0)),
                     pl.BlockSpec((tm, d), lambda i: (0, 0), pipeline_mode=pl.Buffered(1))]
        out_shape = [jax.ShapeDtypeStruct((p_rows, d), F32), jax.ShapeDtypeStruct((t - p_rows, d), F32)]
    else:
        out_specs = pl.BlockSpec((tm, d), lambda i: (i, 0))
        out_shape = jax.ShapeDtypeStruct((t, d), F32)
    return pl.pallas_call(
        functools.partial(_mix_ffn_body, n_x=len(x_parts), mix_counts=tuple(len(p) for p in mix_in),
                          p_tiles=p_tiles, ff_chunks=ff_chunks, final=final),
        grid=(t // tm,),
        in_specs=in_specs,
        out_specs=out_specs,
        out_shape=out_shape,
        scratch_shapes=[pltpu.VMEM((tm, f), BF16)],
        compiler_params=_params(("arbitrary",), VMEM_LIMIT_FFN),
        name="mix_ffn",
    )(*args)


def _mla_prep_q_body(sm_ref, qn_ref, kvn_ref, wq_ref, rc_ref, rs1_ref, rs2_ref, cos_t_ref, sin_t_ref,
                     ckv_ref, kr_ref, qt_ref):
    sm = sm_ref[...]
    cq = _rms(sm[:, SM_CQ:SM_CQ + Q_RANK], qn_ref[...]).astype(BF16)
    ckv_ref[...] = _rms(sm[:, SM_CKV:SM_CKV + KV_RANK], kvn_ref[...])
    grp = sm[:, SM_KR:SM_KR + LANES]
    half = B_ROPE // 2
    rot = (grp * rc_ref[...] + pltpu.roll(grp, half, 1) * rs1_ref[...]
           + pltpu.roll(grp, LANES - half, 1) * rs2_ref[...])
    kr_ref[...] = rot[:, :B_ROPE]
    qt = _dot_nt(wq_ref[...], cq) * (MLA_SCALE * LOG2E)
    cos_t = cos_t_ref[...]
    sin_t = sin_t_ref[...]
    for h in range(B_HEADS):
        r0 = h * MLA_HEAD_PAD
        x1 = qt[r0 + B_NOPE:r0 + B_NOPE + half]
        x2 = qt[r0 + B_NOPE + half:r0 + B_QK]
        qt_ref[0, r0:r0 + B_NOPE, :] = qt[r0:r0 + B_NOPE].astype(BF16)
        rot_q = jnp.concatenate([x1 * cos_t - x2 * sin_t, x1 * sin_t + x2 * cos_t], axis=0)
        qt_ref[0, r0 + B_NOPE:r0 + B_QK, :] = rot_q.astype(BF16)
        qt_ref[0, r0 + B_QK:r0 + MLA_HEAD_PAD, :] = qt[r0 + B_QK:r0 + MLA_HEAD_PAD].astype(BF16)


def mla_prep_q(small, qn, kvn, wq_t, rope_tabs, tab_index, tm):
    t = small.shape[0]
    rc, rs1, rs2, cos_t, sin_t = rope_tabs
    half = B_ROPE // 2
    row_tab = pl.BlockSpec((tm, LANES), lambda i: (tab_index(i), 0))
    col_tab = pl.BlockSpec((half, tm), lambda i: (0, tab_index(i)))
    return pl.pallas_call(
        _mla_prep_q_body,
        grid=(t // tm,),
        in_specs=[pl.BlockSpec((tm, SM_WIDTH), lambda i: (i, 0)), _const_spec(qn.shape), _const_spec(kvn.shape),
                  _const_spec(wq_t.shape), row_tab, row_tab, row_tab, col_tab, col_tab],
        out_specs=[pl.BlockSpec((tm, KV_RANK), lambda i: (i, 0)), pl.BlockSpec((tm, B_ROPE), lambda i: (i, 0)),
                   pl.BlockSpec((1, B_HEADS * MLA_HEAD_PAD, tm), lambda i: (i, 0, 0))],
        out_shape=[jax.ShapeDtypeStruct((t, KV_RANK), F32), jax.ShapeDtypeStruct((t, B_ROPE), F32),
                   jax.ShapeDtypeStruct((t // tm, B_HEADS * MLA_HEAD_PAD, tm), BF16)],
        compiler_params=_params(("parallel",)),
        name="mla_prep_q",
    )(small, qn, kvn, wq_t, rc, rs1, rs2, cos_t, sin_t)


def _mla_prep_kv_body(ckv_ref, kr_ref, wk_ref, place_ref, wv_ref, kp_ref, vt_ref):
    c = ckv_ref[...].astype(BF16)
    kr = kr_ref[...].astype(BF16)
    kp_ref[...] = (_dot(c, wk_ref[...]) + _dot(kr, place_ref[...])).astype(BF16)
    vt = _dot_nt(wv_ref[...], c).astype(BF16)
    ones = jnp.ones((MLA_V_ROWS - B_VDIM, vt.shape[1]), BF16)
    for h in range(B_HEADS):
        vt_ref[0, h * MLA_V_ROWS:h * MLA_V_ROWS + B_VDIM, :] = vt[h * B_VDIM:(h + 1) * B_VDIM]
        vt_ref[0, h * MLA_V_ROWS + B_VDIM:(h + 1) * MLA_V_ROWS, :] = ones


def mla_prep_kv(ckv, krope, wk_pad, place, wv_t, n_tiles, tm):
    kw = B_HEADS * MLA_HEAD_PAD
    return pl.pallas_call(
        _mla_prep_kv_body,
        grid=(n_tiles,),
        in_specs=[pl.BlockSpec((tm, KV_RANK), lambda i: (i, 0)), pl.BlockSpec((tm, B_ROPE), lambda i: (i, 0)),
                  _const_spec(wk_pad.shape), _const_spec(place.shape), _const_spec(wv_t.shape)],
        out_specs=[pl.BlockSpec((tm, kw), lambda i: (i, 0)), pl.BlockSpec((1, MLA_VT_ROWS, tm), lambda i: (i, 0, 0))],
        out_shape=[jax.ShapeDtypeStruct((n_tiles * tm, kw), BF16),
                   jax.ShapeDtypeStruct((n_tiles, MLA_VT_ROWS, tm), BF16)],
        compiler_params=_params(("parallel",)),
        name="mla_prep_kv",
    )(ckv, krope, wk_pad, place, wv_t)


def _mla_attn_body(qt_ref, kp_ref, vt_ref, o_ref, m_ref, l_ref, acc_ref, gap_ref, *, tq, tk, q0, n_diag):
    i = pl.program_id(1)
    start = q0 + i * tq
    n_full = start // tk
    key_chunk = lax.broadcasted_iota(jnp.int32, (tk, tq), 0) // CHUNK
    qry_chunk = lax.broadcasted_iota(jnp.int32, (tk, tq), 1) // CHUNK

    def step(j, masked, stale_max):
        row0 = pl.multiple_of(j * tk, tk)
        if masked:
            visible = key_chunk + (j * tk - start) // CHUNK <= qry_chunk

        def scores(h):
            k_h = kp_ref[pl.ds(row0, tk), h * MLA_HEAD_PAD:(h + 1) * MLA_HEAD_PAD]
            q_h = qt_ref[0, h * MLA_HEAD_PAD:(h + 1) * MLA_HEAD_PAD, :]
            return _dot(k_h, q_h)

        s_next = scores(0)
        for h in range(B_HEADS):
            s = s_next
            if h + 1 < B_HEADS:
                s_next = scores(h + 1)
            if masked:
                s = jnp.where(visible, s, NEG)
            rows = slice(h * B_VDIM, (h + 1) * B_VDIM)
            m_prev = m_ref[h:h + 1, :]
            blk_max = jnp.max(s, axis=0, keepdims=True)
            m_new = jnp.maximum(m_prev, blk_max)
            alpha = jnp.exp2(m_prev - m_new)
            p = jnp.exp2(s - (m_prev if stale_max else m_new)).astype(BF16)
            pv = _dot(vt_ref[j, h * MLA_V_ROWS:(h + 1) * MLA_V_ROWS, :], p)
            if stale_max:
                gap_ref[h:h + 1, :] = jnp.maximum(gap_ref[h:h + 1, :], blk_max - m_prev)
                l_ref[h:h + 1, :] = alpha * (l_ref[h:h + 1, :] + pv[B_VDIM:B_VDIM + 1])
                acc_ref[rows, :] = alpha * (acc_ref[rows, :] + pv[:B_VDIM])
            else:
                l_ref[h:h + 1, :] = alpha * l_ref[h:h + 1, :] + pv[B_VDIM:B_VDIM + 1]
                acc_ref[rows, :] = alpha * acc_ref[rows, :] + pv[:B_VDIM]
            m_ref[h:h + 1, :] = m_new

    def sweep(stale_max):
        m_ref[...] = jnp.full(m_ref.shape, NEG, F32)
        l_ref[...] = jnp.zeros(l_ref.shape, F32)
        acc_ref[...] = jnp.zeros(acc_ref.shape, F32)
        first = 0
        if stale_max:
            @pl.when(n_full > 0)
            def _():
                step(0, False, False)

            first = 1

        def full_step(j, carry):
            step(j, False, stale_max)
            return carry

        lax.fori_loop(first, n_full, full_step, 0)
        for d in range(n_diag):
            if stale_max and d == 0:
                @pl.when(n_full > 0)
                def _():
                    step(n_full, True, True)

                @pl.when(n_full == 0)
                def _():
                    step(n_full, True, False)
            else:
                step(n_full + d, True, stale_max)

    gap_ref[...] = jnp.zeros(gap_ref.shape, F32)
    sweep(True)

    @pl.when(jnp.max(gap_ref[...]) > MAX_OVERSHOOT)
    def _():
        sweep(False)

    for h in range(B_HEADS):
        rows = slice(h * B_VDIM, (h + 1) * B_VDIM)
        acc_ref[rows, :] = acc_ref[rows, :] / l_ref[h:h + 1, :]
    o_ref[...] = jnp.transpose(acc_ref[...]).astype(o_ref.dtype)


def mla_attn(qt, kp, vt, groups, nq, tq, tk, n_blocks, q0, q_frames):
    kw = B_HEADS * MLA_HEAD_PAD
    assert q0 % tk == 0 and (tq % tk == 0 or (nq == 1 and q_frames <= tk))
    n_diag = pl.cdiv(q_frames, tk)
    return pl.pallas_call(
        functools.partial(_mla_attn_body, tq=tq, tk=tk, q0=q0, n_diag=n_diag),
        grid=(groups, nq),
        in_specs=[pl.BlockSpec((1, kw, tq), lambda g, i: (g * nq + i, 0, 0)),
                  pl.BlockSpec((n_blocks * tk, kw), lambda g, i: (g, 0), pipeline_mode=pl.Buffered(1)),
                  pl.BlockSpec((n_blocks, MLA_VT_ROWS, tk), lambda g, i: (g, 0, 0), pipeline_mode=pl.Buffered(1))],
        out_specs=pl.BlockSpec((tq, B_WIDTH), lambda g, i: (g * nq + i, 0)),
        out_shape=jax.ShapeDtypeStruct((groups * nq * tq, B_WIDTH), BF16),
        scratch_shapes=[pltpu.VMEM((B_HEADS, tq), F32), pltpu.VMEM((B_HEADS, tq), F32),
                        pltpu.VMEM((B_WIDTH, tq), F32), pltpu.VMEM((B_HEADS, tq), F32)],
        compiler_params=_params(("parallel", "arbitrary")),
        name="mla_attn",
    )(qt, kp, vt)


def _mla_sample_body(sm_ref, qn_ref, wq_ref, rc_ref, rs1_ref, rs2_ref, wabs_ref, sel_ref, cckv_ref, ckr_ref,
                     nckv_ref, nkr_ref, wuv_ref, o_ref, m_ref, l_ref, acc_ref, *, n_chunks, chunk):
    frames = sm_ref.shape[0]
    half = B_ROPE // 2
    cq = _rms(sm_ref[:, SM_CQ:SM_CQ + Q_RANK], qn_ref[...]).astype(BF16)
    qf = _dot(cq, wq_ref[...]) * (MLA_SCALE * LOG2E)
    width = qf.shape[1]
    qb = (qf * rc_ref[...] + pltpu.roll(qf, half, 1) * rs1_ref[...]
          + pltpu.roll(qf, width - half, 1) * rs2_ref[...]).astype(BF16)
    q_lat, q_rope = [], []
    for h in range(B_HEADS):
        q_h = qb[:, h * MLA_HEAD_PAD:(h + 1) * MLA_HEAD_PAD]
        q_lat.append(_dot(q_h, wabs_ref[h]).astype(BF16))
        q_rope.append(_dot(q_h, sel_ref[...]).astype(BF16))
    q_lat = jnp.concatenate(q_lat, axis=0)
    q_rope = jnp.concatenate(q_rope, axis=0)
    m_ref[...] = jnp.full(m_ref.shape, NEG, F32)
    l_ref[...] = jnp.zeros(l_ref.shape, F32)
    acc_ref[...] = jnp.zeros(acc_ref.shape, F32)

    def attend(lat, rope, rope_feature_major):
        lat = lat.astype(BF16)
        rope = rope.astype(BF16)
        s_rope = _dot(q_rope, rope) if rope_feature_major else _dot_nt(q_rope, rope)
        s = _dot_nt(q_lat, lat) + s_rope
        m_prev = m_ref[...]
        m_new = jnp.maximum(m_prev, jnp.max(s, axis=1, keepdims=True))
        alpha = jnp.exp2(m_prev - m_new)
        p = jnp.exp2(s - m_new)
        l_ref[...] = alpha * l_ref[...] + jnp.sum(p, axis=1, keepdims=True)
        acc_ref[...] = alpha * acc_ref[...] + _dot(p.astype(BF16), lat)
        m_ref[...] = m_new

    for c in range(n_chunks):
        attend(cckv_ref[0, 0, c * chunk:(c + 1) * chunk, :], ckr_ref[0, 0, :, c * chunk:(c + 1) * chunk], True)
    attend(nckv_ref[...], nkr_ref[...], False)
    o_lat = (acc_ref[...] / l_ref[...]).astype(BF16)
    out = _dot(o_lat[:frames], wuv_ref[0])
    for h in range(1, B_HEADS):
        out = out + _dot(o_lat[h * frames:(h + 1) * frames], wuv_ref[h])
    o_ref[...] = out.astype(o_ref.dtype)


def mla_sample(small, ckv, krope, cache_ckv, cache_krope_t, layer, qn, wq, rope_full, wabs, sel, wuv_place,
               streams, frames, row0):
    past = cache_ckv.shape[2]
    chunk = 512
    base = row0 // frames
    rows_q = B_HEADS * frames
    rc, rs1, rs2 = rope_full
    return pl.pallas_call(
        functools.partial(_mla_sample_body, n_chunks=past // chunk, chunk=chunk),
        grid=(streams,),
        in_specs=[pl.BlockSpec((frames, SM_WIDTH), lambda g: (base + g, 0)), _const_spec(qn.shape),
                  _const_spec(wq.shape), _const_spec(rc.shape), _const_spec(rs1.shape), _const_spec(rs2.shape),
                  _const_spec(wabs.shape), _const_spec(sel.shape),
                  pl.BlockSpec((1, 1, past, KV_RANK), lambda g: (layer, g, 0, 0)),
                  pl.BlockSpec((1, 1, B_ROPE, past), lambda g: (layer, g, 0, 0)),
                  pl.BlockSpec((frames, KV_RANK), lambda g: (base + g, 0)),
                  pl.BlockSpec((frames, B_ROPE), lambda g: (base + g, 0)),
                  _const_spec(wuv_place.shape)],
        out_specs=pl.BlockSpec((frames, B_WIDTH), lambda g: (g, 0)),
        out_shape=jax.ShapeDtypeStruct((streams * frames, B_WIDTH), BF16),
        scratch_shapes=[pltpu.VMEM((rows_q, 1), F32), pltpu.VMEM((rows_q, 1), F32),
                        pltpu.VMEM((rows_q, KV_RANK), F32)],
        compiler_params=_params(("parallel",)),
        name="mla_sample",
    )(small, qn, wq, rc, rs1, rs2, wabs, sel, cache_ckv, cache_krope_t, ckv, krope, wuv_place)


def _scan_rows(x, op, fill, length):
    row = lax.broadcasted_iota(jnp.int32, x.shape, 0)
    shift = 1
    while shift < length:
        moved = pltpu.roll(x, shift, 0)
        x = op(x, jnp.where(row >= shift, moved, fill))
        shift *= 2
    return x


def _mlstm_body(*refs, blk, per_step):
    n_in = 5 * per_step
    bias_ref, hn_ref, c0_ref, m0_ref, ha_ref, c_out_ref, m_out_ref, c_scr, m_scr = refs[n_in:]
    dh = A_HEAD_DIM
    step_idx = pl.program_id(1)

    @pl.when(step_idx == 0)
    def _():
        c_scr[...] = c0_ref[...]
        m_scr[...] = m0_ref[...]

    causal = (lax.broadcasted_iota(jnp.int32, (blk, blk), 0) >= lax.broadcasted_iota(jnp.int32, (blk, blk), 1))
    ones_col = jnp.ones((blk, dh), BF16)
    ones_mat = jnp.ones((2 * dh, dh), BF16)
    pad = max(blk, LANES) - blk
    prep = []
    for u in range(per_step):
        gt_ref = refs[5 * u + 4]
        gates = gt_ref[...] + bias_ref[...]
        log_f = jax.nn.log_sigmoid(gates)
        b_all = pltpu.roll(_scan_rows(log_f, jnp.add, 0.0, blk), LANES - A_HEADS, 1)
        a_all = gates - b_all
        amax_all = _scan_rows(a_all, jnp.maximum, NEG, blk)
        a_sq = a_all if pad == 0 else jnp.concatenate([a_all, jnp.zeros((pad, LANES), F32)], axis=0)
        prep.append((a_all, b_all, amax_all, jnp.transpose(a_sq)))
    chains = [(u, h) for u in range(per_step) for h in range(A_HEADS)]

    def cols(h):
        return slice(h * dh, (h + 1) * dh)

    scores, v_exts, run_maxes, m_prevs = [], [], [], []
    for u, h in chains:
        q_ref, k_ref, v_ref = refs[5 * u:5 * u + 3]
        scores.append(_dot_nt(q_ref[:, cols(h)], k_ref[:, cols(h)]))
        v_exts.append(jnp.concatenate([v_ref[:, cols(h)], ones_col], axis=1))
        m_prev = m_scr[u, h:h + 1, 0:1]
        lane = GATE_I_LANE + h
        m_prevs.append(m_prev)
        run_maxes.append(jnp.maximum(prep[u][2][:, lane:lane + 1], m_prev))
    probs = []
    for n, (u, h) in enumerate(chains):
        lane = GATE_I_LANE + h
        a_row = prep[u][3][lane:lane + 1, :blk]
        decay_mat = jnp.exp(jnp.where(causal, a_row - run_maxes[n], NEG))
        probs.append((scores[n] * decay_mat).astype(BF16))
    numdens = []
    for n, (u, h) in enumerate(chains):
        q_ref = refs[5 * u]
        state = c_scr[u, h]
        w_inter = jnp.exp(m_prevs[n] - run_maxes[n])
        numdens.append(w_inter * _dot(q_ref[:, cols(h)], state.astype(BF16)) + _dot(probs[n], v_exts[n]))
    for n, (u, h) in enumerate(chains):
        o_ref = refs[5 * u + 3]
        lane = GATE_I_LANE + h
        b_col = prep[u][1][:, lane:lane + 1]
        numden = numdens[n]
        den = numden[:, dh:]
        hh = numden[:, :dh] / jnp.maximum(jnp.abs(den), jnp.exp(-(b_col + run_maxes[n])))
        sq = hh * hh
        sq_hi = sq.astype(BF16)
        sq_lo = (sq - sq_hi.astype(F32)).astype(BF16)
        mean_sq = _dot(jnp.concatenate([sq_hi, sq_lo], axis=1), ones_mat) * (1.0 / dh)
        hh = hh * lax.rsqrt(mean_sq + EPS) * hn_ref[:, cols(h)]
        ha_ref[u, :, cols(h)] = (hh * jax.nn.sigmoid(o_ref[:, cols(h)])).astype(ha_ref.dtype)
    for n, (u, h) in enumerate(chains):
        k_ref = refs[5 * u + 1]
        lane = GATE_I_LANE + h
        a_col = prep[u][0][:, lane:lane + 1]
        b_col = prep[u][1][:, lane:lane + 1]
        max_last = run_maxes[n][blk - 1:blk, :]
        k_w = (k_ref[:, cols(h)].astype(F32) * jnp.exp(a_col - max_last)).astype(BF16)
        c_scr[u, h] = jnp.exp(m_prevs[n] - max_last) * c_scr[u, h] + _dot_tn(k_w, v_exts[n])
        m_scr[u, h:h + 1, :] = jnp.broadcast_to(b_col[blk - 1:blk, :] + max_last, (1, LANES))

    @pl.when(step_idx == pl.num_programs(1) - 1)
    def _():
        c_out_ref[...] = c_scr[...]
        m_out_ref[...] = m_scr[...]


def mlstm(qkv, o32, small, bias, hnorm, c0, m0, groups, steps, blk, row0, per_step):
    base = row0 // blk
    gate_block = SM_KR // LANES
    state_shape = (per_step, A_HEADS, A_HEAD_DIM, 2 * A_HEAD_DIM)

    def rows(u, col):
        return lambda g, s: (base + (g * per_step + u) * steps + s, col)

    in_specs, args = [], []
    for u in range(per_step):
        in_specs += [pl.BlockSpec((blk, A_WIDTH), rows(u, 0)), pl.BlockSpec((blk, A_WIDTH), rows(u, 1)),
                     pl.BlockSpec((blk, A_WIDTH), rows(u, 2)), pl.BlockSpec((blk, A_WIDTH), rows(u, 0)),
                     pl.BlockSpec((blk, LANES), rows(u, gate_block))]
        args += [qkv, qkv, qkv, o32, small]
    in_specs += [_const_spec((1, LANES)), _const_spec((1, A_WIDTH)),
                 pl.BlockSpec(state_shape, lambda g, s: (g, 0, 0, 0)),
                 pl.BlockSpec((per_step, 8, LANES), lambda g, s: (g, 0, 0))]
    ha, c_out, m_out = pl.pallas_call(
        functools.partial(_mlstm_body, blk=blk, per_step=per_step),
        grid=(groups // per_step, steps),
        in_specs=in_specs,
        out_specs=[pl.BlockSpec((per_step, blk, A_WIDTH), lambda g, s: (g, s, 0)),
                   pl.BlockSpec(state_shape, lambda g, s: (g, 0, 0, 0)),
                   pl.BlockSpec((per_step, 8, LANES), lambda g, s: (g, 0, 0))],
        out_shape=[jax.ShapeDtypeStruct((groups, steps * blk, A_WIDTH), BF16),
                   jax.ShapeDtypeStruct((groups, A_HEADS, A_HEAD_DIM, 2 * A_HEAD_DIM), F32),
                   jax.ShapeDtypeStruct((groups, 8, LANES), F32)],
        scratch_shapes=[pltpu.VMEM(state_shape, F32), pltpu.VMEM((per_step, 8, LANES), F32)],
        compiler_params=_params(("parallel", "arbitrary")),
        name="mlstm",
    )(*args, bias, hnorm, c0, m0)
    return ha.reshape(groups * steps * blk, A_WIDTH), c_out, m_out


def _band_proj_body(x_ref, g_ref, wk_ref, wq_t_ref, wv_t_ref, wkv_t_ref, k_ref, qt_ref, vt_ref, tail_ref,
                    *, tiles_per_seq, p_tiles):
    i = pl.program_id(0)
    h = _rms(x_ref[...], g_ref[...]).astype(BF16)
    step = 512
    for c0 in range(0, C_WIDTH, step):
        k_ref[:, c0:c0 + step] = _dot(h, wk_ref[:, c0:c0 + step]).astype(BF16)
        qt = _dot_nt(wq_t_ref[c0:c0 + step, :], h) * (C_SCALE * LOG2E)
        qt_ref[0, c0:c0 + step, :] = qt.astype(BF16)
        vt_ref[0, c0:c0 + step, :] = _dot_nt(wv_t_ref[c0:c0 + step, :], h).astype(BF16)

    is_tail = ((i + 1) % tiles_per_seq == 0) | (i >= p_tiles)

    @pl.when(is_tail)
    def _():
        for c0 in range(0, 2 * C_WIDTH, step):
            tail_ref[0, c0:c0 + step, :] = _dot_nt(wkv_t_ref[c0:c0 + step, :], h)

    @pl.when(jnp.logical_not(is_tail) & (i % tiles_per_seq == 0))
    def _():
        tail_ref[...] = jnp.zeros(tail_ref.shape, F32)


def band_proj(x, g, wk, wq_t, wv_t, wkv_t, tm, tiles_per_seq, p_tiles, n_seq):
    t, d = x.shape
    n_tiles = t // tm
    n_tail = n_seq + n_tiles - p_tiles
    cw = C_WIDTH

    def tail_index(i):
        return (jnp.where(i < p_tiles, i // tiles_per_seq, n_seq + i - p_tiles), 0, 0)

    return pl.pallas_call(
        functools.partial(_band_proj_body, tiles_per_seq=tiles_per_seq, p_tiles=p_tiles),
        grid=(n_tiles,),
        in_specs=[pl.BlockSpec((tm, d), lambda i: (i, 0)), _const_spec((1, d)), _const_spec((d, cw)),
                  _const_spec((cw, d)), _const_spec((cw, d)), _const_spec((2 * cw, d))],
        out_specs=[pl.BlockSpec((tm, cw), lambda i: (i, 0)), pl.BlockSpec((1, cw, tm), lambda i: (i, 0, 0)),
                   pl.BlockSpec((1, cw, tm), lambda i: (i, 0, 0)), pl.BlockSpec((1, 2 * cw, tm), tail_index)],
        out_shape=[jax.ShapeDtypeStruct((t, cw), BF16), jax.ShapeDtypeStruct((n_tiles, cw, tm), BF16),
                   jax.ShapeDtypeStruct((n_tiles, cw, tm), BF16), jax.ShapeDtypeStruct((n_tail, 2 * cw, tm), F32)],
        compiler_params=_params(("arbitrary",)),
        name="band_proj",
    )(x, g, wk, wq_t, wv_t, wkv_t)


def _band_table_body(base_ref, tab_ref):
    tq = BAND_TILE
    shape = (tq, BAND_ROLL_WIDTH)
    row = lax.broadcasted_iota(jnp.int32, shape, 0)
    tab = jnp.broadcast_to(base_ref[0], shape)
    shift = 1
    while shift < tq:
        tab = jnp.where((row & shift) != 0, pltpu.roll(tab, shift, 1), tab)
        shift *= 2
    win = lax.broadcasted_iota(jnp.int32, (tq, BAND_WINDOW), 1) // CHUNK
    qch = lax.broadcasted_iota(jnp.int32, (tq, BAND_WINDOW), 0) // CHUNK
    valid = (win >= qch) & (win <= qch + LEFT_CHUNKS)
    tab_ref[0] = jnp.transpose(jnp.where(valid, tab[:, :BAND_WINDOW] * LOG2E, NEG))


def band_table(base_rows):
    return pl.pallas_call(
        _band_table_body,
        grid=(C_HEADS,),
        in_specs=[pl.BlockSpec((1, 1, BAND_ROLL_WIDTH), lambda h: (h, 0, 0))],
        out_specs=pl.BlockSpec((1, BAND_WINDOW, BAND_TILE), lambda h: (h, 0, 0)),
        out_shape=jax.ShapeDtypeStruct((C_HEADS, BAND_WINDOW, BAND_TILE), F32),
        compiler_params=_params(("parallel",)),
        name="band_table",
    )(base_rows)


def _band_body(qt_ref, *refs, clamp_start, cache_values):
    nb = BAND_WINDOW // BAND_TILE
    k_refs = refs[:nb]
    v_refs = refs[nb:2 * nb]
    tab_ref, o_ref, acc_ref = refs[2 * nb:]
    tq = BAND_TILE
    i = pl.program_id(1)
    row_half = lax.broadcasted_iota(jnp.int32, (LANES, tq), 0) // C_HEAD_DIM
    ones = jnp.ones((16, tq), BF16)

    def value_rows(j, rows):
        if cache_values and j < nb - 1:
            v = v_refs[j][0, 0, rows, :].astype(BF16)
        else:
            v = v_refs[j][0, rows, :]
        return jnp.concatenate([v, ones], axis=0)

    def scores(h, first_tiles):
        pair, half = divmod(h, 2)
        q_pair = qt_ref[0, pair * LANES:(pair + 1) * LANES, :]
        q_h = jnp.where(row_half == half, q_pair, jnp.zeros_like(q_pair))
        parts = []
        for j in range(nb):
            if cache_values and j < nb - 1:
                k_t = k_refs[j][0, 0, pair * LANES:(pair + 1) * LANES, :]
                s_j = _dot_tn(k_t.astype(BF16), q_h)
            else:
                s_j = _dot(k_refs[j][:, pair * LANES:(pair + 1) * LANES], q_h)
            if first_tiles and j < nb - 1:
                s_j = jnp.where(i + j < nb - 1, NEG, s_j)
            parts.append(s_j + tab_ref[h, j * BAND_TILE:(j + 1) * BAND_TILE, :])
        return parts

    def attend(h, parts, m):
        rows = slice(h * C_HEAD_DIM, (h + 1) * C_HEAD_DIM)
        pv = None
        for j in range(nb):
            part = _dot(value_rows(j, rows), jnp.exp2(parts[j] - m).astype(BF16))
            pv = part if pv is None else pv + part
        acc_ref[rows, :] = pv[:C_HEAD_DIM] / pv[C_HEAD_DIM:C_HEAD_DIM + 1]

    def run(first_tiles):
        s_next = scores(0, first_tiles)
        for h in range(C_HEADS):
            parts = s_next
            if h + 1 < C_HEADS:
                s_next = scores(h + 1, first_tiles)
            m = parts[0]
            for j in range(1, nb):
                m = jnp.maximum(m, parts[j])
            attend(h, parts, jnp.max(m, axis=0, keepdims=True))

    if clamp_start:
        @pl.when(i < nb - 1)
        def _():
            run(True)

        @pl.when(i >= nb - 1)
        def _():
            run(False)
    else:
        run(False)
    o_ref[...] = jnp.transpose(acc_ref[...]).astype(o_ref.dtype)


def band_attn(qt_arr, qt_map, k_arrs, k_specs, v_arrs, v_specs, tab, groups, tiles, clamp_start, cache_values):
    tq = BAND_TILE
    in_specs = [pl.BlockSpec((1, C_WIDTH, tq), qt_map)]
    in_specs += list(k_specs)
    in_specs += list(v_specs)
    in_specs.append(_const_spec(tab.shape))
    return pl.pallas_call(
        functools.partial(_band_body, clamp_start=clamp_start, cache_values=cache_values),
        grid=(groups, tiles),
        in_specs=in_specs,
        out_specs=pl.BlockSpec((tq, C_WIDTH), lambda g, i: (g * tiles + i, 0)),
        out_shape=jax.ShapeDtypeStruct((groups * tiles * tq, C_WIDTH), BF16),
        scratch_shapes=[pltpu.VMEM((C_WIDTH, tq), F32)],
        compiler_params=_params(("parallel", "arbitrary")),
        name="band_attn",
    )(qt_arr, *k_arrs, *v_arrs, tab)


def _cache_roll_body(c_ref, *refs, new_frames):
    tail_refs = refs[:-1]
    o_ref = refs[-1]
    layer = pl.program_id(0)
    stream = pl.program_id(1)
    buf = c_ref.shape[3]
    rolled = pltpu.roll(c_ref[0, 0], buf - new_frames, 1)
    new = tail_refs[0][0]
    for n in range(1, len(tail_refs)):
        new = jnp.where(layer == n, tail_refs[n][0], new)
    upper_half = (stream % (LANES // new_frames)) == 1
    new = jnp.where(upper_half, new, pltpu.roll(new, new_frames, 1))
    o_ref[0, 0] = rolled
    lane = lax.broadcasted_iota(jnp.int32, new.shape, 1)
    o_ref[0, 0, :, buf - LANES:] = jnp.where(lane < LANES - new_frames, rolled[:, buf - LANES:], new)


def cache_roll(cache_t, tails, row_block, first_tile, new_frames):
    layers, streams, feat, buf = cache_t.shape
    tm = tails[0].shape[2]
    per_tile = tm // new_frames
    groups = LANES // new_frames
    assert 2 * new_frames == LANES and layers == len(tails)

    def tail_map(l, s):
        return (first_tile + s // per_tile, row_block, (s % per_tile) // groups)

    return pl.pallas_call(
        functools.partial(_cache_roll_body, new_frames=new_frames),
        grid=(layers, streams),
        in_specs=[pl.BlockSpec((1, 1, feat, buf), lambda l, s: (l, s, 0, 0))]
                 + [pl.BlockSpec((1, feat, LANES), tail_map)] * layers,
        out_specs=pl.BlockSpec((1, 1, feat, buf), lambda l, s: (l, s, 0, 0)),
        out_shape=jax.ShapeDtypeStruct(cache_t.shape, cache_t.dtype),
        compiler_params=_params(("parallel", "parallel")),
        name="cache_roll",
    )(cache_t, *tails)


def _rope_tables(positions):
    half = B_ROPE // 2
    inv = ROPE_THETA ** (-jnp.arange(half, dtype=F32) / half)
    ang = positions.astype(F32)[:, None] * inv[None, :]
    cos, sin = jnp.cos(ang), jnp.sin(ang)
    n = positions.shape[0]
    zeros = jnp.zeros((n, LANES - B_ROPE), F32)
    zh = jnp.zeros((n, half), F32)
    rc = jnp.concatenate([cos, cos, zeros], axis=1)
    rs1 = jnp.concatenate([zh, sin, zeros], axis=1)
    rs2 = jnp.concatenate([-sin, zh, zeros], axis=1)
    return rc, rs1, rs2, cos.T, sin.T


def _band_base_rows(rel_bias):
    x = jnp.arange(BAND_ROLL_WIDTH)
    rel = jnp.where(x < BAND_WINDOW, BAND_LEFT - x, BAND_LEFT + 1)
    idx = jnp.clip(rel, -MAX_REL, MAX_REL) + MAX_REL
    return rel_bias[:, None, idx]


def kernel(x_prompt, x_sample, cache_mla_ckv, cache_mla_krope, state_mlstm_C, state_mlstm_n, state_mlstm_m,
           cache_band_k, cache_band_v, norm_mix, norm_ffn, norm_final, w_in_ab, b_gates, mlstm_hnorm,
           mla_q_norm, mla_kv_norm, mla_w_uq, mla_w_uk, mla_w_uv, w_out_ab, w_qkv_c, w_out_c, rel_bias_c,
           w_gate, w_up, w_down):
    batch, seq, d = x_prompt.shape
    dec_batch, dec_seq, _ = x_sample.shape
    past = cache_mla_ckv.shape[2]
    band_buf = cache_band_k.shape[2]
    depth = norm_mix.shape[0]
    tp = batch * seq
    ts = dec_batch * dec_seq
    t = tp + ts
    tm_big = 1024
    tm = MLA_Q_TILE
    tk = MLA_KV_BLOCK
    assert d == D_MODEL and dec_seq == CHUNK and band_buf == BAND_LEFT and past % tk == 0
    assert tp % tm_big == 0 and ts % tm_big == 0 and seq % tm == 0 and tm % dec_seq == 0 and tm % tk == 0

    assert past % CHUNK == 0 and ts == tm_big
    x_parts = (x_prompt.reshape(tp, d), x_sample.reshape(ts, d))

    pos_tab = jnp.concatenate([jnp.arange(seq, dtype=jnp.int32),
                               past + (jnp.arange(tm, dtype=jnp.int32) % dec_seq)])
    rope_tabs = _rope_tables(pos_tab)
    head_pat = jnp.concatenate([jnp.ones((dec_seq, B_NOPE), F32), jnp.zeros((dec_seq, LANES - B_NOPE), F32)], axis=1)
    roll_pad = ((0, 0), (B_NOPE, LANES - B_QK))
    rope_full = tuple(jnp.tile(tab, (1, B_HEADS)) for tab in (
        head_pat + jnp.pad(rope_tabs[0][seq:seq + dec_seq, :B_ROPE], roll_pad),
        jnp.pad(rope_tabs[1][seq:seq + dec_seq, :B_ROPE], roll_pad),
        jnp.pad(rope_tabs[2][seq:seq + dec_seq, :B_ROPE], roll_pad)))
    sel = jnp.pad(jnp.eye(B_ROPE, dtype=F32), ((B_NOPE, MLA_HEAD_PAD - B_QK), (0, 0))).astype(BF16)
    cache_krope_t = jnp.swapaxes(cache_mla_krope, 2, 3)

    def frames_last(c):
        return jnp.transpose(c, (0, 1, 3, 4, 2)).reshape(c.shape[0], dec_batch, C_WIDTH, band_buf)

    cache_k_t = frames_last(cache_band_k)
    cache_v_t = frames_last(cache_band_v)
    band_tails = []
    p_tiles = tp // tm
    seq_tiles = seq // tm

    def tab_index(i):
        return jnp.where(i < p_tiles, i % seq_tiles, seq_tiles)

    outs = {k: [] for k in ("p_ckv", "p_kr", "p_C", "p_n", "p_m", "p_bk", "p_bv",
                            "s_ckv", "s_kr", "s_C", "s_n", "s_m", "s_bk", "s_bv")}
    a4 = 4 * A_WIDTH
    for layer in range(depth):
        j = layer // 2
        g_mix = norm_mix[layer][None, :]
        g_ffn = norm_ffn[layer][None, :]
        last = layer == depth - 1
        g_fin = norm_final[None, :] if last else None
        ffn_w = (w_gate[layer].astype(BF16), w_up[layer].astype(BF16), w_down[layer].astype(BF16))
        if layer % 2 == 0:
            w = w_in_ab[j]
            gate_cols = jnp.concatenate([w[:, a4 + 2 * A_HEADS + Q_RANK + KV_RANK:], w[:, a4:a4 + 2 * A_HEADS],
                                         jnp.zeros((d, LANES - B_ROPE - 2 * A_HEADS), F32)], axis=1)
            w_all = jnp.concatenate([w[:, :a4], w[:, a4 + 2 * A_HEADS:a4 + 2 * A_HEADS + Q_RANK + KV_RANK],
                                     gate_cols], axis=1).astype(BF16)
            aw = A_WIDTH
            plan = [(0, aw, [(0, 0, 1.0)]), (aw, aw, [(0, aw, A_HEAD_DIM ** -0.5)]), (2 * aw, aw, [(0, 2 * aw, 1.0)]),
                    (3 * aw, aw, [(1, 0, 1.0)]), (4 * aw, Q_RANK, [(2, SM_CQ, 1.0)]),
                    (4 * aw + Q_RANK, KV_RANK + LANES, [(2, SM_CKV, 1.0)])]
            qkv, o32, small = norm_proj(x_parts, g_mix, w_all, plan, (3 * aw, aw, SM_WIDTH), (BF16, F32, F32), tm_big)

            wq = mla_w_uq[j].reshape(Q_RANK, B_HEADS, B_QK)
            wq_t = jnp.pad(wq, ((0, 0), (0, 0), (0, MLA_HEAD_PAD - B_QK))).reshape(Q_RANK, -1).T.astype(BF16)
            ckv, krope, qt = mla_prep_q(small, mla_q_norm[j][None, :], mla_kv_norm[j][None, :], wq_t,
                                        rope_tabs, tab_index, tm)
            wk_pad = jnp.pad(mla_w_uk[j], ((0, 0), (0, 0), (0, MLA_HEAD_PAD - B_NOPE))).reshape(KV_RANK, -1).astype(BF16)
            place = jnp.pad(jnp.eye(B_ROPE, dtype=F32), ((0, 0), (B_NOPE, MLA_HEAD_PAD - B_QK)))
            place = jnp.tile(place, (1, B_HEADS)).astype(BF16)
            wv_t = mla_w_uv[j].reshape(KV_RANK, B_WIDTH).T.astype(BF16)
            kp_p, vt_p = mla_prep_kv(ckv, krope, wk_pad, place, wv_t, tp // tk, tk)
            ckv_s = ckv[tp:].reshape(dec_batch, dec_seq, KV_RANK)
            kr_s = krope[tp:].reshape(dec_batch, dec_seq, B_ROPE)

            hb_p = mla_attn(qt, kp_p, vt_p, batch, seq_tiles, tm, tk, seq // tk, 0, tm)
            wabs = jnp.pad(jnp.transpose(mla_w_uk[j], (1, 2, 0)),
                           ((0, 0), (0, MLA_HEAD_PAD - B_NOPE), (0, 0))).astype(BF16)
            wuv_place = jnp.einsum("rhv,hg->hrgv", mla_w_uv[j], jnp.eye(B_HEADS, dtype=F32))
            wuv_place = wuv_place.reshape(B_HEADS, KV_RANK, B_WIDTH).astype(BF16)
            hb_s = mla_sample(small, ckv, krope, cache_mla_ckv, cache_krope_t, j, mla_q_norm[j][None, :],
                              wq_t.T, rope_full, wabs, sel, wuv_place, dec_batch, dec_seq, tp)

            bias = jnp.zeros((1, LANES), F32).at[0, GATE_I_LANE:GATE_I_LANE + 2 * A_HEADS].set(b_gates[j])
            hn = mlstm_hnorm[j][None, :]
            c0_p = jnp.zeros((batch, A_HEADS, A_HEAD_DIM, 2 * A_HEAD_DIM), F32)
            m0_p = jnp.zeros((batch, 8, LANES), F32)
            blk_p = 256
            ha_p, c_p, m_p = mlstm(qkv, o32, small, bias, hn, c0_p, m0_p, batch, seq // blk_p, blk_p, 0, batch)
            n_rep = jnp.broadcast_to(state_mlstm_n[j][..., None], (dec_batch, A_HEADS, A_HEAD_DIM, A_HEAD_DIM))
            c0_s = jnp.concatenate([state_mlstm_C[j], n_rep], axis=-1)
            m0_s = jnp.broadcast_to(jnp.pad(state_mlstm_m[j], ((0, 0), (0, 8 - A_HEADS)))[..., None],
                                    (dec_batch, 8, LANES))
            ha_s, c_s, m_s = mlstm(qkv, o32, small, bias, hn, c0_s, m0_s, dec_batch, 1, dec_seq, tp, 2)

            wo = w_out_ab[j].astype(BF16)
            x_parts = mix_ffn(x_parts, ((ha_p, ha_s), (hb_p, hb_s)), (wo[:A_WIDTH], wo[A_WIDTH:]), g_ffn, *ffn_w,
                              g_fin, tm_big, tp, False)
            x_parts = (x_parts,)

            outs["p_ckv"].append(ckv[:tp].reshape(batch, seq, KV_RANK))
            outs["p_kr"].append(krope[:tp].reshape(batch, seq, B_ROPE))
            outs["p_C"].append(c_p[..., :A_HEAD_DIM])
            outs["p_n"].append(c_p[..., A_HEAD_DIM])
            outs["p_m"].append(m_p[:, :A_HEADS, 0])
            outs["s_ckv"].append(ckv_s)
            outs["s_kr"].append(kr_s)
            outs["s_C"].append(c_s[..., :A_HEAD_DIM])
            outs["s_n"].append(c_s[..., A_HEAD_DIM])
            outs["s_m"].append(m_s[:, :A_HEADS, 0])
        else:
            cw = C_WIDTH
            wqkv = w_qkv_c[j].astype(BF16)
            tmb = 512
            tps = seq // tmb
            ptb = tp // tmb
            assert band_buf <= tmb and seq % tmb == 0 and ts % tmb == 0 and tmb % BAND_TILE == 0
            (x,) = x_parts
            k_bf, qt_all, vt_all, kv_tail = band_proj(x, g_mix, wqkv[:, cw:2 * cw], wqkv[:, :cw].T,
                                                       wqkv[:, 2 * cw:].T, wqkv[:, cw:].T, tmb, tps, ptb, batch)
            tab = band_table(_band_base_rows(rel_bias_c[j]))
            tq = BAND_TILE
            nb = BAND_WINDOW // tq
            sub = tmb // tq
            tiles = seq // tq

            def block(g, i, jj):
                return g * tiles + jnp.maximum(i + jj - (nb - 1), 0)

            o_p = band_attn(
                qt_all, lambda g, i: ((g * tiles + i) // sub, 0, (g * tiles + i) % sub),
                [k_bf] * nb,
                [pl.BlockSpec((tq, cw), functools.partial(lambda g, i, jj: (block(g, i, jj), 0), jj=jj))
                 for jj in range(nb)],
                [vt_all] * nb,
                [pl.BlockSpec((1, cw, tq), functools.partial(
                    lambda g, i, jj: (block(g, i, jj) // sub, 0, block(g, i, jj) % sub), jj=jj)) for jj in range(nb)],
                tab, batch, tiles, True, False)

            def per_stream_t(a):
                a = a.reshape(-1, cw, tmb // dec_seq, dec_seq)
                return jnp.moveaxis(a, 2, 1).reshape(dec_batch, cw, dec_seq)

            qt_s = jnp.pad(per_stream_t(qt_all[ptb:]), ((0, 0), (0, 0), (0, tq - dec_seq)))
            k_new = jnp.pad(k_bf[tp:].reshape(dec_batch, dec_seq, cw), ((0, 0), (0, tq - dec_seq), (0, 0)))
            vt_new = jnp.pad(per_stream_t(vt_all[ptb:]), ((0, 0), (0, 0), (0, tq - dec_seq)))
            cache_spec = [pl.BlockSpec((1, 1, cw, tq), functools.partial(lambda g, i, jj: (j, g, 0, jj), jj=jj))
                          for jj in range(nb - 1)]
            o_s = band_attn(
                qt_s, lambda g, i: (g, 0, 0),
                [cache_k_t] * (nb - 1) + [k_new.reshape(-1, cw)],
                cache_spec + [pl.BlockSpec((tq, cw), lambda g, i: (g, 0))],
                [cache_v_t] * (nb - 1) + [vt_new], cache_spec + [pl.BlockSpec((1, cw, tq), lambda g, i: (g, 0, 0))],
                tab, dec_batch, 1, False, True)
            o_s = o_s.reshape(dec_batch, tq, cw)[:, :dec_seq].reshape(ts, cw)
            x_parts = mix_ffn(x_parts, ((o_p, o_s),), (w_out_c[j].astype(BF16),), g_ffn, *ffn_w, g_fin, tm_big,
                              tp, last)
            x_parts = tuple(x_parts) if last else (x_parts,)

            kv_p = kv_tail[:batch, :, tmb - band_buf:].reshape(batch, 2, C_HEADS, C_HEAD_DIM, band_buf)
            outs["p_bk"].append(jnp.transpose(kv_p[:, 0], (0, 3, 1, 2)))
            outs["p_bv"].append(jnp.transpose(kv_p[:, 1], (0, 3, 1, 2)))
            band_tails.append(kv_tail)

    y_prompt = x_parts[0].reshape(batch, seq, d)
    y_sample = x_parts[1].reshape(dec_batch, dec_seq, d)

    def frames_first(c):
        return jnp.transpose(c.reshape(-1, dec_batch, C_HEADS, C_HEAD_DIM, band_buf), (0, 1, 4, 2, 3))

    s_bk = frames_first(cache_roll(cache_k_t, band_tails, 0, batch, dec_seq))
    s_bv = frames_first(cache_roll(cache_v_t, band_tails, 1, batch, dec_seq))
    st = {k: jnp.stack(v) for k, v in outs.items() if v}
    return (y_prompt, y_sample, st["p_ckv"], st["p_kr"], st["p_C"], st["p_n"], st["p_m"], st["p_bk"], st["p_bv"],
            st["s_ckv"], st["s_kr"], st["s_C"], st["s_n"], st["s_m"], s_bk, s_bv)
```

```python
import functools
import math

import jax
import jax.numpy as jnp
from jax import lax
from jax.experimental import pallas as pl
from jax.experimental.pallas import tpu as pltpu

F32 = jnp.float32
BF16 = jnp.bfloat16

D_MODEL = 1024
CHUNK = 64
A_HEADS = 4
A_HEAD_DIM = 128
A_WIDTH = A_HEADS * A_HEAD_DIM
B_HEADS = 8
B_NOPE = 64
B_ROPE = 32
B_VDIM = 64
B_QK = B_NOPE + B_ROPE
B_WIDTH = B_HEADS * B_VDIM
Q_RANK = 384
KV_RANK = 256
ROPE_THETA = 10000.0
MLA_SCALE = B_QK ** -0.5
C_HEADS = 16
C_HEAD_DIM = 64
C_WIDTH = C_HEADS * C_HEAD_DIM
LEFT_CHUNKS = 8
MAX_REL = 128
C_SCALE = C_HEAD_DIM ** -0.5
EPS = 1e-6
NEG = -1e30
LOG2E = math.log2(math.e)

LANES = 128
MLA_HEAD_PAD = 128
MLA_KV_BLOCK = 512
MLA_Q_TILE = 512
MAX_OVERSHOOT = 64.0
MLA_V_ROWS = B_VDIM + 16
MLA_VT_ROWS = B_HEADS * MLA_V_ROWS
BAND_TILE = 256
BAND_LEFT = LEFT_CHUNKS * CHUNK
BAND_WINDOW = BAND_LEFT + BAND_TILE
BAND_ROLL_WIDTH = BAND_WINDOW + BAND_TILE
VMEM_LIMIT = 56 * 1024 * 1024
VMEM_LIMIT_FFN = 61 * 1024 * 1024

SM_CQ = 0
SM_CKV = Q_RANK
SM_KR = Q_RANK + KV_RANK
SM_WIDTH = SM_KR + LANES
GATE_I_LANE = B_ROPE
GATE_F_LANE = B_ROPE + A_HEADS


def _const_spec(shape):
    zeros = (0,) * len(shape)
    return pl.BlockSpec(shape, lambda *_: zeros, pipeline_mode=pl.Buffered(1))


def _params(semantics, vmem_limit=VMEM_LIMIT):
    return pltpu.CompilerParams(dimension_semantics=semantics, vmem_limit_bytes=vmem_limit)


def _rms(x, g):
    return x * lax.rsqrt(jnp.mean(x * x, axis=-1, keepdims=True) + EPS) * g


def _dot(a, b):
    return jnp.dot(a, b, preferred_element_type=F32)


def _dot_nt(a, b):
    return lax.dot_general(a, b, (((1,), (1,)), ((), ())), preferred_element_type=F32)


def _dot_tn(a, b):
    return lax.dot_general(a, b, (((0,), (0,)), ((), ())), preferred_element_type=F32)


def _row_specs(parts, tm, p_tiles):
    width = parts[0].shape[1]
    if len(parts) == 1:
        return [pl.BlockSpec((tm, width), lambda i: (i, 0))]
    assert parts[1].shape[0] == tm
    return [pl.BlockSpec((tm, width), lambda i: (jnp.minimum(i, p_tiles - 1), 0)),
            pl.BlockSpec((tm, width), lambda i: (0, 0), pipeline_mode=pl.Buffered(1))]


def _row_load(refs, p_tiles):
    if len(refs) == 1:
        return refs[0][...]
    return jnp.where(pl.program_id(0) < p_tiles, refs[0][...], refs[1][...])


def _norm_proj_body(*refs, n_x, p_tiles, plan):
    x_refs = refs[:n_x]
    g_ref, w_ref = refs[n_x:n_x + 2]
    out_refs = refs[n_x + 2:]
    h = _rms(_row_load(x_refs, p_tiles), g_ref[...]).astype(BF16)
    for w0, width, dests in plan:
        z = _dot(h, w_ref[:, w0:w0 + width])
        for out_idx, o0, scale in dests:
            o_ref = out_refs[out_idx]
            zz = z if scale == 1.0 else z * scale
            o_ref[:, o0:o0 + width] = zz.astype(o_ref.dtype)


def norm_proj(x_parts, g, w, plan, out_widths, out_dtypes, tm):
    t = sum(p.shape[0] for p in x_parts)
    d = x_parts[0].shape[1]
    n = w.shape[1]
    p_tiles = x_parts[0].shape[0] // tm
    return pl.pallas_call(
        functools.partial(_norm_proj_body, n_x=len(x_parts), p_tiles=p_tiles, plan=plan),
        grid=(t // tm,),
        in_specs=_row_specs(x_parts, tm, p_tiles) + [_const_spec((1, d)), _const_spec((d, n))],
        out_specs=[pl.BlockSpec((tm, ow), lambda i: (i, 0)) for ow in out_widths],
        out_shape=[jax.ShapeDtypeStruct((t, ow), dt) for ow, dt in zip(out_widths, out_dtypes)],
        compiler_params=_params(("parallel",)),
        name="norm_proj",
    )(*x_parts, g, w)


def _mix_ffn_body(*refs, n_x, mix_counts, p_tiles, ff_chunks, final):
    x_refs = refs[:n_x]
    pos = n_x
    a_groups = []
    for count in mix_counts:
        a_groups.append(refs[pos:pos + count])
        pos += count
    wo_refs = refs[pos:pos + len(mix_counts)]
    pos += len(mix_counts)
    g_ref, wg_ref, wu_ref, wd_ref = refs[pos:pos + 4]
    pos += 4
    if final:
        gf_ref = refs[pos]
        pos += 1
    out_refs = refs[pos:-1]
    act_ref = refs[-1]
    x = _row_load(x_refs, p_tiles)
    for a_refs, wo_ref in zip(a_groups, wo_refs):
        x = x + _dot(_row_load(a_refs, p_tiles), wo_ref[...])
    h = _rms(x, g_ref[...]).astype(BF16)
    for c0, cw in ff_chunks:
        gate = _dot(h, wg_ref[:, c0:c0 + cw])
        up = _dot(h, wu_ref[:, c0:c0 + cw])
        act_ref[:, c0:c0 + cw] = (gate * jax.nn.sigmoid(gate) * up).astype(BF16)
    y = x + _dot(act_ref[...], wd_ref[...])
    if final:
        y = _rms(y, gf_ref[...])
    if len(out_refs) == 1:
        out_refs[0][...] = y
    else:
        i = pl.program_id(0)

        @pl.when(i < p_tiles)
        def _():
            out_refs[0][...] = y

        @pl.when(i >= p_tiles)
        def _():
            out_refs[1][...] = y


def mix_ffn(x_parts, mix_in, mix_w, g, wg, wu, wd, g_final, tm, p_rows, split_out):
    t = sum(p.shape[0] for p in x_parts)
    d = x_parts[0].shape[1]
    f = wg.shape[1]
    p_tiles = p_rows // tm
    chunk = 512
    ff_chunks = [(c0, min(chunk, f - c0)) for c0 in range(0, f, chunk)]
    final = g_final is not None
    in_specs = _row_specs(x_parts, tm, p_tiles)
    args = list(x_parts)
    for parts in mix_in:
        in_specs += _row_specs(parts, tm, p_tiles)
        args += list(parts)
    in_specs += [_const_spec(w.shape) for w in mix_w]
    in_specs += [_const_spec((1, d)), _const_spec((d, f)), _const_spec((d, f)), _const_spec((f, d))]
    args += [*mix_w, g, wg, wu, wd]
    if final:
        in_specs.append(_const_spec((1, d)))
        args.append(g_final)
    if split_out:
        assert t - p_rows == tm
        out_specs = [pl.BlockSpec((tm, d), lambda i: (jnp.minimum(i, p_tiles - 1), 0)),
                     pl.BlockSpec((tm, d), lambda i: (0, 0), pipeline_mode=pl.Buffered(1))]
        out_shape = [jax.ShapeDtypeStruct((p_rows, d), F32), jax.ShapeDtypeStruct((t - p_rows, d), F32)]
    else:
        out_specs = pl.BlockSpec((tm, d), lambda i: (i, 0))
        out_shape = jax.ShapeDtypeStruct((t, d), F32)
    return pl.pallas_call(
        functools.partial(_mix_ffn_body, n_x=len(x_parts), mix_counts=tuple(len(p) for p in mix_in),
                          p_tiles=p_tiles, ff_chunks=ff_chunks, final=final),
        grid=(t // tm,),
        in_specs=in_specs,
        out_specs=out_specs,
        out_shape=out_shape,
        scratch_shapes=[pltpu.VMEM((tm, f), BF16)],
        compiler_params=_params(("arbitrary",), VMEM_LIMIT_FFN),
        name="mix_ffn",
    )(*args)


def _mla_prep_q_body(sm_ref, qn_ref, kvn_ref, wq_ref, rc_ref, rs1_ref, rs2_ref, cos_t_ref, sin_t_ref,
                     ckv_ref, kr_ref, qt_ref):
    sm = sm_ref[...]
    cq = _rms(sm[:, SM_CQ:SM_CQ + Q_RANK], qn_ref[...]).astype(BF16)
    ckv_ref[...] = _rms(sm[:, SM_CKV:SM_CKV + KV_RANK], kvn_ref[...])
    grp = sm[:, SM_KR:SM_KR + LANES]
    half = B_ROPE // 2
    rot = (grp * rc_ref[...] + pltpu.roll(grp, half, 1) * rs1_ref[...]
           + pltpu.roll(grp, LANES - half, 1) * rs2_ref[...])
    kr_ref[...] = rot[:, :B_ROPE]
    qt = _dot_nt(wq_ref[...], cq) * (MLA_SCALE * LOG2E)
    cos_t = cos_t_ref[...]
    sin_t = sin_t_ref[...]
    for h in range(B_HEADS):
        r0 = h * MLA_HEAD_PAD
        x1 = qt[r0 + B_NOPE:r0 + B_NOPE + half]
        x2 = qt[r0 + B_NOPE + half:r0 + B_QK]
        qt_ref[0, r0:r0 + B_NOPE, :] = qt[r0:r0 + B_NOPE].astype(BF16)
        rot_q = jnp.concatenate([x1 * cos_t - x2 * sin_t, x1 * sin_t + x2 * cos_t], axis=0)
        qt_ref[0, r0 + B_NOPE:r0 + B_QK, :] = rot_q.astype(BF16)
        qt_ref[0, r0 + B_QK:r0 + MLA_HEAD_PAD, :] = qt[r0 + B_QK:r0 + MLA_HEAD_PAD].astype(BF16)


def mla_prep_q(small, qn, kvn, wq_t, rope_tabs, tab_index, tm):
    t = small.shape[0]
    rc, rs1, rs2, cos_t, sin_t = rope_tabs
    half = B_ROPE // 2
    row_tab = pl.BlockSpec((tm, LANES), lambda i: (tab_index(i), 0))
    col_tab = pl.BlockSpec((half, tm), lambda i: (0, tab_index(i)))
    return pl.pallas_call(
        _mla_prep_q_body,
        grid=(t // tm,),
        in_specs=[pl.BlockSpec((tm, SM_WIDTH), lambda i: (i, 0)), _const_spec(qn.shape), _const_spec(kvn.shape),
                  _const_spec(wq_t.shape), row_tab, row_tab, row_tab, col_tab, col_tab],
        out_specs=[pl.BlockSpec((tm, KV_RANK), lambda i: (i, 0)), pl.BlockSpec((tm, B_ROPE), lambda i: (i, 0)),
                   pl.BlockSpec((1, B_HEADS * MLA_HEAD_PAD, tm), lambda i: (i, 0, 0))],
        out_shape=[jax.ShapeDtypeStruct((t, KV_RANK), F32), jax.ShapeDtypeStruct((t, B_ROPE), F32),
                   jax.ShapeDtypeStruct((t // tm, B_HEADS * MLA_HEAD_PAD, tm), BF16)],
        compiler_params=_params(("parallel",)),
        name="mla_prep_q",
    )(small, qn, kvn, wq_t, rc, rs1, rs2, cos_t, sin_t)


def _mla_prep_kv_body(ckv_ref, kr_ref, wk_ref, place_ref, wv_ref, kp_ref, vt_ref):
    c = ckv_ref[...].astype(BF16)
    kr = kr_ref[...].astype(BF16)
    kp_ref[...] = (_dot(c, wk_ref[...]) + _dot(kr, place_ref[...])).astype(BF16)
    vt = _dot_nt(wv_ref[...], c).astype(BF16)
    ones = jnp.ones((MLA_V_ROWS - B_VDIM, vt.shape[1]), BF16)
    for h in range(B_HEADS):
        vt_ref[0, h * MLA_V_ROWS:h * MLA_V_ROWS + B_VDIM, :] = vt[h * B_VDIM:(h + 1) * B_VDIM]
        vt_ref[0, h * MLA_V_ROWS + B_VDIM:(h + 1) * MLA_V_ROWS, :] = ones


def mla_prep_kv(ckv, krope, wk_pad, place, wv_t, n_tiles, tm):
    kw = B_HEADS * MLA_HEAD_PAD
    return pl.pallas_call(
        _mla_prep_kv_body,
        grid=(n_tiles,),
        in_specs=[pl.BlockSpec((tm, KV_RANK), lambda i: (i, 0)), pl.BlockSpec((tm, B_ROPE), lambda i: (i, 0)),
                  _const_spec(wk_pad.shape), _const_spec(place.shape), _const_spec(wv_t.shape)],
        out_specs=[pl.BlockSpec((tm, kw), lambda i: (i, 0)), pl.BlockSpec((1, MLA_VT_ROWS, tm), lambda i: (i, 0, 0))],
        out_shape=[jax.ShapeDtypeStruct((n_tiles * tm, kw), BF16),
                   jax.ShapeDtypeStruct((n_tiles, MLA_VT_ROWS, tm), BF16)],
        compiler_params=_params(("parallel",)),
        name="mla_prep_kv",
    )(ckv, krope, wk_pad, place, wv_t)


def _mla_attn_body(qt_ref, kp_ref, vt_ref, o_ref, m_ref, l_ref, acc_ref, gap_ref, *, tq, tk, q0, n_diag):
    i = pl.program_id(1)
    start = q0 + i * tq
    n_full = start // tk
    key_chunk = lax.broadcasted_iota(jnp.int32, (tk, tq), 0) // CHUNK
    qry_chunk = lax.broadcasted_iota(jnp.int32, (tk, tq), 1) // CHUNK

    def step(j, masked, stale_max):
        row0 = pl.multiple_of(j * tk, tk)
        if masked:
            visible = key_chunk + (j * tk - start) // CHUNK <= qry_chunk

        def scores(h):
            k_h = kp_ref[pl.ds(row0, tk), h * MLA_HEAD_PAD:(h + 1) * MLA_HEAD_PAD]
            q_h = qt_ref[0, h * MLA_HEAD_PAD:(h + 1) * MLA_HEAD_PAD, :]
            return _dot(k_h, q_h)

        s_next = scores(0)
        for h in range(B_HEADS):
            s = s_next
            if h + 1 < B_HEADS:
                s_next = scores(h + 1)
            if masked:
                s = jnp.where(visible, s, NEG)
            rows = slice(h * B_VDIM, (h + 1) * B_VDIM)
            m_prev = m_ref[h:h + 1, :]
            blk_max = jnp.max(s, axis=0, keepdims=True)
            m_new = jnp.maximum(m_prev, blk_max)
            alpha = jnp.exp2(m_prev - m_new)
            p = jnp.exp2(s - (m_prev if stale_max else m_new)).astype(BF16)
            pv = _dot(vt_ref[j, h * MLA_V_ROWS:(h + 1) * MLA_V_ROWS, :], p)
            if stale_max:
                gap_ref[h:h + 1, :] = jnp.maximum(gap_ref[h:h + 1, :], blk_max - m_prev)
                l_ref[h:h + 1, :] = alpha * (l_ref[h:h + 1, :] + pv[B_VDIM:B_VDIM + 1])
                acc_ref[rows, :] = alpha * (acc_ref[rows, :] + pv[:B_VDIM])
            else:
                l_ref[h:h + 1, :] = alpha * l_ref[h:h + 1, :] + pv[B_VDIM:B_VDIM + 1]
                acc_ref[rows, :] = alpha * acc_ref[rows, :] + pv[:B_VDIM]
            m_ref[h:h + 1, :] = m_new

    def sweep(stale_max):
        m_ref[...] = jnp.full(m_ref.shape, NEG, F32)
        l_ref[...] = jnp.zeros(l_ref.shape, F32)
        acc_ref[...] = jnp.zeros(acc_ref.shape, F32)
        first = 0
        if stale_max:
            @pl.when(n_full > 0)
            def _():
                step(0, False, False)

            first = 1

        def full_step(j, carry):
            step(j, False, stale_max)
            return carry

        lax.fori_loop(first, n_full, full_step, 0)
        for d in range(n_diag):
            if stale_max and d == 0:
                @pl.when(n_full > 0)
                def _():
                    step(n_full, True, True)

                @pl.when(n_full == 0)
                def _():
                    step(n_full, True, False)
            else:
                step(n_full + d, True, stale_max)

    gap_ref[...] = jnp.zeros(gap_ref.shape, F32)
    sweep(True)

    @pl.when(jnp.max(gap_ref[...]) > MAX_OVERSHOOT)
    def _():
        sweep(False)

    for h in range(B_HEADS):
        rows = slice(h * B_VDIM, (h + 1) * B_VDIM)
        acc_ref[rows, :] = acc_ref[rows, :] / l_ref[h:h + 1, :]
    o_ref[...] = jnp.transpose(acc_ref[...]).astype(o_ref.dtype)


def mla_attn(qt, kp, vt, groups, nq, tq, tk, n_blocks, q0, q_frames):
    kw = B_HEADS * MLA_HEAD_PAD
    assert q0 % tk == 0 and (tq % tk == 0 or (nq == 1 and q_frames <= tk))
    n_diag = pl.cdiv(q_frames, tk)
    return pl.pallas_call(
        functools.partial(_mla_attn_body, tq=tq, tk=tk, q0=q0, n_diag=n_diag),
        grid=(groups, nq),
        in_specs=[pl.BlockSpec((1, kw, tq), lambda g, i: (g * nq + i, 0, 0)),
                  pl.BlockSpec((n_blocks * tk, kw), lambda g, i: (g, 0), pipeline_mode=pl.Buffered(1)),
                  pl.BlockSpec((n_blocks, MLA_VT_ROWS, tk), lambda g, i: (g, 0, 0), pipeline_mode=pl.Buffered(1))],
        out_specs=pl.BlockSpec((tq, B_WIDTH), lambda g, i: (g * nq + i, 0)),
        out_shape=jax.ShapeDtypeStruct((groups * nq * tq, B_WIDTH), BF16),
        scratch_shapes=[pltpu.VMEM((B_HEADS, tq), F32), pltpu.VMEM((B_HEADS, tq), F32),
                        pltpu.VMEM((B_WIDTH, tq), F32), pltpu.VMEM((B_HEADS, tq), F32)],
        compiler_params=_params(("parallel", "arbitrary")),
        name="mla_attn",
    )(qt, kp, vt)


def _mla_sample_body(sm_ref, qn_ref, wq_ref, rc_ref, rs1_ref, rs2_ref, wabs_ref, sel_ref, cckv_ref, ckr_ref,
                     nckv_ref, nkr_ref, wuv_ref, o_ref, m_ref, l_ref, acc_ref, *, n_chunks, chunk):
    frames = sm_ref.shape[0]
    half = B_ROPE // 2
    cq = _rms(sm_ref[:, SM_CQ:SM_CQ + Q_RANK], qn_ref[...]).astype(BF16)
    qf = _dot(cq, wq_ref[...]) * (MLA_SCALE * LOG2E)
    width = qf.shape[1]
    qb = (qf * rc_ref[...] + pltpu.roll(qf, half, 1) * rs1_ref[...]
          + pltpu.roll(qf, width - half, 1) * rs2_ref[...]).astype(BF16)
    q_lat, q_rope = [], []
    for h in range(B_HEADS):
        q_h = qb[:, h * MLA_HEAD_PAD:(h + 1) * MLA_HEAD_PAD]
        q_lat.append(_dot(q_h, wabs_ref[h]).astype(BF16))
        q_rope.append(_dot(q_h, sel_ref[...]).astype(BF16))
    q_lat = jnp.concatenate(q_lat, axis=0)
    q_rope = jnp.concatenate(q_rope, axis=0)
    m_ref[...] = jnp.full(m_ref.shape, NEG, F32)
    l_ref[...] = jnp.zeros(l_ref.shape, F32)
    acc_ref[...] = jnp.zeros(acc_ref.shape, F32)

    def attend(lat, rope, rope_feature_major):
        lat = lat.astype(BF16)
        rope = rope.astype(BF16)
        s_rope = _dot(q_rope, rope) if rope_feature_major else _dot_nt(q_rope, rope)
        s = _dot_nt(q_lat, lat) + s_rope
        m_prev = m_ref[...]
        m_new = jnp.maximum(m_prev, jnp.max(s, axis=1, keepdims=True))
        alpha = jnp.exp2(m_prev - m_new)
        p = jnp.exp2(s - m_new).astype(BF16)
        l_ref[...] = alpha * l_ref[...] + _dot(p, jnp.ones((p.shape[1], LANES), BF16))
        acc_ref[...] = alpha * acc_ref[...] + _dot(p, lat)
        m_ref[...] = m_new

    for c in range(n_chunks):
        attend(cckv_ref[0, 0, c * chunk:(c + 1) * chunk, :], ckr_ref[0, 0, :, c * chunk:(c + 1) * chunk], True)
    attend(nckv_ref[...], nkr_ref[...], False)
    o_lat = (acc_ref[...] / l_ref[:, 0:1]).astype(BF16)
    out = _dot(o_lat[:frames], wuv_ref[0])
    for h in range(1, B_HEADS):
        out = out + _dot(o_lat[h * frames:(h + 1) * frames], wuv_ref[h])
    o_ref[...] = out.astype(o_ref.dtype)


def mla_sample(small, ckv, krope, cache_ckv, cache_krope_t, layer, qn, wq, rope_full, wabs, sel, wuv_place,
               streams, frames, row0):
    past = cache_ckv.shape[2]
    chunk = 512
    base = row0 // frames
    rows_q = B_HEADS * frames
    rc, rs1, rs2 = rope_full
    return pl.pallas_call(
        functools.partial(_mla_sample_body, n_chunks=past // chunk, chunk=chunk),
        grid=(streams,),
        in_specs=[pl.BlockSpec((frames, SM_WIDTH), lambda g: (base + g, 0)), _const_spec(qn.shape),
                  _const_spec(wq.shape), _const_spec(rc.shape), _const_spec(rs1.shape), _const_spec(rs2.shape),
                  _const_spec(wabs.shape), _const_spec(sel.shape),
                  pl.BlockSpec((1, 1, past, KV_RANK), lambda g: (layer, g, 0, 0)),
                  pl.BlockSpec((1, 1, B_ROPE, past), lambda g: (layer, g, 0, 0)),
                  pl.BlockSpec((frames, KV_RANK), lambda g: (base + g, 0)),
                  pl.BlockSpec((frames, B_ROPE), lambda g: (base + g, 0)),
                  _const_spec(wuv_place.shape)],
        out_specs=pl.BlockSpec((frames, B_WIDTH), lambda g: (g, 0)),
        out_shape=jax.ShapeDtypeStruct((streams * frames, B_WIDTH), BF16),
        scratch_shapes=[pltpu.VMEM((rows_q, 1), F32), pltpu.VMEM((rows_q, LANES), F32),
                        pltpu.VMEM((rows_q, KV_RANK), F32)],
        compiler_params=_params(("parallel",)),
        name="mla_sample",
    )(small, qn, wq, rc, rs1, rs2, wabs, sel, cache_ckv, cache_krope_t, ckv, krope, wuv_place)


def _scan_rows(x, op, fill, length):
    row = lax.broadcasted_iota(jnp.int32, x.shape, 0)
    shift = 1
    while shift < length:
        moved = pltpu.roll(x, shift, 0)
        x = op(x, jnp.where(row >= shift, moved, fill))
        shift *= 2
    return x


def _mlstm_body(*refs, blk, per_step):
    n_in = 5 * per_step
    bias_ref, hn_ref, c0_ref, m0_ref, ha_ref, c_out_ref, m_out_ref, c_scr, m_scr = refs[n_in:]
    dh = A_HEAD_DIM
    step_idx = pl.program_id(1)

    @pl.when(step_idx == 0)
    def _():
        c_scr[...] = c0_ref[...]
        m_scr[...] = m0_ref[...]

    causal = (lax.broadcasted_iota(jnp.int32, (blk, blk), 0) >= lax.broadcasted_iota(jnp.int32, (blk, blk), 1))
    ones_col = jnp.ones((blk, dh), BF16)
    ones_mat = jnp.ones((2 * dh, dh), BF16)
    pad = max(blk, LANES) - blk
    prep = []
    for u in range(per_step):
        gt_ref = refs[5 * u + 4]
        gates = gt_ref[...] + bias_ref[...]
        log_f = jax.nn.log_sigmoid(gates)
        b_all = pltpu.roll(_scan_rows(log_f, jnp.add, 0.0, blk), LANES - A_HEADS, 1)
        a_all = gates - b_all
        amax_all = _scan_rows(a_all, jnp.maximum, NEG, blk)
        a_sq = a_all if pad == 0 else jnp.concatenate([a_all, jnp.zeros((pad, LANES), F32)], axis=0)
        prep.append((a_all, b_all, amax_all, jnp.transpose(a_sq)))
    chains = [(u, h) for u in range(per_step) for h in range(A_HEADS)]

    def cols(h):
        return slice(h * dh, (h + 1) * dh)

    scores, v_exts, run_maxes, m_prevs = [], [], [], []
    for u, h in chains:
        q_ref, k_ref, v_ref = refs[5 * u:5 * u + 3]
        scores.append(_dot_nt(q_ref[:, cols(h)], k_ref[:, cols(h)]))
        v_exts.append(jnp.concatenate([v_ref[:, cols(h)], ones_col], axis=1))
        m_prev = m_scr[u, h:h + 1, 0:1]
        lane = GATE_I_LANE + h
        m_prevs.append(m_prev)
        run_maxes.append(jnp.maximum(prep[u][2][:, lane:lane + 1], m_prev))
    probs = []
    for n, (u, h) in enumerate(chains):
        lane = GATE_I_LANE + h
        a_row = prep[u][3][lane:lane + 1, :blk]
        decay_mat = jnp.exp(jnp.where(causal, a_row - run_maxes[n], NEG))
        probs.append((scores[n] * decay_mat).astype(BF16))
    numdens = []
    for n, (u, h) in enumerate(chains):
        q_ref = refs[5 * u]
        state = c_scr[u, h]
        w_inter = jnp.exp(m_prevs[n] - run_maxes[n])
        numdens.append(w_inter * _dot(q_ref[:, cols(h)], state.astype(BF16)) + _dot(probs[n], v_exts[n]))
    for n, (u, h) in enumerate(chains):
        o_ref = refs[5 * u + 3]
        lane = GATE_I_LANE + h
        b_col = prep[u][1][:, lane:lane + 1]
        numden = numdens[n]
        den = numden[:, dh:]
        hh = numden[:, :dh] / jnp.maximum(jnp.abs(den), jnp.exp(-(b_col + run_maxes[n])))
        sq = hh * hh
        sq_hi = sq.astype(BF16)
        sq_lo = (sq - sq_hi.astype(F32)).astype(BF16)
        mean_sq = _dot(jnp.concatenate([sq_hi, sq_lo], axis=1), ones_mat) * (1.0 / dh)
        hh = hh * lax.rsqrt(mean_sq + EPS) * hn_ref[:, cols(h)]
        ha_ref[u, :, cols(h)] = (hh * jax.nn.sigmoid(o_ref[:, cols(h)])).astype(ha_ref.dtype)
    for n, (u, h) in enumerate(chains):
        k_ref = refs[5 * u + 1]
        lane = GATE_I_LANE + h
        a_col = prep[u][0][:, lane:lane + 1]
        b_col = prep[u][1][:, lane:lane + 1]
        max_last = run_maxes[n][blk - 1:blk, :]
        k_w = (k_ref[:, cols(h)].astype(F32) * jnp.exp(a_col - max_last)).astype(BF16)
        c_scr[u, h] = jnp.exp(m_prevs[n] - max_last) * c_scr[u, h] + _dot_tn(k_w, v_exts[n])
        m_scr[u, h:h + 1, :] = jnp.broadcast_to(b_col[blk - 1:blk, :] + max_last, (1, LANES))

    @pl.when(step_idx == pl.num_programs(1) - 1)
    def _():
        c_out_ref[...] = c_scr[...]
        m_out_ref[...] = m_scr[...]


def mlstm(qkv, o32, small, bias, hnorm, c0, m0, groups, steps, blk, row0, per_step):
    base = row0 // blk
    gate_block = SM_KR // LANES
    state_shape = (per_step, A_HEADS, A_HEAD_DIM, 2 * A_HEAD_DIM)

    def rows(u, col):
        return lambda g, s: (base + (g * per_step + u) * steps + s, col)

    in_specs, args = [], []
    for u in range(per_step):
        in_specs += [pl.BlockSpec((blk, A_WIDTH), rows(u, 0)), pl.BlockSpec((blk, A_WIDTH), rows(u, 1)),
                     pl.BlockSpec((blk, A_WIDTH), rows(u, 2)), pl.BlockSpec((blk, A_WIDTH), rows(u, 0)),
                     pl.BlockSpec((blk, LANES), rows(u, gate_block))]
        args += [qkv, qkv, qkv, o32, small]
    in_specs += [_const_spec((1, LANES)), _const_spec((1, A_WIDTH)),
                 pl.BlockSpec(state_shape, lambda g, s: (g, 0, 0, 0)),
                 pl.BlockSpec((per_step, 8, LANES), lambda g, s: (g, 0, 0))]
    ha, c_out, m_out = pl.pallas_call(
        functools.partial(_mlstm_body, blk=blk, per_step=per_step),
        grid=(groups // per_step, steps),
        in_specs=in_specs,
        out_specs=[pl.BlockSpec((per_step, blk, A_WIDTH), lambda g, s: (g, s, 0)),
                   pl.BlockSpec(state_shape, lambda g, s: (g, 0, 0, 0)),
                   pl.BlockSpec((per_step, 8, LANES), lambda g, s: (g, 0, 0))],
        out_shape=[jax.ShapeDtypeStruct((groups, steps * blk, A_WIDTH), BF16),
                   jax.ShapeDtypeStruct((groups, A_HEADS, A_HEAD_DIM, 2 * A_HEAD_DIM), F32),
                   jax.ShapeDtypeStruct((groups, 8, LANES), F32)],
        scratch_shapes=[pltpu.VMEM(state_shape, F32), pltpu.VMEM((per_step, 8, LANES), F32)],
        compiler_params=_params(("parallel", "arbitrary")),
        name="mlstm",
    )(*args, bias, hnorm, c0, m0)
    return ha.reshape(groups * steps * blk, A_WIDTH), c_out, m_out


def _band_proj_body(x_ref, g_ref, wk_ref, wq_t_ref, wv_t_ref, wkv_t_ref, k_ref, qt_ref, vt_ref, tail_ref,
                    *, tiles_per_seq, p_tiles):
    i = pl.program_id(0)
    h = _rms(x_ref[...], g_ref[...]).astype(BF16)
    step = 512
    for c0 in range(0, C_WIDTH, step):
        k_ref[:, c0:c0 + step] = _dot(h, wk_ref[:, c0:c0 + step]).astype(BF16)
        qt = _dot_nt(wq_t_ref[c0:c0 + step, :], h) * (C_SCALE * LOG2E)
        qt_ref[0, c0:c0 + step, :] = qt.astype(BF16)
        vt_ref[0, c0:c0 + step, :] = _dot_nt(wv_t_ref[c0:c0 + step, :], h).astype(BF16)

    is_tail = ((i + 1) % tiles_per_seq == 0) | (i >= p_tiles)

    @pl.when(is_tail)
    def _():
        for c0 in range(0, 2 * C_WIDTH, step):
            tail_ref[0, c0:c0 + step, :] = _dot_nt(wkv_t_ref[c0:c0 + step, :], h)

    @pl.when(jnp.logical_not(is_tail) & (i % tiles_per_seq == 0))
    def _():
        tail_ref[...] = jnp.zeros(tail_ref.shape, F32)


def band_proj(x, g, wk, wq_t, wv_t, wkv_t, tm, tiles_per_seq, p_tiles, n_seq):
    t, d = x.shape
    n_tiles = t // tm
    n_tail = n_seq + n_tiles - p_tiles
    cw = C_WIDTH

    def tail_index(i):
        return (jnp.where(i < p_tiles, i // tiles_per_seq, n_seq + i - p_tiles), 0, 0)

    return pl.pallas_call(
        functools.partial(_band_proj_body, tiles_per_seq=tiles_per_seq, p_tiles=p_tiles),
        grid=(n_tiles,),
        in_specs=[pl.BlockSpec((tm, d), lambda i: (i, 0)), _const_spec((1, d)), _const_spec((d, cw)),
                  _const_spec((cw, d)), _const_spec((cw, d)), _const_spec((2 * cw, d))],
        out_specs=[pl.BlockSpec((tm, cw), lambda i: (i, 0)), pl.BlockSpec((1, cw, tm), lambda i: (i, 0, 0)),
                   pl.BlockSpec((1, cw, tm), lambda i: (i, 0, 0)), pl.BlockSpec((1, 2 * cw, tm), tail_index)],
        out_shape=[jax.ShapeDtypeStruct((t, cw), BF16), jax.ShapeDtypeStruct((n_tiles, cw, tm), BF16),
                   jax.ShapeDtypeStruct((n_tiles, cw, tm), BF16), jax.ShapeDtypeStruct((n_tail, 2 * cw, tm), F32)],
        compiler_params=_params(("arbitrary",)),
        name="band_proj",
    )(x, g, wk, wq_t, wv_t, wkv_t)


def _band_table_body(base_ref, tab_ref):
    tq = BAND_TILE
    shape = (tq, BAND_ROLL_WIDTH)
    row = lax.broadcasted_iota(jnp.int32, shape, 0)
    tab = jnp.broadcast_to(base_ref[0], shape)
    shift = 1
    while shift < tq:
        tab = jnp.where((row & shift) != 0, pltpu.roll(tab, shift, 1), tab)
        shift *= 2
    win = lax.broadcasted_iota(jnp.int32, (tq, BAND_WINDOW), 1) // CHUNK
    qch = lax.broadcasted_iota(jnp.int32, (tq, BAND_WINDOW), 0) // CHUNK
    valid = (win >= qch) & (win <= qch + LEFT_CHUNKS)
    tab_ref[0] = jnp.transpose(jnp.where(valid, tab[:, :BAND_WINDOW] * LOG2E, NEG))


def band_table(base_rows):
    return pl.pallas_call(
        _band_table_body,
        grid=(C_HEADS,),
        in_specs=[pl.BlockSpec((1, 1, BAND_ROLL_WIDTH), lambda h: (h, 0, 0))],
        out_specs=pl.BlockSpec((1, BAND_WINDOW, BAND_TILE), lambda h: (h, 0, 0)),
        out_shape=jax.ShapeDtypeStruct((C_HEADS, BAND_WINDOW, BAND_TILE), F32),
        compiler_params=_params(("parallel",)),
        name="band_table",
    )(base_rows)


def _band_body(qt_ref, *refs, clamp_start, cache_values):
    nb = BAND_WINDOW // BAND_TILE
    k_refs = refs[:nb]
    v_refs = refs[nb:2 * nb]
    tab_ref, o_ref, acc_ref = refs[2 * nb:]
    tq = BAND_TILE
    i = pl.program_id(1)
    row_half = lax.broadcasted_iota(jnp.int32, (LANES, tq), 0) // C_HEAD_DIM
    ones = jnp.ones((16, tq), BF16)

    def value_rows(j, rows):
        if cache_values and j < nb - 1:
            v = v_refs[j][0, 0, rows, :].astype(BF16)
        else:
            v = v_refs[j][0, rows, :]
        return jnp.concatenate([v, ones], axis=0)

    def scores(h, first_tiles):
        pair, half = divmod(h, 2)
        q_pair = qt_ref[0, pair * LANES:(pair + 1) * LANES, :]
        q_h = jnp.where(row_half == half, q_pair, jnp.zeros_like(q_pair))
        parts = []
        for j in range(nb):
            if cache_values and j < nb - 1:
                k_t = k_refs[j][0, 0, pair * LANES:(pair + 1) * LANES, :]
                s_j = _dot_tn(k_t.astype(BF16), q_h)
            else:
                s_j = _dot(k_refs[j][:, pair * LANES:(pair + 1) * LANES], q_h)
            if first_tiles and j < nb - 1:
                s_j = jnp.where(i + j < nb - 1, NEG, s_j)
            parts.append(s_j + tab_ref[h, j * BAND_TILE:(j + 1) * BAND_TILE, :])
        return parts

    def attend(h, parts, m):
        rows = slice(h * C_HEAD_DIM, (h + 1) * C_HEAD_DIM)
        pv = None
        for j in range(nb):
            part = _dot(value_rows(j, rows), jnp.exp2(parts[j] - m).astype(BF16))
            pv = part if pv is None else pv + part
        acc_ref[rows, :] = pv[:C_HEAD_DIM] / pv[C_HEAD_DIM:C_HEAD_DIM + 1]

    def run(first_tiles):
        s_next = scores(0, first_tiles)
        for h in range(C_HEADS):
            parts = s_next
            if h + 1 < C_HEADS:
                s_next = scores(h + 1, first_tiles)
            m = parts[0]
            for j in range(1, nb):
                m = jnp.maximum(m, parts[j])
            attend(h, parts, jnp.max(m, axis=0, keepdims=True))

    if clamp_start:
        @pl.when(i < nb - 1)
        def _():
            run(True)

        @pl.when(i >= nb - 1)
        def _():
            run(False)
    else:
        run(False)
    o_ref[...] = jnp.transpose(acc_ref[...]).astype(o_ref.dtype)


def band_attn(qt_arr, qt_map, k_arrs, k_specs, v_arrs, v_specs, tab, groups, tiles, clamp_start, cache_values):
    tq = BAND_TILE
    in_specs = [pl.BlockSpec((1, C_WIDTH, tq), qt_map)]
    in_specs += list(k_specs)
    in_specs += list(v_specs)
    in_specs.append(_const_spec(tab.shape))
    return pl.pallas_call(
        functools.partial(_band_body, clamp_start=clamp_start, cache_values=cache_values),
        grid=(groups, tiles),
        in_specs=in_specs,
        out_specs=pl.BlockSpec((tq, C_WIDTH), lambda g, i: (g * tiles + i, 0)),
        out_shape=jax.ShapeDtypeStruct((groups * tiles * tq, C_WIDTH), BF16),
        scratch_shapes=[pltpu.VMEM((C_WIDTH, tq), F32)],
        compiler_params=_params(("parallel", "arbitrary")),
        name="band_attn",
    )(qt_arr, *k_arrs, *v_arrs, tab)


def _cache_roll_body(c_ref, *refs, new_frames):
    tail_refs = refs[:-1]
    o_ref = refs[-1]
    layer = pl.program_id(0)
    stream = pl.program_id(1)
    buf = c_ref.shape[3]
    rolled = pltpu.roll(c_ref[0, 0], buf - new_frames, 1)
    new = tail_refs[0][0]
    for n in range(1, len(tail_refs)):
        new = jnp.where(layer == n, tail_refs[n][0], new)
    upper_half = (stream % (LANES // new_frames)) == 1
    new = jnp.where(upper_half, new, pltpu.roll(new, new_frames, 1))
    o_ref[0, 0] = rolled
    lane = lax.broadcasted_iota(jnp.int32, new.shape, 1)
    o_ref[0, 0, :, buf - LANES:] = jnp.where(lane < LANES - new_frames, rolled[:, buf - LANES:], new)


def cache_roll(cache_t, tails, row_block, first_tile, new_frames):
    layers, streams, feat, buf = cache_t.shape
    tm = tails[0].shape[2]
    per_tile = tm // new_frames
    groups = LANES // new_frames
    assert 2 * new_frames == LANES and layers == len(tails)

    def tail_map(l, s):
        return (first_tile + s // per_tile, row_block, (s % per_tile) // groups)

    return pl.pallas_call(
        functools.partial(_cache_roll_body, new_frames=new_frames),
        grid=(layers, streams),
        in_specs=[pl.BlockSpec((1, 1, feat, buf), lambda l, s: (l, s, 0, 0))]
                 + [pl.BlockSpec((1, feat, LANES), tail_map)] * layers,
        out_specs=pl.BlockSpec((1, 1, feat, buf), lambda l, s: (l, s, 0, 0)),
        out_shape=jax.ShapeDtypeStruct(cache_t.shape, cache_t.dtype),
        compiler_params=_params(("parallel", "parallel")),
        name="cache_roll",
    )(cache_t, *tails)


def _rope_tables(positions):
    half = B_ROPE // 2
    inv = ROPE_THETA ** (-jnp.arange(half, dtype=F32) / half)
    ang = positions.astype(F32)[:, None] * inv[None, :]
    cos, sin = jnp.cos(ang), jnp.sin(ang)
    n = positions.shape[0]
    zeros = jnp.zeros((n, LANES - B_ROPE), F32)
    zh = jnp.zeros((n, half), F32)
    rc = jnp.concatenate([cos, cos, zeros], axis=1)
    rs1 = jnp.concatenate([zh, sin, zeros], axis=1)
    rs2 = jnp.concatenate([-sin, zh, zeros], axis=1)
    return rc, rs1, rs2, cos.T, sin.T


def _band_base_rows(rel_bias):
    x = jnp.arange(BAND_ROLL_WIDTH)
    rel = jnp.where(x < BAND_WINDOW, BAND_LEFT - x, BAND_LEFT + 1)
    idx = jnp.clip(rel, -MAX_REL, MAX_REL) + MAX_REL
    return rel_bias[:, None, idx]


def kernel(x_prompt, x_sample, cache_mla_ckv, cache_mla_krope, state_mlstm_C, state_mlstm_n, state_mlstm_m,
           cache_band_k, cache_band_v, norm_mix, norm_ffn, norm_final, w_in_ab, b_gates, mlstm_hnorm,
           mla_q_norm, mla_kv_norm, mla_w_uq, mla_w_uk, mla_w_uv, w_out_ab, w_qkv_c, w_out_c, rel_bias_c,
           w_gate, w_up, w_down):
    batch, seq, d = x_prompt.shape
    dec_batch, dec_seq, _ = x_sample.shape
    past = cache_mla_ckv.shape[2]
    band_buf = cache_band_k.shape[2]
    depth = norm_mix.shape[0]
    tp = batch * seq
    ts = dec_batch * dec_seq
    t = tp + ts
    tm_big = 1024
    tm = MLA_Q_TILE
    tk = MLA_KV_BLOCK
    assert d == D_MODEL and dec_seq == CHUNK and band_buf == BAND_LEFT and past % tk == 0
    assert tp % tm_big == 0 and ts % tm_big == 0 and seq % tm == 0 and tm % dec_seq == 0 and tm % tk == 0

    assert past % CHUNK == 0 and ts == tm_big
    x_parts = (x_prompt.reshape(tp, d), x_sample.reshape(ts, d))

    pos_tab = jnp.concatenate([jnp.arange(seq, dtype=jnp.int32),
                               past + (jnp.arange(tm, dtype=jnp.int32) % dec_seq)])
    rope_tabs = _rope_tables(pos_tab)
    head_pat = jnp.concatenate([jnp.ones((dec_seq, B_NOPE), F32), jnp.zeros((dec_seq, LANES - B_NOPE), F32)], axis=1)
    roll_pad = ((0, 0), (B_NOPE, LANES - B_QK))
    rope_full = tuple(jnp.tile(tab, (1, B_HEADS)) for tab in (
        head_pat + jnp.pad(rope_tabs[0][seq:seq + dec_seq, :B_ROPE], roll_pad),
        jnp.pad(rope_tabs[1][seq:seq + dec_seq, :B_ROPE], roll_pad),
        jnp.pad(rope_tabs[2][seq:seq + dec_seq, :B_ROPE], roll_pad)))
    sel = jnp.pad(jnp.eye(B_ROPE, dtype=F32), ((B_NOPE, MLA_HEAD_PAD - B_QK), (0, 0))).astype(BF16)
    cache_krope_t = jnp.swapaxes(cache_mla_krope, 2, 3)

    def frames_last(c):
        return jnp.transpose(c, (0, 1, 3, 4, 2)).reshape(c.shape[0], dec_batch, C_WIDTH, band_buf)

    cache_k_t = frames_last(cache_band_k)
    cache_v_t = frames_last(cache_band_v)
    band_tails = []
    p_tiles = tp // tm
    seq_tiles = seq // tm

    def tab_index(i):
        return jnp.where(i < p_tiles, i % seq_tiles, seq_tiles)

    outs = {k: [] for k in ("p_ckv", "p_kr", "p_C", "p_n", "p_m", "p_bk", "p_bv",
                            "s_ckv", "s_kr", "s_C", "s_n", "s_m", "s_bk", "s_bv")}
    a4 = 4 * A_WIDTH
    for layer in range(depth):
        j = layer // 2
        g_mix = norm_mix[layer][None, :]
        g_ffn = norm_ffn[layer][None, :]
        last = layer == depth - 1
        g_fin = norm_final[None, :] if last else None
        ffn_w = (w_gate[layer].astype(BF16), w_up[layer].astype(BF16), w_down[layer].astype(BF16))
        if layer % 2 == 0:
            w = w_in_ab[j]
            gate_cols = jnp.concatenate([w[:, a4 + 2 * A_HEADS + Q_RANK + KV_RANK:], w[:, a4:a4 + 2 * A_HEADS],
                                         jnp.zeros((d, LANES - B_ROPE - 2 * A_HEADS), F32)], axis=1)
            w_all = jnp.concatenate([w[:, :a4], w[:, a4 + 2 * A_HEADS:a4 + 2 * A_HEADS + Q_RANK + KV_RANK],
                                     gate_cols], axis=1).astype(BF16)
            aw = A_WIDTH
            plan = [(0, aw, [(0, 0, 1.0)]), (aw, aw, [(0, aw, A_HEAD_DIM ** -0.5)]), (2 * aw, aw, [(0, 2 * aw, 1.0)]),
                    (3 * aw, aw, [(1, 0, 1.0)]), (4 * aw, Q_RANK, [(2, SM_CQ, 1.0)]),
                    (4 * aw + Q_RANK, KV_RANK + LANES, [(2, SM_CKV, 1.0)])]
            qkv, o32, small = norm_proj(x_parts, g_mix, w_all, plan, (3 * aw, aw, SM_WIDTH), (BF16, F32, F32), tm_big)

            wq = mla_w_uq[j].reshape(Q_RANK, B_HEADS, B_QK)
            wq_t = jnp.pad(wq, ((0, 0), (0, 0), (0, MLA_HEAD_PAD - B_QK))).reshape(Q_RANK, -1).T.astype(BF16)
            ckv, krope, qt = mla_prep_q(small, mla_q_norm[j][None, :], mla_kv_norm[j][None, :], wq_t,
                                        rope_tabs, tab_index, tm)
            wk_pad = jnp.pad(mla_w_uk[j], ((0, 0), (0, 0), (0, MLA_HEAD_PAD - B_NOPE))).reshape(KV_RANK, -1).astype(BF16)
            place = jnp.pad(jnp.eye(B_ROPE, dtype=F32), ((0, 0), (B_NOPE, MLA_HEAD_PAD - B_QK)))
            place = jnp.tile(place, (1, B_HEADS)).astype(BF16)
            wv_t = mla_w_uv[j].reshape(KV_RANK, B_WIDTH).T.astype(BF16)
            kp_p, vt_p = mla_prep_kv(ckv, krope, wk_pad, place, wv_t, tp // tk, tk)
            ckv_s = ckv[tp:].reshape(dec_batch, dec_seq, KV_RANK)
            kr_s = krope[tp:].reshape(dec_batch, dec_seq, B_ROPE)

            hb_p = mla_attn(qt, kp_p, vt_p, batch, seq_tiles, tm, tk, seq // tk, 0, tm)
            wabs = jnp.pad(jnp.transpose(mla_w_uk[j], (1, 2, 0)),
                           ((0, 0), (0, MLA_HEAD_PAD - B_NOPE), (0, 0))).astype(BF16)
            wuv_place = jnp.einsum("rhv,hg->hrgv", mla_w_uv[j], jnp.eye(B_HEADS, dtype=F32))
            wuv_place = wuv_place.reshape(B_HEADS, KV_RANK, B_WIDTH).astype(BF16)
            hb_s = mla_sample(small, ckv, krope, cache_mla_ckv, cache_krope_t, j, mla_q_norm[j][None, :],
                              wq_t.T, rope_full, wabs, sel, wuv_place, dec_batch, dec_seq, tp)

            bias = jnp.zeros((1, LANES), F32).at[0, GATE_I_LANE:GATE_I_LANE + 2 * A_HEADS].set(b_gates[j])
            hn = mlstm_hnorm[j][None, :]
            c0_p = jnp.zeros((batch, A_HEADS, A_HEAD_DIM, 2 * A_HEAD_DIM), F32)
            m0_p = jnp.zeros((batch, 8, LANES), F32)
            blk_p = 256
            ha_p, c_p, m_p = mlstm(qkv, o32, small, bias, hn, c0_p, m0_p, batch, seq // blk_p, blk_p, 0, batch)
            n_rep = jnp.broadcast_to(state_mlstm_n[j][..., None], (dec_batch, A_HEADS, A_HEAD_DIM, A_HEAD_DIM))
            c0_s = jnp.concatenate([state_mlstm_C[j], n_rep], axis=-1)
            m0_s = jnp.broadcast_to(jnp.pad(state_mlstm_m[j], ((0, 0), (0, 8 - A_HEADS)))[..., None],
                                    (dec_batch, 8, LANES))
            ha_s, c_s, m_s = mlstm(qkv, o32, small, bias, hn, c0_s, m0_s, dec_batch, 1, dec_seq, tp, 2)

            wo = w_out_ab[j].astype(BF16)
            x_parts = mix_ffn(x_parts, ((ha_p, ha_s), (hb_p, hb_s)), (wo[:A_WIDTH], wo[A_WIDTH:]), g_ffn, *ffn_w,
                              g_fin, tm_big, tp, False)
            x_parts = (x_parts,)

            outs["p_ckv"].append(ckv[:tp].reshape(batch, seq, KV_RANK))
            outs["p_kr"].append(krope[:tp].reshape(batch, seq, B_ROPE))
            outs["p_C"].append(c_p[..., :A_HEAD_DIM])
            outs["p_n"].append(c_p[..., A_HEAD_DIM])
            outs["p_m"].append(m_p[:, :A_HEADS, 0])
            outs["s_ckv"].append(ckv_s)
            outs["s_kr"].append(kr_s)
            outs["s_C"].append(c_s[..., :A_HEAD_DIM])
            outs["s_n"].append(c_s[..., A_HEAD_DIM])
            outs["s_m"].append(m_s[:, :A_HEADS, 0])
        else:
            cw = C_WIDTH
            wqkv = w_qkv_c[j].astype(BF16)
            tmb = 1024
            tps = seq // tmb
            ptb = tp // tmb
            assert band_buf <= tmb and seq % tmb == 0 and ts % tmb == 0 and tmb % BAND_TILE == 0
            (x,) = x_parts
            k_bf, qt_all, vt_all, kv_tail = band_proj(x, g_mix, wqkv[:, cw:2 * cw], wqkv[:, :cw].T,
                                                       wqkv[:, 2 * cw:].T, wqkv[:, cw:].T, tmb, tps, ptb, batch)
            tab = band_table(_band_base_rows(rel_bias_c[j]))
            tq = BAND_TILE
            nb = BAND_WINDOW // tq
            sub = tmb // tq
            tiles = seq // tq

            def block(g, i, jj):
                return g * tiles + jnp.maximum(i + jj - (nb - 1), 0)

            o_p = band_attn(
                qt_all, lambda g, i: ((g * tiles + i) // sub, 0, (g * tiles + i) % sub),
                [k_bf] * nb,
                [pl.BlockSpec((tq, cw), functools.partial(lambda g, i, jj: (block(g, i, jj), 0), jj=jj))
                 for jj in range(nb)],
                [vt_all] * nb,
                [pl.BlockSpec((1, cw, tq), functools.partial(
                    lambda g, i, jj: (block(g, i, jj) // sub, 0, block(g, i, jj) % sub), jj=jj)) for jj in range(nb)],
                tab, batch, tiles, True, False)

            def per_stream_t(a):
                a = a.reshape(-1, cw, tmb // dec_seq, dec_seq)
                return jnp.moveaxis(a, 2, 1).reshape(dec_batch, cw, dec_seq)

            qt_s = jnp.pad(per_stream_t(qt_all[ptb:]), ((0, 0), (0, 0), (0, tq - dec_seq)))
            k_new = jnp.pad(k_bf[tp:].reshape(dec_batch, dec_seq, cw), ((0, 0), (0, tq - dec_seq), (0, 0)))
            vt_new = jnp.pad(per_stream_t(vt_all[ptb:]), ((0, 0), (0, 0), (0, tq - dec_seq)))
            cache_spec = [pl.BlockSpec((1, 1, cw, tq), functools.partial(lambda g, i, jj: (j, g, 0, jj), jj=jj))
                          for jj in range(nb - 1)]
            o_s = band_attn(
                qt_s, lambda g, i: (g, 0, 0),
                [cache_k_t] * (nb - 1) + [k_new.reshape(-1, cw)],
                cache_spec + [pl.BlockSpec((tq, cw), lambda g, i: (g, 0))],
                [cache_v_t] * (nb - 1) + [vt_new], cache_spec + [pl.BlockSpec((1, cw, tq), lambda g, i: (g, 0, 0))],
                tab, dec_batch, 1, False, True)
            o_s = o_s.reshape(dec_batch, tq, cw)[:, :dec_seq].reshape(ts, cw)
            x_parts = mix_ffn(x_parts, ((o_p, o_s),), (w_out_c[j].astype(BF16),), g_ffn, *ffn_w, g_fin, tm_big,
                              tp, last)
            x_parts = tuple(x_parts) if last else (x_parts,)

            kv_p = kv_tail[:batch, :, tmb - band_buf:].reshape(batch, 2, C_HEADS, C_HEAD_DIM, band_buf)
            outs["p_bk"].append(jnp.transpose(kv_p[:, 0], (0, 3, 1, 2)))
            outs["p_bv"].append(jnp.transpose(kv_p[:, 1], (0, 3, 1, 2)))
            band_tails.append(kv_tail)

    y_prompt = x_parts[0].reshape(batch, seq, d)
    y_sample = x_parts[1].reshape(dec_batch, dec_seq, d)

    def frames_first(c):
        return jnp.transpose(c.reshape(-1, dec_batch, C_HEADS, C_HEAD_DIM, band_buf), (0, 1, 4, 2, 3))

    s_bk = frames_first(cache_roll(cache_k_t, band_tails, 0, batch, dec_seq))
    s_bv = frames_first(cache_roll(cache_v_t, band_tails, 1, batch, dec_seq))
    st = {k: jnp.stack(v) for k, v in outs.items() if v}
    return (y_prompt, y_sample, st["p_ckv"], st["p_kr"], st["p_C"], st["p_n"], st["p_m"], st["p_bk"], st["p_bv"],
            st["s_ckv"], st["s_kr"], st["s_C"], st["s_n"], st["s_m"], s_bk, s_bv)
```

```python
import functools
import math

import jax
import jax.numpy as jnp
from jax import lax
from jax.experimental import pallas as pl
from jax.experimental.pallas import tpu as pltpu

F32 = jnp.float32
BF16 = jnp.bfloat16

D_MODEL = 1024
CHUNK = 64
A_HEADS = 4
A_HEAD_DIM = 128
A_WIDTH = A_HEADS * A_HEAD_DIM
B_HEADS = 8
B_NOPE = 64
B_ROPE = 32
B_VDIM = 64
B_QK = B_NOPE + B_ROPE
B_WIDTH = B_HEADS * B_VDIM
Q_RANK = 384
KV_RANK = 256
ROPE_THETA = 10000.0
MLA_SCALE = B_QK ** -0.5
C_HEADS = 16
C_HEAD_DIM = 64
C_WIDTH = C_HEADS * C_HEAD_DIM
LEFT_CHUNKS = 8
MAX_REL = 128
C_SCALE = C_HEAD_DIM ** -0.5
EPS = 1e-6
NEG = -1e30
LOG2E = math.log2(math.e)

LANES = 128
MLA_HEAD_PAD = 128
MLA_KV_BLOCK = 512
MLA_Q_TILE = 512
MAX_OVERSHOOT = 64.0
MLA_V_ROWS = B_VDIM + 16
MLA_VT_ROWS = B_HEADS * MLA_V_ROWS
BAND_TILE = 256
BAND_V_ROWS = C_HEAD_DIM + 16
BAND_VT_ROWS = C_HEADS * BAND_V_ROWS
BAND_LEFT = LEFT_CHUNKS * CHUNK
BAND_WINDOW = BAND_LEFT + BAND_TILE
BAND_ROLL_WIDTH = BAND_WINDOW + BAND_TILE
VMEM_LIMIT = 56 * 1024 * 1024
VMEM_LIMIT_FFN = 61 * 1024 * 1024

SM_CQ = 0
SM_CKV = Q_RANK
SM_KR = Q_RANK + KV_RANK
SM_WIDTH = SM_KR + LANES
GATE_I_LANE = B_ROPE
GATE_F_LANE = B_ROPE + A_HEADS


def _const_spec(shape):
    zeros = (0,) * len(shape)
    return pl.BlockSpec(shape, lambda *_: zeros, pipeline_mode=pl.Buffered(1))


def _params(semantics, vmem_limit=VMEM_LIMIT):
    return pltpu.CompilerParams(dimension_semantics=semantics, vmem_limit_bytes=vmem_limit)


def _rms(x, g):
    return x * lax.rsqrt(jnp.mean(x * x, axis=-1, keepdims=True) + EPS) * g


def _dot(a, b):
    return jnp.dot(a, b, preferred_element_type=F32)


def _dot_nt(a, b):
    return lax.dot_general(a, b, (((1,), (1,)), ((), ())), preferred_element_type=F32)


def _dot_tn(a, b):
    return lax.dot_general(a, b, (((0,), (0,)), ((), ())), preferred_element_type=F32)


def _row_specs(parts, tm, p_tiles):
    width = parts[0].shape[1]
    if len(parts) == 1:
        return [pl.BlockSpec((tm, width), lambda i: (i, 0))]
    assert parts[1].shape[0] == tm
    return [pl.BlockSpec((tm, width), lambda i: (jnp.minimum(i, p_tiles - 1), 0)),
            pl.BlockSpec((tm, width), lambda i: (0, 0), pipeline_mode=pl.Buffered(1))]


def _row_load(refs, p_tiles):
    if len(refs) == 1:
        return refs[0][...]
    return jnp.where(pl.program_id(0) < p_tiles, refs[0][...], refs[1][...])


def _norm_proj_body(*refs, n_x, p_tiles, plan):
    x_refs = refs[:n_x]
    g_ref, w_ref = refs[n_x:n_x + 2]
    out_refs = refs[n_x + 2:]
    h = _rms(_row_load(x_refs, p_tiles), g_ref[...]).astype(BF16)
    for w0, width, dests in plan:
        z = _dot(h, w_ref[:, w0:w0 + width])
        for out_idx, o0, scale in dests:
            o_ref = out_refs[out_idx]
            zz = z if scale == 1.0 else z * scale
            o_ref[:, o0:o0 + width] = zz.astype(o_ref.dtype)


def norm_proj(x_parts, g, w, plan, out_widths, out_dtypes, tm):
    t = sum(p.shape[0] for p in x_parts)
    d = x_parts[0].shape[1]
    n = w.shape[1]
    p_tiles = x_parts[0].shape[0] // tm
    return pl.pallas_call(
        functools.partial(_norm_proj_body, n_x=len(x_parts), p_tiles=p_tiles, plan=plan),
        grid=(t // tm,),
        in_specs=_row_specs(x_parts, tm, p_tiles) + [_const_spec((1, d)), _const_spec((d, n))],
        out_specs=[pl.BlockSpec((tm, ow), lambda i: (i, 0)) for ow in out_widths],
        out_shape=[jax.ShapeDtypeStruct((t, ow), dt) for ow, dt in zip(out_widths, out_dtypes)],
        compiler_params=_params(("parallel",)),
        name="norm_proj",
    )(*x_parts, g, w)


def _mix_ffn_body(*refs, n_x, mix_counts, p_tiles, ff_chunks, final):
    x_refs = refs[:n_x]
    pos = n_x
    a_groups = []
    for count in mix_counts:
        a_groups.append(refs[pos:pos + count])
        pos += count
    wo_refs = refs[pos:pos + len(mix_counts)]
    pos += len(mix_counts)
    g_ref, wg_ref, wu_ref, wd_ref = refs[pos:pos + 4]
    pos += 4
    if final:
        gf_ref = refs[pos]
        pos += 1
    out_refs = refs[pos:-1]
    act_ref = refs[-1]
    x = _row_load(x_refs, p_tiles)
    for a_refs, wo_ref in zip(a_groups, wo_refs):
        x = x + _dot(_row_load(a_refs, p_tiles), wo_ref[...])
    h = _rms(x, g_ref[...]).astype(BF16)
    for c0, cw in ff_chunks:
        gate = _dot(h, wg_ref[:, c0:c0 + cw])
        up = _dot(h, wu_ref[:, c0:c0 + cw])
        act_ref[:, c0:c0 + cw] = (gate * jax.nn.sigmoid(gate) * up).astype(BF16)
    y = x + _dot(act_ref[...], wd_ref[...])
    if final:
        y = _rms(y, gf_ref[...])
    if len(out_refs) == 1:
        out_refs[0][...] = y
    else:
        i = pl.program_id(0)

        @pl.when(i < p_tiles)
        def _():
            out_refs[0][...] = y

        @pl.when(i >= p_tiles)
        def _():
            out_refs[1][...] = y


def mix_ffn(x_parts, mix_in, mix_w, g, wg, wu, wd, g_final, tm, p_rows, split_out):
    t = sum(p.shape[0] for p in x_parts)
    d = x_parts[0].shape[1]
    f = wg.shape[1]
    p_tiles = p_rows // tm
    chunk = 512
    ff_chunks = [(c0, min(chunk, f - c0)) for c0 in range(0, f, chunk)]
    final = g_final is not None
    in_specs = _row_specs(x_parts, tm, p_tiles)
    args = list(x_parts)
    for parts in mix_in:
        in_specs += _row_specs(parts, tm, p_tiles)
        args += list(parts)
    in_specs += [_const_spec(w.shape) for w in mix_w]
    in_specs += [_const_spec((1, d)), _const_spec((d, f)), _const_spec((d, f)), _const_spec((f, d))]
    args += [*mix_w, g, wg, wu, wd]
    if final:
        in_specs.append(_const_spec((1, d)))
        args.append(g_final)
    if split_out:
        assert t - p_rows == tm
        out_specs = [pl.BlockSpec((tm, d), lambda i: (jnp.minimum(i, p_tiles - 1), 0)),
                     pl.BlockSpec((tm, d), lambda i: (0, 0), pipeline_mode=pl.Buffered(1))]
        out_shape = [jax.ShapeDtypeStruct((p_rows, d), F32), jax.ShapeDtypeStruct((t - p_rows, d), F32)]
    else:
        out_specs = pl.BlockSpec((tm, d), lambda i: (i, 0))
        out_shape = jax.ShapeDtypeStruct((t, d), F32)
    return pl.pallas_call(
        functools.partial(_mix_ffn_body, n_x=len(x_parts), mix_counts=tuple(len(p) for p in mix_in),
                          p_tiles=p_tiles, ff_chunks=ff_chunks, final=final),
        grid=(t // tm,),
        in_specs=in_specs,
        out_specs=out_specs,
        out_shape=out_shape,
        scratch_shapes=[pltpu.VMEM((tm, f), BF16)],
        compiler_params=_params(("arbitrary",), VMEM_LIMIT_FFN),
        name="mix_ffn",
    )(*args)


def _mla_prep_q_body(sm_ref, qn_ref, kvn_ref, wq_ref, rc_ref, rs1_ref, rs2_ref, cos_t_ref, sin_t_ref,
                     ckv_ref, kr_ref, qt_ref):
    sm = sm_ref[...]
    cq = _rms(sm[:, SM_CQ:SM_CQ + Q_RANK], qn_ref[...]).astype(BF16)
    ckv_ref[...] = _rms(sm[:, SM_CKV:SM_CKV + KV_RANK], kvn_ref[...])
    grp = sm[:, SM_KR:SM_KR + LANES]
    half = B_ROPE // 2
    rot = (grp * rc_ref[...] + pltpu.roll(grp, half, 1) * rs1_ref[...]
           + pltpu.roll(grp, LANES - half, 1) * rs2_ref[...])
    kr_ref[...] = rot[:, :B_ROPE]
    qt = _dot_nt(wq_ref[...], cq) * (MLA_SCALE * LOG2E)
    cos_t = cos_t_ref[...]
    sin_t = sin_t_ref[...]
    for h in range(B_HEADS):
        r0 = h * MLA_HEAD_PAD
        x1 = qt[r0 + B_NOPE:r0 + B_NOPE + half]
        x2 = qt[r0 + B_NOPE + half:r0 + B_QK]
        qt_ref[0, r0:r0 + B_NOPE, :] = qt[r0:r0 + B_NOPE].astype(BF16)
        rot_q = jnp.concatenate([x1 * cos_t - x2 * sin_t, x1 * sin_t + x2 * cos_t], axis=0)
        qt_ref[0, r0 + B_NOPE:r0 + B_QK, :] = rot_q.astype(BF16)
        qt_ref[0, r0 + B_QK:r0 + MLA_HEAD_PAD, :] = qt[r0 + B_QK:r0 + MLA_HEAD_PAD].astype(BF16)


def mla_prep_q(small, qn, kvn, wq_t, rope_tabs, tab_index, tm):
    t = small.shape[0]
    rc, rs1, rs2, cos_t, sin_t = rope_tabs
    half = B_ROPE // 2
    row_tab = pl.BlockSpec((tm, LANES), lambda i: (tab_index(i), 0))
    col_tab = pl.BlockSpec((half, tm), lambda i: (0, tab_index(i)))
    return pl.pallas_call(
        _mla_prep_q_body,
        grid=(t // tm,),
        in_specs=[pl.BlockSpec((tm, SM_WIDTH), lambda i: (i, 0)), _const_spec(qn.shape), _const_spec(kvn.shape),
                  _const_spec(wq_t.shape), row_tab, row_tab, row_tab, col_tab, col_tab],
        out_specs=[pl.BlockSpec((tm, KV_RANK), lambda i: (i, 0)), pl.BlockSpec((tm, B_ROPE), lambda i: (i, 0)),
                   pl.BlockSpec((1, B_HEADS * MLA_HEAD_PAD, tm), lambda i: (i, 0, 0))],
        out_shape=[jax.ShapeDtypeStruct((t, KV_RANK), F32), jax.ShapeDtypeStruct((t, B_ROPE), F32),
                   jax.ShapeDtypeStruct((t // tm, B_HEADS * MLA_HEAD_PAD, tm), BF16)],
        compiler_params=_params(("parallel",)),
        name="mla_prep_q",
    )(small, qn, kvn, wq_t, rc, rs1, rs2, cos_t, sin_t)


def _mla_prep_kv_body(ckv_ref, kr_ref, wk_ref, place_ref, wv_ref, kp_ref, vt_ref):
    c = ckv_ref[...].astype(BF16)
    kr = kr_ref[...].astype(BF16)
    kp_ref[...] = (_dot(c, wk_ref[...]) + _dot(kr, place_ref[...])).astype(BF16)
    vt = _dot_nt(wv_ref[...], c).astype(BF16)
    ones = jnp.ones((MLA_V_ROWS - B_VDIM, vt.shape[1]), BF16)
    for h in range(B_HEADS):
        vt_ref[0, h * MLA_V_ROWS:h * MLA_V_ROWS + B_VDIM, :] = vt[h * B_VDIM:(h + 1) * B_VDIM]
        vt_ref[0, h * MLA_V_ROWS + B_VDIM:(h + 1) * MLA_V_ROWS, :] = ones


def mla_prep_kv(ckv, krope, wk_pad, place, wv_t, n_tiles, tm):
    kw = B_HEADS * MLA_HEAD_PAD
    return pl.pallas_call(
        _mla_prep_kv_body,
        grid=(n_tiles,),
        in_specs=[pl.BlockSpec((tm, KV_RANK), lambda i: (i, 0)), pl.BlockSpec((tm, B_ROPE), lambda i: (i, 0)),
                  _const_spec(wk_pad.shape), _const_spec(place.shape), _const_spec(wv_t.shape)],
        out_specs=[pl.BlockSpec((tm, kw), lambda i: (i, 0)), pl.BlockSpec((1, MLA_VT_ROWS, tm), lambda i: (i, 0, 0))],
        out_shape=[jax.ShapeDtypeStruct((n_tiles * tm, kw), BF16),
                   jax.ShapeDtypeStruct((n_tiles, MLA_VT_ROWS, tm), BF16)],
        compiler_params=_params(("parallel",)),
        name="mla_prep_kv",
    )(ckv, krope, wk_pad, place, wv_t)


def _mla_attn_body(qt_ref, kp_ref, vt_ref, o_ref, m_ref, l_ref, acc_ref, gap_ref, *, tq, tk, q0, n_diag):
    i = pl.program_id(1)
    start = q0 + i * tq
    n_full = start // tk
    key_chunk = lax.broadcasted_iota(jnp.int32, (tk, tq), 0) // CHUNK
    qry_chunk = lax.broadcasted_iota(jnp.int32, (tk, tq), 1) // CHUNK

    def step(j, masked, stale_max):
        row0 = pl.multiple_of(j * tk, tk)
        if masked:
            visible = key_chunk + (j * tk - start) // CHUNK <= qry_chunk

        def scores(h):
            k_h = kp_ref[pl.ds(row0, tk), h * MLA_HEAD_PAD:(h + 1) * MLA_HEAD_PAD]
            q_h = qt_ref[0, h * MLA_HEAD_PAD:(h + 1) * MLA_HEAD_PAD, :]
            return _dot(k_h, q_h)

        s_next = scores(0)
        for h in range(B_HEADS):
            s = s_next
            if h + 1 < B_HEADS:
                s_next = scores(h + 1)
            if masked:
                s = jnp.where(visible, s, NEG)
            rows = slice(h * B_VDIM, (h + 1) * B_VDIM)
            m_prev = m_ref[h:h + 1, :]
            blk_max = jnp.max(s, axis=0, keepdims=True)
            m_new = jnp.maximum(m_prev, blk_max)
            alpha = jnp.exp2(m_prev - m_new)
            p = jnp.exp2(s - (m_prev if stale_max else m_new)).astype(BF16)
            pv = _dot(vt_ref[j, h * MLA_V_ROWS:(h + 1) * MLA_V_ROWS, :], p)
            if stale_max:
                gap_ref[h:h + 1, :] = jnp.maximum(gap_ref[h:h + 1, :], blk_max - m_prev)
                l_ref[h:h + 1, :] = alpha * (l_ref[h:h + 1, :] + pv[B_VDIM:B_VDIM + 1])
                acc_ref[rows, :] = alpha * (acc_ref[rows, :] + pv[:B_VDIM])
            else:
                l_ref[h:h + 1, :] = alpha * l_ref[h:h + 1, :] + pv[B_VDIM:B_VDIM + 1]
                acc_ref[rows, :] = alpha * acc_ref[rows, :] + pv[:B_VDIM]
            m_ref[h:h + 1, :] = m_new

    def sweep(stale_max):
        m_ref[...] = jnp.full(m_ref.shape, NEG, F32)
        l_ref[...] = jnp.zeros(l_ref.shape, F32)
        acc_ref[...] = jnp.zeros(acc_ref.shape, F32)
        first = 0
        if stale_max:
            @pl.when(n_full > 0)
            def _():
                step(0, False, False)

            first = 1

        def full_step(j, carry):
            step(j, False, stale_max)
            return carry

        lax.fori_loop(first, n_full, full_step, 0)
        for d in range(n_diag):
            if stale_max and d == 0:
                @pl.when(n_full > 0)
                def _():
                    step(n_full, True, True)

                @pl.when(n_full == 0)
                def _():
                    step(n_full, True, False)
            else:
                step(n_full + d, True, stale_max)

    gap_ref[...] = jnp.zeros(gap_ref.shape, F32)
    sweep(True)

    @pl.when(jnp.max(gap_ref[...]) > MAX_OVERSHOOT)
    def _():
        sweep(False)

    for h in range(B_HEADS):
        rows = slice(h * B_VDIM, (h + 1) * B_VDIM)
        acc_ref[rows, :] = acc_ref[rows, :] / l_ref[h:h + 1, :]
    o_ref[...] = jnp.transpose(acc_ref[...]).astype(o_ref.dtype)


def mla_attn(qt, kp, vt, groups, nq, tq, tk, n_blocks, q0, q_frames):
    kw = B_HEADS * MLA_HEAD_PAD
    assert q0 % tk == 0 and (tq % tk == 0 or (nq == 1 and q_frames <= tk))
    n_diag = pl.cdiv(q_frames, tk)
    return pl.pallas_call(
        functools.partial(_mla_attn_body, tq=tq, tk=tk, q0=q0, n_diag=n_diag),
        grid=(groups, nq),
        in_specs=[pl.BlockSpec((1, kw, tq), lambda g, i: (g * nq + i, 0, 0)),
                  pl.BlockSpec((n_blocks * tk, kw), lambda g, i: (g, 0), pipeline_mode=pl.Buffered(1)),
                  pl.BlockSpec((n_blocks, MLA_VT_ROWS, tk), lambda g, i: (g, 0, 0), pipeline_mode=pl.Buffered(1))],
        out_specs=pl.BlockSpec((tq, B_WIDTH), lambda g, i: (g * nq + i, 0)),
        out_shape=jax.ShapeDtypeStruct((groups * nq * tq, B_WIDTH), BF16),
        scratch_shapes=[pltpu.VMEM((B_HEADS, tq), F32), pltpu.VMEM((B_HEADS, tq), F32),
                        pltpu.VMEM((B_WIDTH, tq), F32), pltpu.VMEM((B_HEADS, tq), F32)],
        compiler_params=_params(("parallel", "arbitrary")),
        name="mla_attn",
    )(qt, kp, vt)


def _mla_sample_body(sm_ref, qn_ref, wq_ref, rc_ref, rs1_ref, rs2_ref, wabs_ref, sel_ref, cckv_ref, ckr_ref,
                     nckv_ref, nkr_ref, wuv_ref, o_ref, m_ref, l_ref, acc_ref, *, n_chunks, chunk):
    frames = sm_ref.shape[0]
    half = B_ROPE // 2
    cq = _rms(sm_ref[:, SM_CQ:SM_CQ + Q_RANK], qn_ref[...]).astype(BF16)
    qf = _dot(cq, wq_ref[...]) * (MLA_SCALE * LOG2E)
    width = qf.shape[1]
    qb = (qf * rc_ref[...] + pltpu.roll(qf, half, 1) * rs1_ref[...]
          + pltpu.roll(qf, width - half, 1) * rs2_ref[...]).astype(BF16)
    q_lat, q_rope = [], []
    for h in range(B_HEADS):
        q_h = qb[:, h * MLA_HEAD_PAD:(h + 1) * MLA_HEAD_PAD]
        q_lat.append(_dot(q_h, wabs_ref[h]).astype(BF16))
        q_rope.append(_dot(q_h, sel_ref[...]).astype(BF16))
    q_lat = jnp.concatenate(q_lat, axis=0)
    q_rope = jnp.concatenate(q_rope, axis=0)
    m_ref[...] = jnp.full(m_ref.shape, NEG, F32)
    l_ref[...] = jnp.zeros(l_ref.shape, F32)
    acc_ref[...] = jnp.zeros(acc_ref.shape, F32)

    def attend(lat, rope, rope_feature_major):
        lat = lat.astype(BF16)
        rope = rope.astype(BF16)
        s_rope = _dot(q_rope, rope) if rope_feature_major else _dot_nt(q_rope, rope)
        s = _dot_nt(q_lat, lat) + s_rope
        m_prev = m_ref[...]
        m_new = jnp.maximum(m_prev, jnp.max(s, axis=1, keepdims=True))
        alpha = jnp.exp2(m_prev - m_new)
        p = jnp.exp2(s - m_new).astype(BF16)
        l_ref[...] = alpha * l_ref[...] + _dot(p, jnp.ones((p.shape[1], LANES), BF16))
        acc_ref[...] = alpha * acc_ref[...] + _dot(p, lat)
        m_ref[...] = m_new

    for c in range(n_chunks):
        attend(cckv_ref[0, 0, c * chunk:(c + 1) * chunk, :], ckr_ref[0, 0, :, c * chunk:(c + 1) * chunk], True)
    attend(nckv_ref[...], nkr_ref[...], False)
    o_lat = (acc_ref[...] / l_ref[:, 0:1]).astype(BF16)
    out = _dot(o_lat[:frames], wuv_ref[0])
    for h in range(1, B_HEADS):
        out = out + _dot(o_lat[h * frames:(h + 1) * frames], wuv_ref[h])
    o_ref[...] = out.astype(o_ref.dtype)


def mla_sample(small, ckv, krope, cache_ckv, cache_krope_t, layer, qn, wq, rope_full, wabs, sel, wuv_place,
               streams, frames, row0):
    past = cache_ckv.shape[2]
    chunk = 512
    base = row0 // frames
    rows_q = B_HEADS * frames
    rc, rs1, rs2 = rope_full
    return pl.pallas_call(
        functools.partial(_mla_sample_body, n_chunks=past // chunk, chunk=chunk),
        grid=(streams,),
        in_specs=[pl.BlockSpec((frames, SM_WIDTH), lambda g: (base + g, 0)), _const_spec(qn.shape),
                  _const_spec(wq.shape), _const_spec(rc.shape), _const_spec(rs1.shape), _const_spec(rs2.shape),
                  _const_spec(wabs.shape), _const_spec(sel.shape),
                  pl.BlockSpec((1, 1, past, KV_RANK), lambda g: (layer, g, 0, 0)),
                  pl.BlockSpec((1, 1, B_ROPE, past), lambda g: (layer, g, 0, 0)),
                  pl.BlockSpec((frames, KV_RANK), lambda g: (base + g, 0)),
                  pl.BlockSpec((frames, B_ROPE), lambda g: (base + g, 0)),
                  _const_spec(wuv_place.shape)],
        out_specs=pl.BlockSpec((frames, B_WIDTH), lambda g: (g, 0)),
        out_shape=jax.ShapeDtypeStruct((streams * frames, B_WIDTH), BF16),
        scratch_shapes=[pltpu.VMEM((rows_q, 1), F32), pltpu.VMEM((rows_q, LANES), F32),
                        pltpu.VMEM((rows_q, KV_RANK), F32)],
        compiler_params=_params(("parallel",)),
        name="mla_sample",
    )(small, qn, wq, rc, rs1, rs2, wabs, sel, cache_ckv, cache_krope_t, ckv, krope, wuv_place)


def _scan_rows(x, op, fill, length):
    row = lax.broadcasted_iota(jnp.int32, x.shape, 0)
    shift = 1
    while shift < length:
        moved = pltpu.roll(x, shift, 0)
        x = op(x, jnp.where(row >= shift, moved, fill))
        shift *= 2
    return x


def _mlstm_body(*refs, blk, per_step):
    n_in = 5 * per_step
    bias_ref, hn_ref, c0_ref, m0_ref, ha_ref, c_out_ref, m_out_ref, c_scr, m_scr = refs[n_in:]
    dh = A_HEAD_DIM
    step_idx = pl.program_id(1)

    @pl.when(step_idx == 0)
    def _():
        c_scr[...] = c0_ref[...]
        m_scr[...] = m0_ref[...]

    causal = (lax.broadcasted_iota(jnp.int32, (blk, blk), 0) >= lax.broadcasted_iota(jnp.int32, (blk, blk), 1))
    ones_col = jnp.ones((blk, dh), BF16)
    ones_mat = jnp.ones((2 * dh, dh), BF16)
    pad = max(blk, LANES) - blk
    prep = []
    for u in range(per_step):
        gt_ref = refs[5 * u + 4]
        gates = gt_ref[...] + bias_ref[...]
        log_f = jax.nn.log_sigmoid(gates)
        b_all = pltpu.roll(_scan_rows(log_f, jnp.add, 0.0, blk), LANES - A_HEADS, 1)
        a_all = gates - b_all
        amax_all = _scan_rows(a_all, jnp.maximum, NEG, blk)
        a_sq = a_all if pad == 0 else jnp.concatenate([a_all, jnp.zeros((pad, LANES), F32)], axis=0)
        prep.append((a_all, b_all, amax_all, jnp.transpose(a_sq)))
    chains = [(u, h) for u in range(per_step) for h in range(A_HEADS)]

    def cols(h):
        return slice(h * dh, (h + 1) * dh)

    scores, v_exts, run_maxes, m_prevs = [], [], [], []
    for u, h in chains:
        q_ref, k_ref, v_ref = refs[5 * u:5 * u + 3]
        scores.append(_dot_nt(q_ref[:, cols(h)], k_ref[:, cols(h)]))
        v_exts.append(jnp.concatenate([v_ref[:, cols(h)], ones_col], axis=1))
        m_prev = m_scr[u, h:h + 1, 0:1]
        lane = GATE_I_LANE + h
        m_prevs.append(m_prev)
        run_maxes.append(jnp.maximum(prep[u][2][:, lane:lane + 1], m_prev))
    probs = []
    for n, (u, h) in enumerate(chains):
        lane = GATE_I_LANE + h
        a_row = prep[u][3][lane:lane + 1, :blk]
        decay_mat = jnp.exp(jnp.where(causal, a_row - run_maxes[n], NEG))
        probs.append((scores[n] * decay_mat).astype(BF16))
    numdens = []
    for n, (u, h) in enumerate(chains):
        q_ref = refs[5 * u]
        state = c_scr[u, h]
        w_inter = jnp.exp(m_prevs[n] - run_maxes[n])
        numdens.append(w_inter * _dot(q_ref[:, cols(h)], state.astype(BF16)) + _dot(probs[n], v_exts[n]))
    for n, (u, h) in enumerate(chains):
        o_ref = refs[5 * u + 3]
        lane = GATE_I_LANE + h
        b_col = prep[u][1][:, lane:lane + 1]
        numden = numdens[n]
        den = numden[:, dh:]
        hh = numden[:, :dh] / jnp.maximum(jnp.abs(den), jnp.exp(-(b_col + run_maxes[n])))
        sq = hh * hh
        sq_hi = sq.astype(BF16)
        sq_lo = (sq - sq_hi.astype(F32)).astype(BF16)
        mean_sq = _dot(jnp.concatenate([sq_hi, sq_lo], axis=1), ones_mat) * (1.0 / dh)
        hh = hh * lax.rsqrt(mean_sq + EPS) * hn_ref[:, cols(h)]
        ha_ref[u, :, cols(h)] = (hh * jax.nn.sigmoid(o_ref[:, cols(h)])).astype(ha_ref.dtype)
    for n, (u, h) in enumerate(chains):
        k_ref = refs[5 * u + 1]
        lane = GATE_I_LANE + h
        a_col = prep[u][0][:, lane:lane + 1]
        b_col = prep[u][1][:, lane:lane + 1]
        max_last = run_maxes[n][blk - 1:blk, :]
        k_w = (k_ref[:, cols(h)].astype(F32) * jnp.exp(a_col - max_last)).astype(BF16)
        c_scr[u, h] = jnp.exp(m_prevs[n] - max_last) * c_scr[u, h] + _dot_tn(k_w, v_exts[n])
        m_scr[u, h:h + 1, :] = jnp.broadcast_to(b_col[blk - 1:blk, :] + max_last, (1, LANES))

    @pl.when(step_idx == pl.num_programs(1) - 1)
    def _():
        c_out_ref[...] = c_scr[...]
        m_out_ref[...] = m_scr[...]


def mlstm(qkv, o32, small, bias, hnorm, c0, m0, groups, steps, blk, row0, per_step):
    base = row0 // blk
    gate_block = SM_KR // LANES
    state_shape = (per_step, A_HEADS, A_HEAD_DIM, 2 * A_HEAD_DIM)

    def rows(u, col):
        return lambda g, s: (base + (g * per_step + u) * steps + s, col)

    in_specs, args = [], []
    for u in range(per_step):
        in_specs += [pl.BlockSpec((blk, A_WIDTH), rows(u, 0)), pl.BlockSpec((blk, A_WIDTH), rows(u, 1)),
                     pl.BlockSpec((blk, A_WIDTH), rows(u, 2)), pl.BlockSpec((blk, A_WIDTH), rows(u, 0)),
                     pl.BlockSpec((blk, LANES), rows(u, gate_block))]
        args += [qkv, qkv, qkv, o32, small]
    in_specs += [_const_spec((1, LANES)), _const_spec((1, A_WIDTH)),
                 pl.BlockSpec(state_shape, lambda g, s: (g, 0, 0, 0)),
                 pl.BlockSpec((per_step, 8, LANES), lambda g, s: (g, 0, 0))]
    ha, c_out, m_out = pl.pallas_call(
        functools.partial(_mlstm_body, blk=blk, per_step=per_step),
        grid=(groups // per_step, steps),
        in_specs=in_specs,
        out_specs=[pl.BlockSpec((per_step, blk, A_WIDTH), lambda g, s: (g, s, 0)),
                   pl.BlockSpec(state_shape, lambda g, s: (g, 0, 0, 0)),
                   pl.BlockSpec((per_step, 8, LANES), lambda g, s: (g, 0, 0))],
        out_shape=[jax.ShapeDtypeStruct((groups, steps * blk, A_WIDTH), BF16),
                   jax.ShapeDtypeStruct((groups, A_HEADS, A_HEAD_DIM, 2 * A_HEAD_DIM), F32),
                   jax.ShapeDtypeStruct((groups, 8, LANES), F32)],
        scratch_shapes=[pltpu.VMEM(state_shape, F32), pltpu.VMEM((per_step, 8, LANES), F32)],
        compiler_params=_params(("parallel", "arbitrary")),
        name="mlstm",
    )(*args, bias, hnorm, c0, m0)
    return ha.reshape(groups * steps * blk, A_WIDTH), c_out, m_out


def _band_proj_body(x_ref, g_ref, wk_ref, wq_t_ref, wv_t_ref, wkv_t_ref, k_ref, qt_ref, vt_ref, tail_ref,
                    *, tiles_per_seq, p_tiles):
    i = pl.program_id(0)
    h = _rms(x_ref[...], g_ref[...]).astype(BF16)
    step = 512
    for c0 in range(0, C_WIDTH, step):
        k_ref[:, c0:c0 + step] = _dot(h, wk_ref[:, c0:c0 + step]).astype(BF16)
        qt = _dot_nt(wq_t_ref[c0:c0 + step, :], h) * (C_SCALE * LOG2E)
        qt_ref[0, c0:c0 + step, :] = qt.astype(BF16)
        vt = _dot_nt(wv_t_ref[c0:c0 + step, :], h).astype(BF16)
        for n in range(step // C_HEAD_DIM):
            r0 = (c0 // C_HEAD_DIM + n) * BAND_V_ROWS
            vt_ref[0, r0:r0 + C_HEAD_DIM, :] = vt[n * C_HEAD_DIM:(n + 1) * C_HEAD_DIM]
            vt_ref[0, r0 + C_HEAD_DIM:r0 + BAND_V_ROWS, :] = jnp.ones((BAND_V_ROWS - C_HEAD_DIM, vt.shape[1]), BF16)

    is_tail = ((i + 1) % tiles_per_seq == 0) | (i >= p_tiles)

    @pl.when(is_tail)
    def _():
        for c0 in range(0, 2 * C_WIDTH, step):
            tail_ref[0, c0:c0 + step, :] = _dot_nt(wkv_t_ref[c0:c0 + step, :], h)

    @pl.when(jnp.logical_not(is_tail) & (i % tiles_per_seq == 0))
    def _():
        tail_ref[...] = jnp.zeros(tail_ref.shape, F32)


def band_proj(x, g, wk, wq_t, wv_t, wkv_t, tm, tiles_per_seq, p_tiles, n_seq):
    t, d = x.shape
    n_tiles = t // tm
    n_tail = n_seq + n_tiles - p_tiles
    cw = C_WIDTH

    def tail_index(i):
        return (jnp.where(i < p_tiles, i // tiles_per_seq, n_seq + i - p_tiles), 0, 0)

    return pl.pallas_call(
        functools.partial(_band_proj_body, tiles_per_seq=tiles_per_seq, p_tiles=p_tiles),
        grid=(n_tiles,),
        in_specs=[pl.BlockSpec((tm, d), lambda i: (i, 0)), _const_spec((1, d)), _const_spec((d, cw)),
                  _const_spec((cw, d)), _const_spec((cw, d)), _const_spec((2 * cw, d))],
        out_specs=[pl.BlockSpec((tm, cw), lambda i: (i, 0)), pl.BlockSpec((1, cw, tm), lambda i: (i, 0, 0)),
                   pl.BlockSpec((1, BAND_VT_ROWS, tm), lambda i: (i, 0, 0)),
                   pl.BlockSpec((1, 2 * cw, tm), tail_index)],
        out_shape=[jax.ShapeDtypeStruct((t, cw), BF16), jax.ShapeDtypeStruct((n_tiles, cw, tm), BF16),
                   jax.ShapeDtypeStruct((n_tiles, BAND_VT_ROWS, tm), BF16),
                   jax.ShapeDtypeStruct((n_tail, 2 * cw, tm), F32)],
        compiler_params=_params(("arbitrary",)),
        name="band_proj",
    )(x, g, wk, wq_t, wv_t, wkv_t)


def _band_table_body(base_ref, tab_ref):
    tq = BAND_TILE
    shape = (tq, BAND_ROLL_WIDTH)
    row = lax.broadcasted_iota(jnp.int32, shape, 0)
    tab = jnp.broadcast_to(base_ref[0], shape)
    shift = 1
    while shift < tq:
        tab = jnp.where((row & shift) != 0, pltpu.roll(tab, shift, 1), tab)
        shift *= 2
    win = lax.broadcasted_iota(jnp.int32, (tq, BAND_WINDOW), 1) // CHUNK
    qch = lax.broadcasted_iota(jnp.int32, (tq, BAND_WINDOW), 0) // CHUNK
    valid = (win >= qch) & (win <= qch + LEFT_CHUNKS)
    tab_ref[0] = jnp.transpose(jnp.where(valid, tab[:, :BAND_WINDOW] * LOG2E, NEG))


def band_table(base_rows):
    return pl.pallas_call(
        _band_table_body,
        grid=(C_HEADS,),
        in_specs=[pl.BlockSpec((1, 1, BAND_ROLL_WIDTH), lambda h: (h, 0, 0))],
        out_specs=pl.BlockSpec((1, BAND_WINDOW, BAND_TILE), lambda h: (h, 0, 0)),
        out_shape=jax.ShapeDtypeStruct((C_HEADS, BAND_WINDOW, BAND_TILE), F32),
        compiler_params=_params(("parallel",)),
        name="band_table",
    )(base_rows)


def _band_body(qt_ref, *refs, clamp_start, cache_values):
    nb = BAND_WINDOW // BAND_TILE
    k_refs = refs[:nb]
    v_refs = refs[nb:2 * nb]
    tab_ref, o_ref, acc_ref = refs[2 * nb:]
    tq = BAND_TILE
    i = pl.program_id(1)
    row_half = lax.broadcasted_iota(jnp.int32, (LANES, tq), 0) // C_HEAD_DIM
    ones = jnp.ones((16, tq), BF16)

    def value_rows(j, h):
        if cache_values and j < nb - 1:
            rows = slice(h * C_HEAD_DIM, (h + 1) * C_HEAD_DIM)
            v = v_refs[j][0, 0, rows, :].astype(BF16)
            return jnp.concatenate([v, ones], axis=0)
        return v_refs[j][0, h * BAND_V_ROWS:(h + 1) * BAND_V_ROWS, :]

    def scores(pair, first_tiles):
        q_pair = qt_ref[0, pair * LANES:(pair + 1) * LANES, :]
        zero = jnp.zeros_like(q_pair)
        q_two = jnp.concatenate([jnp.where(row_half == 0, q_pair, zero), jnp.where(row_half == 1, q_pair, zero)],
                                axis=1)
        parts = []
        for j in range(nb):
            if cache_values and j < nb - 1:
                k_t = k_refs[j][0, 0, pair * LANES:(pair + 1) * LANES, :]
                s_j = _dot_tn(k_t.astype(BF16), q_two)
            else:
                s_j = _dot(k_refs[j][:, pair * LANES:(pair + 1) * LANES], q_two)
            if first_tiles and j < nb - 1:
                s_j = jnp.where(i + j < nb - 1, NEG, s_j)
            blk = slice(j * BAND_TILE, (j + 1) * BAND_TILE)
            parts.append(s_j + jnp.concatenate([tab_ref[2 * pair, blk, :], tab_ref[2 * pair + 1, blk, :]], axis=1))
        return parts

    def attend(pair, parts, m):
        probs = [jnp.exp2(part - m).astype(BF16) for part in parts]
        for half in range(2):
            h = 2 * pair + half
            rows = slice(h * C_HEAD_DIM, (h + 1) * C_HEAD_DIM)
            pv = None
            for j in range(nb):
                part = _dot(value_rows(j, h), probs[j][:, half * tq:(half + 1) * tq])
                pv = part if pv is None else pv + part
            acc_ref[rows, :] = pv[:C_HEAD_DIM] / pv[C_HEAD_DIM:C_HEAD_DIM + 1]

    def run(first_tiles):
        n_pairs = C_HEADS // 2
        s_next = scores(0, first_tiles)
        for pair in range(n_pairs):
            parts = s_next
            if pair + 1 < n_pairs:
                s_next = scores(pair + 1, first_tiles)
            m = parts[0]
            for j in range(1, nb):
                m = jnp.maximum(m, parts[j])
            attend(pair, parts, jnp.max(m, axis=0, keepdims=True))

    if clamp_start:
        @pl.when(i < nb - 1)
        def _():
            run(True)

        @pl.when(i >= nb - 1)
        def _():
            run(False)
    else:
        run(False)
    o_ref[...] = jnp.transpose(acc_ref[...]).astype(o_ref.dtype)


def band_attn(qt_arr, qt_map, k_arrs, k_specs, v_arrs, v_specs, tab, groups, tiles, clamp_start, cache_values):
    tq = BAND_TILE
    in_specs = [pl.BlockSpec((1, C_WIDTH, tq), qt_map)]
    in_specs += list(k_specs)
    in_specs += list(v_specs)
    in_specs.append(_const_spec(tab.shape))
    return pl.pallas_call(
        functools.partial(_band_body, clamp_start=clamp_start, cache_values=cache_values),
        grid=(groups, tiles),
        in_specs=in_specs,
        out_specs=pl.BlockSpec((tq, C_WIDTH), lambda g, i: (g * tiles + i, 0)),
        out_shape=jax.ShapeDtypeStruct((groups * tiles * tq, C_WIDTH), BF16),
        scratch_shapes=[pltpu.VMEM((C_WIDTH, tq), F32)],
        compiler_params=_params(("parallel", "arbitrary")),
        name="band_attn",
    )(qt_arr, *k_arrs, *v_arrs, tab)


def _cache_roll_body(c_ref, *refs, new_frames):
    tail_refs = refs[:-1]
    o_ref = refs[-1]
    layer = pl.program_id(0)
    stream = pl.program_id(1)
    buf = c_ref.shape[3]
    rolled = pltpu.roll(c_ref[0, 0], buf - new_frames, 1)
    new = tail_refs[0][0]
    for n in range(1, len(tail_refs)):
        new = jnp.where(layer == n, tail_refs[n][0], new)
    upper_half = (stream % (LANES // new_frames)) == 1
    new = jnp.where(upper_half, new, pltpu.roll(new, new_frames, 1))
    o_ref[0, 0] = rolled
    lane = lax.broadcasted_iota(jnp.int32, new.shape, 1)
    o_ref[0, 0, :, buf - LANES:] = jnp.where(lane < LANES - new_frames, rolled[:, buf - LANES:], new)


def cache_roll(cache_t, tails, row_block, first_tile, new_frames):
    layers, streams, feat, buf = cache_t.shape
    tm = tails[0].shape[2]
    per_tile = tm // new_frames
    groups = LANES // new_frames
    assert 2 * new_frames == LANES and layers == len(tails)

    def tail_map(l, s):
        return (first_tile + s // per_tile, row_block, (s % per_tile) // groups)

    return pl.pallas_call(
        functools.partial(_cache_roll_body, new_frames=new_frames),
        grid=(layers, streams),
        in_specs=[pl.BlockSpec((1, 1, feat, buf), lambda l, s: (l, s, 0, 0))]
                 + [pl.BlockSpec((1, feat, LANES), tail_map)] * layers,
        out_specs=pl.BlockSpec((1, 1, feat, buf), lambda l, s: (l, s, 0, 0)),
        out_shape=jax.ShapeDtypeStruct(cache_t.shape, cache_t.dtype),
        compiler_params=_params(("parallel", "parallel")),
        name="cache_roll",
    )(cache_t, *tails)


def _rope_tables(positions):
    half = B_ROPE // 2
    inv = ROPE_THETA ** (-jnp.arange(half, dtype=F32) / half)
    ang = positions.astype(F32)[:, None] * inv[None, :]
    cos, sin = jnp.cos(ang), jnp.sin(ang)
    n = positions.shape[0]
    zeros = jnp.zeros((n, LANES - B_ROPE), F32)
    zh = jnp.zeros((n, half), F32)
    rc = jnp.concatenate([cos, cos, zeros], axis=1)
    rs1 = jnp.concatenate([zh, sin, zeros], axis=1)
    rs2 = jnp.concatenate([-sin, zh, zeros], axis=1)
    return rc, rs1, rs2, cos.T, sin.T


def _band_base_rows(rel_bias):
    x = jnp.arange(BAND_ROLL_WIDTH)
    rel = jnp.where(x < BAND_WINDOW, BAND_LEFT - x, BAND_LEFT + 1)
    idx = jnp.clip(rel, -MAX_REL, MAX_REL) + MAX_REL
    return rel_bias[:, None, idx]


def kernel(x_prompt, x_sample, cache_mla_ckv, cache_mla_krope, state_mlstm_C, state_mlstm_n, state_mlstm_m,
           cache_band_k, cache_band_v, norm_mix, norm_ffn, norm_final, w_in_ab, b_gates, mlstm_hnorm,
           mla_q_norm, mla_kv_norm, mla_w_uq, mla_w_uk, mla_w_uv, w_out_ab, w_qkv_c, w_out_c, rel_bias_c,
           w_gate, w_up, w_down):
    batch, seq, d = x_prompt.shape
    dec_batch, dec_seq, _ = x_sample.shape
    past = cache_mla_ckv.shape[2]
    band_buf = cache_band_k.shape[2]
    depth = norm_mix.shape[0]
    tp = batch * seq
    ts = dec_batch * dec_seq
    t = tp + ts
    tm_big = 1024
    tm = MLA_Q_TILE
    tk = MLA_KV_BLOCK
    assert d == D_MODEL and dec_seq == CHUNK and band_buf == BAND_LEFT and past % tk == 0
    assert tp % tm_big == 0 and ts % tm_big == 0 and seq % tm == 0 and tm % dec_seq == 0 and tm % tk == 0

    assert past % CHUNK == 0 and ts == tm_big
    x_parts = (x_prompt.reshape(tp, d), x_sample.reshape(ts, d))

    pos_tab = jnp.concatenate([jnp.arange(seq, dtype=jnp.int32),
                               past + (jnp.arange(tm, dtype=jnp.int32) % dec_seq)])
    rope_tabs = _rope_tables(pos_tab)
    head_pat = jnp.concatenate([jnp.ones((dec_seq, B_NOPE), F32), jnp.zeros((dec_seq, LANES - B_NOPE), F32)], axis=1)
    roll_pad = ((0, 0), (B_NOPE, LANES - B_QK))
    rope_full = tuple(jnp.tile(tab, (1, B_HEADS)) for tab in (
        head_pat + jnp.pad(rope_tabs[0][seq:seq + dec_seq, :B_ROPE], roll_pad),
        jnp.pad(rope_tabs[1][seq:seq + dec_seq, :B_ROPE], roll_pad),
        jnp.pad(rope_tabs[2][seq:seq + dec_seq, :B_ROPE], roll_pad)))
    sel = jnp.pad(jnp.eye(B_ROPE, dtype=F32), ((B_NOPE, MLA_HEAD_PAD - B_QK), (0, 0))).astype(BF16)
    cache_krope_t = jnp.swapaxes(cache_mla_krope, 2, 3)

    def frames_last(c):
        return jnp.transpose(c, (0, 1, 3, 4, 2)).reshape(c.shape[0], dec_batch, C_WIDTH, band_buf)

    cache_k_t = frames_last(cache_band_k)
    cache_v_t = frames_last(cache_band_v)
    band_tails = []
    p_tiles = tp // tm
    seq_tiles = seq // tm

    def tab_index(i):
        return jnp.where(i < p_tiles, i % seq_tiles, seq_tiles)

    outs = {k: [] for k in ("p_ckv", "p_kr", "p_C", "p_n", "p_m", "p_bk", "p_bv",
                            "s_ckv", "s_kr", "s_C", "s_n", "s_m", "s_bk", "s_bv")}
    a4 = 4 * A_WIDTH
    for layer in range(depth):
        j = layer // 2
        g_mix = norm_mix[layer][None, :]
        g_ffn = norm_ffn[layer][None, :]
        last = layer == depth - 1
        g_fin = norm_final[None, :] if last else None
        ffn_w = (w_gate[layer].astype(BF16), w_up[layer].astype(BF16), w_down[layer].astype(BF16))
        if layer % 2 == 0:
            w = w_in_ab[j]
            gate_cols = jnp.concatenate([w[:, a4 + 2 * A_HEADS + Q_RANK + KV_RANK:], w[:, a4:a4 + 2 * A_HEADS],
                                         jnp.zeros((d, LANES - B_ROPE - 2 * A_HEADS), F32)], axis=1)
            w_all = jnp.concatenate([w[:, :a4], w[:, a4 + 2 * A_HEADS:a4 + 2 * A_HEADS + Q_RANK + KV_RANK],
                                     gate_cols], axis=1).astype(BF16)
            aw = A_WIDTH
            plan = [(0, aw, [(0, 0, 1.0)]), (aw, aw, [(0, aw, A_HEAD_DIM ** -0.5)]), (2 * aw, aw, [(0, 2 * aw, 1.0)]),
                    (3 * aw, aw, [(1, 0, 1.0)]), (4 * aw, Q_RANK, [(2, SM_CQ, 1.0)]),
                    (4 * aw + Q_RANK, KV_RANK + LANES, [(2, SM_CKV, 1.0)])]
            qkv, o32, small = norm_proj(x_parts, g_mix, w_all, plan, (3 * aw, aw, SM_WIDTH), (BF16, F32, F32), tm_big)

            wq = mla_w_uq[j].reshape(Q_RANK, B_HEADS, B_QK)
            wq_t = jnp.pad(wq, ((0, 0), (0, 0), (0, MLA_HEAD_PAD - B_QK))).reshape(Q_RANK, -1).T.astype(BF16)
            ckv, krope, qt = mla_prep_q(small, mla_q_norm[j][None, :], mla_kv_norm[j][None, :], wq_t,
                                        rope_tabs, tab_index, tm)
            wk_pad = jnp.pad(mla_w_uk[j], ((0, 0), (0, 0), (0, MLA_HEAD_PAD - B_NOPE))).reshape(KV_RANK, -1).astype(BF16)
            place = jnp.pad(jnp.eye(B_ROPE, dtype=F32), ((0, 0), (B_NOPE, MLA_HEAD_PAD - B_QK)))
            place = jnp.tile(place, (1, B_HEADS)).astype(BF16)
            wv_t = mla_w_uv[j].reshape(KV_RANK, B_WIDTH).T.astype(BF16)
            kp_p, vt_p = mla_prep_kv(ckv, krope, wk_pad, place, wv_t, tp // tk, tk)
            ckv_s = ckv[tp:].reshape(dec_batch, dec_seq, KV_RANK)
            kr_s = krope[tp:].reshape(dec_batch, dec_seq, B_ROPE)

            hb_p = mla_attn(qt, kp_p, vt_p, batch, seq_tiles, tm, tk, seq // tk, 0, tm)
            wabs = jnp.pad(jnp.transpose(mla_w_uk[j], (1, 2, 0)),
                           ((0, 0), (0, MLA_HEAD_PAD - B_NOPE), (0, 0))).astype(BF16)
            wuv_place = jnp.einsum("rhv,hg->hrgv", mla_w_uv[j], jnp.eye(B_HEADS, dtype=F32))
            wuv_place = wuv_place.reshape(B_HEADS, KV_RANK, B_WIDTH).astype(BF16)
            hb_s = mla_sample(small, ckv, krope, cache_mla_ckv, cache_krope_t, j, mla_q_norm[j][None, :],
                              wq_t.T, rope_full, wabs, sel, wuv_place, dec_batch, dec_seq, tp)

            bias = jnp.zeros((1, LANES), F32).at[0, GATE_I_LANE:GATE_I_LANE + 2 * A_HEADS].set(b_gates[j])
            hn = mlstm_hnorm[j][None, :]
            c0_p = jnp.zeros((batch, A_HEADS, A_HEAD_DIM, 2 * A_HEAD_DIM), F32)
            m0_p = jnp.zeros((batch, 8, LANES), F32)
            blk_p = 256
            ha_p, c_p, m_p = mlstm(qkv, o32, small, bias, hn, c0_p, m0_p, batch, seq // blk_p, blk_p, 0, batch)
            n_rep = jnp.broadcast_to(state_mlstm_n[j][..., None], (dec_batch, A_HEADS, A_HEAD_DIM, A_HEAD_DIM))
            c0_s = jnp.concatenate([state_mlstm_C[j], n_rep], axis=-1)
            m0_s = jnp.broadcast_to(jnp.pad(state_mlstm_m[j], ((0, 0), (0, 8 - A_HEADS)))[..., None],
                                    (dec_batch, 8, LANES))
            ha_s, c_s, m_s = mlstm(qkv, o32, small, bias, hn, c0_s, m0_s, dec_batch, 1, dec_seq, tp, 2)

            wo = w_out_ab[j].astype(BF16)
            x_parts = mix_ffn(x_parts, ((ha_p, ha_s), (hb_p, hb_s)), (wo[:A_WIDTH], wo[A_WIDTH:]), g_ffn, *ffn_w,
                              g_fin, tm_big, tp, False)
            x_parts = (x_parts,)

            outs["p_ckv"].append(ckv[:tp].reshape(batch, seq, KV_RANK))
            outs["p_kr"].append(krope[:tp].reshape(batch, seq, B_ROPE))
            outs["p_C"].append(c_p[..., :A_HEAD_DIM])
            outs["p_n"].append(c_p[..., A_HEAD_DIM])
            outs["p_m"].append(m_p[:, :A_HEADS, 0])
            outs["s_ckv"].append(ckv_s)
            outs["s_kr"].append(kr_s)
            outs["s_C"].append(c_s[..., :A_HEAD_DIM])
            outs["s_n"].append(c_s[..., A_HEAD_DIM])
            outs["s_m"].append(m_s[:, :A_HEADS, 0])
        else:
            cw = C_WIDTH
            wqkv = w_qkv_c[j].astype(BF16)
            tmb = 1024
            tps = seq // tmb
            ptb = tp // tmb
            assert band_buf <= tmb and seq % tmb == 0 and ts % tmb == 0 and tmb % BAND_TILE == 0
            (x,) = x_parts
            k_bf, qt_all, vt_all, kv_tail = band_proj(x, g_mix, wqkv[:, cw:2 * cw], wqkv[:, :cw].T,
                                                       wqkv[:, 2 * cw:].T, wqkv[:, cw:].T, tmb, tps, ptb, batch)
            tab = band_table(_band_base_rows(rel_bias_c[j]))
            tq = BAND_TILE
            nb = BAND_WINDOW // tq
            sub = tmb // tq
            tiles = seq // tq

            def block(g, i, jj):
                return g * tiles + jnp.maximum(i + jj - (nb - 1), 0)

            o_p = band_attn(
                qt_all, lambda g, i: ((g * tiles + i) // sub, 0, (g * tiles + i) % sub),
                [k_bf] * nb,
                [pl.BlockSpec((tq, cw), functools.partial(lambda g, i, jj: (block(g, i, jj), 0), jj=jj))
                 for jj in range(nb)],
                [vt_all] * nb,
                [pl.BlockSpec((1, BAND_VT_ROWS, tq), functools.partial(
                    lambda g, i, jj: (block(g, i, jj) // sub, 0, block(g, i, jj) % sub), jj=jj)) for jj in range(nb)],
                tab, batch, tiles, True, False)

            def per_stream_t(a):
                rows = a.shape[1]
                a = a.reshape(-1, rows, tmb // dec_seq, dec_seq)
                return jnp.moveaxis(a, 2, 1).reshape(dec_batch, rows, dec_seq)

            qt_s = jnp.pad(per_stream_t(qt_all[ptb:]), ((0, 0), (0, 0), (0, tq - dec_seq)))
            k_new = jnp.pad(k_bf[tp:].reshape(dec_batch, dec_seq, cw), ((0, 0), (0, tq - dec_seq), (0, 0)))
            vt_new = jnp.pad(per_stream_t(vt_all[ptb:]), ((0, 0), (0, 0), (0, tq - dec_seq)))
            cache_spec = [pl.BlockSpec((1, 1, cw, tq), functools.partial(lambda g, i, jj: (j, g, 0, jj), jj=jj))
                          for jj in range(nb - 1)]
            o_s = band_attn(
                qt_s, lambda g, i: (g, 0, 0),
                [cache_k_t] * (nb - 1) + [k_new.reshape(-1, cw)],
                cache_spec + [pl.BlockSpec((tq, cw), lambda g, i: (g, 0))],
                [cache_v_t] * (nb - 1) + [vt_new],
                cache_spec + [pl.BlockSpec((1, BAND_VT_ROWS, tq), lambda g, i: (g, 0, 0))],
                tab, dec_batch, 1, False, True)
            o_s = o_s.reshape(dec_batch, tq, cw)[:, :dec_seq].reshape(ts, cw)
            x_parts = mix_ffn(x_parts, ((o_p, o_s),), (w_out_c[j].astype(BF16),), g_ffn, *ffn_w, g_fin, tm_big,
                              tp, last)
            x_parts = tuple(x_parts) if last else (x_parts,)

            kv_p = kv_tail[:batch, :, tmb - band_buf:].reshape(batch, 2, C_HEADS, C_HEAD_DIM, band_buf)
            outs["p_bk"].append(jnp.transpose(kv_p[:, 0], (0, 3, 1, 2)))
            outs["p_bv"].append(jnp.transpose(kv_p[:, 1], (0, 3, 1, 2)))
            band_tails.append(kv_tail)

    y_prompt = x_parts[0].reshape(batch, seq, d)
    y_sample = x_parts[1].reshape(dec_batch, dec_seq, d)

    def frames_first(c):
        return jnp.transpose(c.reshape(-1, dec_batch, C_HEADS, C_HEAD_DIM, band_buf), (0, 1, 4, 2, 3))

    s_bk = frames_first(cache_roll(cache_k_t, band_tails, 0, batch, dec_seq))
    s_bv = frames_first(cache_roll(cache_v_t, band_tails, 1, batch, dec_seq))
    st = {k: jnp.stack(v) for k, v in outs.items() if v}
    return (y_prompt, y_sample, st["p_ckv"], st["p_kr"], st["p_C"], st["p_n"], st["p_m"], st["p_bk"], st["p_bv"],
            st["s_ckv"], st["s_kr"], st["s_C"], st["s_n"], st["s_m"], s_bk, s_bv)
```
